```python
import jax
import jax.numpy as jnp
from jax import lax
import numpy as np

D_MODEL = 2048
BATCH = 4
SEQ = 2048
DEPTH = 4
DEC_BATCH = 8
DEC_SEQ = 1
PAST_LEN = 16384
PAGE_SIZE = 128

N_MIXERS = 4
HEAD_DIM = 128
RMS_EPS = 1e-6
ROPE_THETA = 10000.0
A_HEADS = D_MODEL // HEAD_DIM
A_PATTERNS = ((128, 1), (512, 4), (2048, 16))
A_BLOCK = 128
CONV_W = 3
C_HEADS = D_MODEL // HEAD_DIM
C_KV_HEADS = 4
C_GROUP = C_HEADS // C_KV_HEADS
C_CMP_LEN = 32
C_CMP_STRIDE = 16
C_CMP_HIDDEN = 256
C_SEL_BLOCK = 64
C_SEL_TOPN = 16
C_WINDOW = 512
C_QCHUNK = 64
C_FORCE_SCORE = 1e4
C_IN = C_HEADS * HEAD_DIM + 3 * 2 * C_KV_HEADS * HEAD_DIM + 3 * C_HEADS
D_POOLS = (2, 4, 8, 16)
D_GROUP_W = D_MODEL // len(D_POOLS)
D_HIST = max(D_POOLS) - 1
D_FF = -(-8 * D_MODEL // (3 * 256)) * 256
PLE_DIM = 256
N_LAYERS_A = len(range(0, DEPTH, N_MIXERS))
N_LAYERS_B = len(range(1, DEPTH, N_MIXERS))
N_LAYERS_C = len(range(2, DEPTH, N_MIXERS))
N_LAYERS_D = len(range(3, DEPTH, N_MIXERS))
STATE_NAMES = ('a_w1', 'a_w2', 'a_w3', 'b_conv', 'c_cmp', 'c_slc', 'c_win', 'd_pool')

kernel_name = 'hybrid_dilated_conv_nsa_pool_decoder_step'


def rms_norm(x, g):
    xf = x.astype(jnp.float32)
    y = xf * lax.rsqrt(jnp.mean(xf * xf, axis=-1, keepdims=True) + RMS_EPS)
    return (y * g.astype(jnp.float32)).astype(x.dtype)


def rope(x, pos):
    dh = x.shape[-1]
    half = dh // 2
    inv = ROPE_THETA ** (-(jnp.arange(half, dtype=jnp.float32) * 2.0 / dh))
    ang = pos.astype(jnp.float32)[:, None] * inv[None, :]
    cos, sin = jnp.cos(ang)[:, None, :], jnp.sin(ang)[:, None, :]
    xf = x.astype(jnp.float32)
    x1, x2 = xf[..., :half], xf[..., half:]
    return jnp.concatenate([x1 * cos - x2 * sin, x2 * cos + x1 * sin], axis=-1).astype(x.dtype)


def masked_softmax(s, valid):
    s = jnp.where(valid, s, -jnp.inf)
    m = jnp.max(s, axis=-1, keepdims=True)
    m = jnp.where(jnp.isfinite(m), m, 0.0)
    e = jnp.where(valid, jnp.exp(s - m), 0.0)
    return e / jnp.maximum(jnp.sum(e, axis=-1, keepdims=True), 1e-30)


def attend(spec, p, v):
    return jnp.einsum(spec, p.astype(v.dtype), v, preferred_element_type=jnp.float32)


def a_qkv(h, pos, w_qkv, q_gain, k_gain):
    b, s, _ = h.shape
    z = (h @ w_qkv).reshape(b, s, len(A_PATTERNS), 3, A_HEADS, HEAD_DIM)
    z = jnp.moveaxis(z, (2, 3), (0, 1))
    q = rope(rms_norm(z[:, 0], q_gain[:, None, None, None, :]), pos)
    k = rope(rms_norm(z[:, 1], k_gain[:, None, None, None, :]), pos)
    return q, k, z[:, 2]


def a_group_prompt(q, k, v, window, dil):
    b, s, h, dh = q.shape
    n_sub = s // dil
    n_pad = -(-n_sub // A_BLOCK) * A_BLOCK
    nb = n_pad // A_BLOCK
    bd = b * dil

    def by_residue(t):
        t = t.reshape(b, n_sub, dil, h, dh).transpose(0, 2, 1, 3, 4).reshape(bd, n_sub, h, dh)
        return jnp.pad(t, ((0, 0), (0, n_pad - n_sub), (0, 0), (0, 0)))

    def band(t):
        t = jnp.pad(by_residue(t), ((0, 0), (A_BLOCK, 0), (0, 0), (0, 0))).reshape(bd, nb + 1, A_BLOCK, h, dh)
        return jnp.concatenate([t[:, :-1], t[:, 1:]], axis=2)

    qb = by_residue(q).reshape(bd, nb, A_BLOCK, h, dh)
    kb, vb = band(k), band(v)
    sc = jnp.einsum('bnqhd,bnkhd->bnhqk', qb, kb, preferred_element_type=jnp.float32) * (dh ** -0.5)
    qi = jnp.arange(A_BLOCK)[:, None]
    ki = jnp.arange(2 * A_BLOCK)[None, :]
    dist = qi + A_BLOCK - ki
    kidx = jnp.arange(nb)[:, None, None] * A_BLOCK - A_BLOCK + ki
    valid = (dist >= 0) & (dist <= window // dil) & (kidx >= 0)
    sc = jnp.where(valid[None, :, None], sc, -jnp.inf)
    lse = jax.nn.logsumexp(sc, axis=-1)
    o = attend('bnhqk,bnkhd->bnqhd', jnp.exp(sc - lse[..., None]), vb)
    o = o.reshape(bd, n_pad, h, dh)[:, :n_sub].reshape(b, dil, n_sub, h, dh)
    o = o.transpose(0, 2, 1, 3, 4).reshape(b, s, h, dh)
    lse = lse.transpose(0, 1, 3, 2).reshape(bd, n_pad, h)[:, :n_sub].reshape(b, dil, n_sub, h)
    lse = lse.transpose(0, 2, 1, 3).reshape(b, s, h)
    return o, lse


def a_group_sample(q, k, v, buf, window, dil):
    n = q.shape[1]
    lb = buf.shape[1]
    kc = jnp.concatenate([buf[:, :, 0], k], axis=1)
    vc = jnp.concatenate([buf[:, :, 1], v], axis=1)
    idx = lb + jnp.arange(n)[:, None] - dil * jnp.arange(window // dil + 1)[None, :]
    valid = idx >= 0
    idx = jnp.maximum(idx, 0)
    kg, vg = kc[:, idx], vc[:, idx]
    sc = jnp.einsum('bjhd,bjmhd->bjhm', q, kg, preferred_element_type=jnp.float32) * (HEAD_DIM ** -0.5)
    sc = jnp.where(valid[None, :, None, :], sc, -jnp.inf)
    lse = jax.nn.logsumexp(sc, axis=-1)
    o = attend('bjhm,bjmhd->bjhd', jnp.exp(sc - lse[..., None]), vg)
    new_buf = jnp.concatenate([buf, jnp.stack([k, v], axis=2)], axis=1)[:, -lb:]
    return o, lse, new_buf


def a_merge(outs, lses, w_out):
    wts = jax.nn.softmax(jnp.stack(lses, axis=0), axis=0)
    o = jnp.sum(wts[..., None] * jnp.stack(outs, axis=0), axis=0)
    b, s = o.shape[:2]
    return o.reshape(b, s, -1).astype(w_out.dtype) @ w_out


def mixer_a_prompt(h, pos, w_qkv, q_gain, k_gain, w_out):
    q, k, v = a_qkv(h, pos, w_qkv, q_gain, k_gain)
    s = h.shape[1]
    outs, lses, bufs = [], [], []
    for g, (win, dil) in enumerate(A_PATTERNS):
        o, l = a_group_prompt(q[g], k[g], v[g], win, dil)
        outs.append(o)
        lses.append(l)
        bufs.append(jnp.stack([k[g], v[g]], axis=2)[:, -min(win, s):])
    return a_merge(outs, lses, w_out), bufs


def mixer_a_sample(h, pos, past_bufs, w_qkv, q_gain, k_gain, w_out):
    q, k, v = a_qkv(h, pos, w_qkv, q_gain, k_gain)
    outs, lses, bufs = [], [], []
    for g, (win, dil) in enumerate(A_PATTERNS):
        o, l, nb = a_group_sample(q[g], k[g], v[g], past_bufs[g], win, dil)
        outs.append(o)
        lses.append(l)
        bufs.append(nb)
    return a_merge(outs, lses, w_out), bufs


def mixer_b(h, hist, w_in, conv_w, w_out):
    n = h.shape[1]
    bg, cg, xt = jnp.split(h @ w_in, 3, axis=-1)
    u = cg * xt
    uh = jnp.concatenate([hist, u], axis=1)
    y = conv_w[0] * uh[:, 0:n] + conv_w[1] * uh[:, 1:n + 1] + conv_w[2] * uh[:, 2:n + 2]
    return (bg * y) @ w_out, uh[:, -(CONV_W - 1):]


def c_project(h, pos, w_in, q_gain, k_gain):
    b, s, _ = h.shape
    nq = C_HEADS * HEAD_DIM
    nkv = 3 * 2 * C_KV_HEADS * HEAD_DIM
    z = h @ w_in
    q = rms_norm(z[..., :nq].reshape(b, s, C_HEADS, HEAD_DIM), q_gain)
    kv = z[..., nq:nq + nkv].reshape(b, s, 3, 2, C_KV_HEADS, HEAD_DIM)
    gates = jax.nn.sigmoid(z[..., nq + nkv:].astype(jnp.float32)).reshape(b, s, C_HEADS, 3)

    def roped(br):
        return jnp.stack([rope(rms_norm(kv[:, :, br, 0], k_gain[br]), pos), kv[:, :, br, 1]], axis=2)

    return q, rope(q, pos), gates, kv[:, :, 0], roped(1), roped(2)


def c_compress(cmp_kv, pe, w1, w2, k_gain0):
    b, t = cmp_kv.shape[:2]
    nch = t // C_CMP_STRIDE
    ch = cmp_kv[:, :nch * C_CMP_STRIDE].reshape(b, nch, C_CMP_STRIDE, 2, C_KV_HEADS, HEAD_DIM)
    first = jnp.einsum('bnpckd,cpdh->bnckh', ch, w1[:, :C_CMP_STRIDE])
    second = jnp.einsum('bnpckd,cpdh->bnckh', ch, w1[:, C_CMP_STRIDE:])
    pe_term = jnp.einsum('cpd,cpdh->ch', pe, w1)
    hid = jax.nn.silu(first[:, :-1] + second[:, 1:] + pe_term[:, None, :])
    out = jnp.einsum('bnckh,chd->bnckd', hid, w2)
    return rms_norm(out[:, :, 0], k_gain0), out[:, :, 1]


def c_sel_blocks(slc_kv):
    b, t = slc_kv.shape[:2]
    tp = -(-t // C_SEL_BLOCK) * C_SEL_BLOCK
    x = jnp.pad(slc_kv, ((0, 0), (0, tp - t), (0, 0), (0, 0), (0, 0)))
    return x.reshape(b, tp // C_SEL_BLOCK, C_SEL_BLOCK, 2, C_KV_HEADS, HEAD_DIM).transpose(0, 3, 4, 1, 2, 5)


def nsa_core(q, qr, gate, t, kc, vc, ks_blk, vs_blk, kw, vw, kw_pos):
    qn = q.shape[0]
    scale = HEAD_DIM ** -0.5
    qg = q.reshape(qn, C_KV_HEADS, C_GROUP, HEAD_DIM)
    qrg = qr.reshape(qn, C_KV_HEADS, C_GROUP, HEAD_DIM)
    ncmp = kc.shape[0]
    cmp_start = jnp.arange(ncmp) * C_CMP_STRIDE
    sc = jnp.einsum('qkgd,nkd->kgqn', qg, kc, preferred_element_type=jnp.float32) * scale
    p_cmp = masked_softmax(sc, (cmp_start + C_CMP_LEN - 1)[None, :] <= t[:, None])
    o_cmp = attend('kgqn,nkd->qkgd', p_cmp, vc)
    nsel = ks_blk.shape[1]
    sel_start = jnp.arange(nsel) * C_SEL_BLOCK
    overlap = ((cmp_start[:, None] < sel_start[None, :] + C_SEL_BLOCK)
               & (cmp_start[:, None] + C_CMP_LEN > sel_start[None, :])).astype(jnp.float32)
    imp = jnp.einsum('kgqn,nj->kqj', p_cmp, overlap)
    cur = (t // C_SEL_BLOCK)[:, None]
    blk = jnp.arange(nsel)[None, :]
    forced = (blk == 0) | (blk == cur) | (blk == cur - 1)
    imp = jnp.where(blk <= cur, jnp.where(forced, C_FORCE_SCORE, imp), -jnp.inf)
    top_val, top_idx = lax.top_k(imp, min(C_SEL_TOPN, nsel))
    kg = jax.vmap(lambda bl, ix: bl[ix])(ks_blk, top_idx)
    vg = jax.vmap(lambda bl, ix: bl[ix])(vs_blk, top_idx)
    kpos = top_idx[..., None] * C_SEL_BLOCK + jnp.arange(C_SEL_BLOCK)
    kvalid = jnp.isfinite(top_val)[..., None] & (kpos <= t[None, :, None, None])
    nk = kg.shape[2] * C_SEL_BLOCK
    kg = kg.reshape(C_KV_HEADS, qn, nk, HEAD_DIM)
    vg = vg.reshape(C_KV_HEADS, qn, nk, HEAD_DIM)
    sc = jnp.einsum('qkgd,kqmd->kgqm', qrg, kg, preferred_element_type=jnp.float32) * scale
    o_slc = attend('kgqm,kqmd->qkgd', masked_softmax(sc, kvalid.reshape(C_KV_HEADS, 1, qn, nk)), vg)
    sc = jnp.einsum('qkgd,mkd->kgqm', qrg, kw, preferred_element_type=jnp.float32) * scale
    dist = t[:, None] - kw_pos[None, :]
    wvalid = (dist >= 0) & (dist < C_WINDOW) & (kw_pos >= 0)[None, :]
    o_win = attend('kgqm,mkd->qkgd', masked_softmax(sc, wvalid), vw)
    g = gate.astype(jnp.float32).reshape(qn, C_KV_HEADS, C_GROUP, 3, 1)
    o = g[..., 0, :] * o_cmp + g[..., 1, :] * o_slc + g[..., 2, :] * o_win
    return o.reshape(qn, C_HEADS, HEAD_DIM).astype(q.dtype)


def mixer_c_prompt(h, pos, w_in, q_gain, k_gain, pe, w1, w2, w_out):
    b, s, _ = h.shape
    q, qr, gates, cmp_kv, slc_kv, win_kv = c_project(h, pos, w_in, q_gain, k_gain)
    kc, vc = c_compress(cmp_kv, pe, w1, w2, k_gain[0])
    slc_blk = c_sel_blocks(slc_kv)
    win_pad = jnp.pad(win_kv, ((0, 0), (C_WINDOW, 0), (0, 0), (0, 0), (0, 0)))
    nq = s // C_QCHUNK

    def step(i):
        bi, t0 = i // nq, (i % nq) * C_QCHUNK
        sl = lambda a, n: lax.dynamic_slice_in_dim(a[bi], t0, n, 0)
        kw = sl(win_pad, C_QCHUNK + C_WINDOW)
        t = t0 + jnp.arange(C_QCHUNK)
        kw_pos = t0 - C_WINDOW + jnp.arange(C_QCHUNK + C_WINDOW)
        return nsa_core(sl(q, C_QCHUNK), sl(qr, C_QCHUNK), sl(gates, C_QCHUNK), t, kc[bi], vc[bi],
                        slc_blk[bi, 0], slc_blk[bi, 1], kw[:, 0], kw[:, 1], kw_pos)

    o = lax.map(step, jnp.arange(b * nq))
    out = o.reshape(b, s, -1) @ w_out
    return out, cmp_kv, slc_kv, win_kv[:, -min(C_WINDOW, s):]


def mixer_c_sample(h, pos, page_table, pool_cmp, pool_slc, win_buf, w_in, q_gain, k_gain, pe, w1, w2, w_out):
    db, n, _ = h.shape
    q, qr, gates, cmp_new, slc_new, win_new = c_project(h, pos, w_in, q_gain, k_gain)
    past_len = page_table.shape[1] * pool_cmp.shape[1]

    def gather(pool):
        return pool[page_table].reshape((db, past_len) + pool.shape[2:])

    kc, vc = c_compress(jnp.concatenate([gather(pool_cmp), cmp_new], axis=1), pe, w1, w2, k_gain[0])
    slc_blk = c_sel_blocks(jnp.concatenate([gather(pool_slc), slc_new], axis=1))
    lw = win_buf.shape[1]
    win_all = jnp.concatenate([win_buf, win_new], axis=1)
    kw_pos = past_len - lw + jnp.arange(lw + n)
    o = jax.vmap(nsa_core, in_axes=(0, 0, 0, None, 0, 0, 0, 0, 0, 0, None))(
        q, qr, gates, pos, kc, vc, slc_blk[:, 0], slc_blk[:, 1], win_all[:, :, 0], win_all[:, :, 1], kw_pos)
    out = o.reshape(db, n, -1) @ w_out
    return out, cmp_new, slc_new, win_all[:, -lw:]


def mixer_d(h, hist, w_group, scale):
    b, n, d = h.shape
    lh = hist.shape[1]
    ua = jnp.concatenate([hist, h], axis=1)
    cs = jnp.pad(jnp.cumsum(ua.astype(jnp.float32), axis=1), ((0, 0), (1, 0), (0, 0)))
    end = lh + 1 + jnp.arange(n)
    parts = []
    for g, win in enumerate(D_POOLS):
        start = jnp.maximum(end - win, 0)
        c = cs[..., g * D_GROUP_W:(g + 1) * D_GROUP_W]
        parts.append((c[:, end] - c[:, start]) / (end - start).astype(jnp.float32)[None, :, None])
    pooled = jnp.concatenate(parts, axis=-1) - h.astype(jnp.float32)
    mixed = jnp.einsum('bngc,gcd->bngd', pooled.reshape(b, n, len(D_POOLS), D_GROUP_W).astype(h.dtype), w_group)
    return mixed.reshape(b, n, d) * scale, ua[:, -D_HIST:]


def swiglu(h, w_in, w_out):
    g, u = jnp.split(h @ w_in, 2, axis=-1)
    return (jax.nn.silu(g) * u) @ w_out


def run_group(x, p, pos, prm, past):
    new = {nm: [] for nm in STATE_NAMES}
    for i in range(DEPTH):
        kind, li = i % N_MIXERS, i // N_MIXERS
        h = rms_norm(x, prm['attn_norm'][i])
        if kind == 0:
            wa = (prm['w_a_qkv'][li], prm['a_q_norm'][li], prm['a_k_norm'][li], prm['w_a_out'][li])
            if past is None:
                out, bufs = mixer_a_prompt(h, pos, *wa)
            else:
                out, bufs = mixer_a_sample(h, pos, [past[nm][li] for nm in ('a_w1', 'a_w2', 'a_w3')], *wa)
            for nm, bf in zip(('a_w1', 'a_w2', 'a_w3'), bufs):
                new[nm].append(bf)
        elif kind == 1:
            hist = jnp.zeros((x.shape[0], CONV_W - 1, D_MODEL), x.dtype) if past is None else past['b_conv'][li]
            out, st = mixer_b(h, hist, prm['w_b_in'][li], prm['b_conv'][li], prm['w_b_out'][li])
            new['b_conv'].append(st)
        elif kind == 2:
            wc = (prm['w_c_in'][li], prm['c_q_norm'][li], prm['c_k_norm'][li], prm['c_cmp_pe'][li],
                  prm['c_cmp_w1'][li], prm['c_cmp_w2'][li], prm['w_c_out'][li])
            if past is None:
                out, c_cmp, c_slc, c_win = mixer_c_prompt(h, pos, *wc)
            else:
                out, c_cmp, c_slc, c_win = mixer_c_sample(h, pos, past['page_table'], past['c_cmp'][li],
                                                          past['c_slc'][li], past['c_win'][li], *wc)
            new['c_cmp'].append(c_cmp)
            new['c_slc'].append(c_slc)
            new['c_win'].append(c_win)
        else:
            hist = jnp.zeros((x.shape[0], 0, D_MODEL), x.dtype) if past is None else past['d_pool'][li]
            out, st = mixer_d(h, hist, prm['w_d_group'][li], prm['d_scale'][li])
            new['d_pool'].append(st)
        x = x + out
        x = x + swiglu(rms_norm(x, prm['ffn_norm'][i]), prm['w_ffn_in'][i], prm['w_ffn_out'][i])
        x = x + (p[i] @ prm['w_ple_proj'][i]) * jax.nn.sigmoid(rms_norm(x, prm['ple_norm'][i]) @ prm['w_ple_gate'][i])
    return x, {nm: jnp.stack(v, axis=0) for nm, v in new.items()}


def setup_inputs(seed: int = 0) -> dict:
    key = jax.random.key(seed)
    keys = jax.random.split(key, 48)
    counter = [0]
    f32 = jnp.float32

    def nxt():
        k = keys[counter[0]]
        counter[0] += 1
        return k

    def nrm(shape, scale=1.0):
        return jax.random.normal(nxt(), shape, f32) * scale

    def gain(shape):
        return 1.0 + 0.02 * jax.random.normal(nxt(), shape, f32)

    n_pages = PAST_LEN // PAGE_SIZE
    n_pool = (DEC_BATCH * n_pages * 5 + 3) // 4
    page_table = jax.random.permutation(nxt(), n_pool)[:DEC_BATCH * n_pages].reshape(DEC_BATCH, n_pages).astype(jnp.int32)
    a_len = [min(w, PAST_LEN) for w, _ in A_PATTERNS]
    return {
        'x_prompt': nrm((BATCH, SEQ, D_MODEL)),
        'x_sample': nrm((DEC_BATCH, DEC_SEQ, D_MODEL)),
        'cache_a_w1': nrm((N_LAYERS_A, DEC_BATCH, a_len[0], 2, A_HEADS, HEAD_DIM)),
        'cache_a_w2': nrm((N_LAYERS_A, DEC_BATCH, a_len[1], 2, A_HEADS, HEAD_DIM)),
        'cache_a_w3': nrm((N_LAYERS_A, DEC_BATCH, a_len[2], 2, A_HEADS, HEAD_DIM)),
        'state_b_conv': nrm((N_LAYERS_B, DEC_BATCH, CONV_W - 1, D_MODEL)),
        'cache_c_cmp': nrm((N_LAYERS_C, n_pool, PAGE_SIZE, 2, C_KV_HEADS, HEAD_DIM)),
        'cache_c_slc': nrm((N_LAYERS_C, n_pool, PAGE_SIZE, 2, C_KV_HEADS, HEAD_DIM)),
        'cache_c_win': nrm((N_LAYERS_C, DEC_BATCH, min(C_WINDOW, PAST_LEN), 2, C_KV_HEADS, HEAD_DIM)),
        'state_d_pool': nrm((N_LAYERS_D, DEC_BATCH, D_HIST, D_MODEL)),
        'page_table': page_table,
        'p_prompt': nrm((DEPTH, BATCH, SEQ, PLE_DIM)),
        'p_sample': nrm((DEPTH, DEC_BATCH, DEC_SEQ, PLE_DIM)),
        'attn_norm': gain((DEPTH, D_MODEL)),
        'ffn_norm': gain((DEPTH, D_MODEL)),
        'ple_norm': gain((DEPTH, D_MODEL)),
        'w_a_qkv': nrm((N_LAYERS_A, D_MODEL, len(A_PATTERNS) * 3 * A_HEADS * HEAD_DIM), D_MODEL ** -0.5),
        'a_q_norm': gain((N_LAYERS_A, len(A_PATTERNS), HEAD_DIM)),
        'a_k_norm': gain((N_LAYERS_A, len(A_PATTERNS), HEAD_DIM)),
        'w_a_out': nrm((N_LAYERS_A, A_HEADS * HEAD_DIM, D_MODEL), (A_HEADS * HEAD_DIM) ** -0.5),
        'w_b_in': nrm((N_LAYERS_B, D_MODEL, 3 * D_MODEL), D_MODEL ** -0.5),
        'b_conv': nrm((N_LAYERS_B, CONV_W, D_MODEL), CONV_W ** -0.5),
        'w_b_out': nrm((N_LAYERS_B, D_MODEL, D_MODEL), D_MODEL ** -0.5),
        'w_c_in': nrm((N_LAYERS_C, D_MODEL, C_IN), D_MODEL ** -0.5),
        'c_q_norm': gain((N_LAYERS_C, HEAD_DIM)),
        'c_k_norm': gain((N_LAYERS_C, 3, HEAD_DIM)),
        'c_cmp_pe': nrm((N_LAYERS_C, 2, C_CMP_LEN, HEAD_DIM), 0.1),
        'c_cmp_w1': nrm((N_LAYERS_C, 2, C_CMP_LEN, HEAD_DIM, C_CMP_HIDDEN), (C_CMP_LEN * HEAD_DIM) ** -0.5),
        'c_cmp_w2': nrm((N_LAYERS_C, 2, C_CMP_HIDDEN, HEAD_DIM), C_CMP_HIDDEN ** -0.5),
        'w_c_out': nrm((N_LAYERS_C, C_HEADS * HEAD_DIM, D_MODEL), (C_HEADS * HEAD_DIM) ** -0.5),
        'w_d_group': nrm((N_LAYERS_D, len(D_POOLS), D_GROUP_W, D_GROUP_W), D_GROUP_W ** -0.5),
        'd_scale': gain((N_LAYERS_D, D_MODEL)),
        'w_ffn_in': nrm((DEPTH, D_MODEL, 2 * D_FF), D_MODEL ** -0.5),
        'w_ffn_out': nrm((DEPTH, D_FF, D_MODEL), D_FF ** -0.5),
        'w_ple_proj': nrm((DEPTH, PLE_DIM, D_MODEL), PLE_DIM ** -0.5),
        'w_ple_gate': nrm((DEPTH, D_MODEL, D_MODEL), D_MODEL ** -0.5),
    }


def reference(x_prompt, x_sample, cache_a_w1, cache_a_w2, cache_a_w3, state_b_conv, cache_c_cmp, cache_c_slc,
              cache_c_win, state_d_pool, page_table, p_prompt, p_sample, attn_norm, ffn_norm, ple_norm,
              w_a_qkv, a_q_norm, a_k_norm, w_a_out, w_b_in, b_conv, w_b_out, w_c_in, c_q_norm, c_k_norm,
              c_cmp_pe, c_cmp_w1, c_cmp_w2, w_c_out, w_d_group, d_scale, w_ffn_in, w_ffn_out, w_ple_proj,
              w_ple_gate):
    prm = dict(attn_norm=attn_norm, ffn_norm=ffn_norm, ple_norm=ple_norm, w_a_qkv=w_a_qkv, a_q_norm=a_q_norm,
               a_k_norm=a_k_norm, w_a_out=w_a_out, w_b_in=w_b_in, b_conv=b_conv, w_b_out=w_b_out, w_c_in=w_c_in,
               c_q_norm=c_q_norm, c_k_norm=c_k_norm, c_cmp_pe=c_cmp_pe, c_cmp_w1=c_cmp_w1, c_cmp_w2=c_cmp_w2,
               w_c_out=w_c_out, w_d_group=w_d_group, d_scale=d_scale, w_ffn_in=w_ffn_in, w_ffn_out=w_ffn_out,
               w_ple_proj=w_ple_proj, w_ple_gate=w_ple_gate)
    past = dict(a_w1=cache_a_w1, a_w2=cache_a_w2, a_w3=cache_a_w3, b_conv=state_b_conv, c_cmp=cache_c_cmp,
                c_slc=cache_c_slc, c_win=cache_c_win, d_pool=state_d_pool, page_table=page_table)
    past_len = page_table.shape[1] * cache_c_cmp.shape[2]
    pos_prompt = jnp.arange(x_prompt.shape[1])
    pos_sample = past_len + jnp.arange(x_sample.shape[1])
    y_prompt, sp = run_group(x_prompt, p_prompt, pos_prompt, prm, None)
    y_sample, ss = run_group(x_sample, p_sample, pos_sample, prm, past)
    return (y_prompt, y_sample,
            sp['a_w1'], ss['a_w1'], sp['a_w2'], ss['a_w2'], sp['a_w3'], ss['a_w3'],
            sp['b_conv'], ss['b_conv'],
            sp['c_cmp'], ss['c_cmp'], sp['c_slc'], ss['c_slc'], sp['c_win'], ss['c_win'],
            sp['d_pool'], ss['d_pool'])
```

```python
import functools

import jax
import jax.numpy as jnp
from jax import lax
from jax.experimental import pallas as pl
from jax.experimental.pallas import tpu as pltpu

F32 = jnp.float32
BF16 = jnp.bfloat16
I32 = jnp.int32

HEAD_DIM = 128
LANES = 128
SUBLANES_BF16 = 16
RMS_EPS = 1e-6
ROPE_THETA = 10000.0
NEG = -1e30
V7X_VMEM_BYTES = 64 * 1024 * 1024
VMEM_LIMIT = V7X_VMEM_BYTES - 8 * 1024 * 1024

A_PATTERNS = ((128, 1), (512, 4), (2048, 16))
A_BLOCK = 128
CONV_W = 3
C_KV_HEADS = 4
C_GROUP = 4
C_CMP_LEN = 32
C_CMP_STRIDE = 16
C_SEL_BLOCK = 64
C_SEL_TOPN = 16
C_WINDOW = 512
C_FORCE_SCORE = 1e4
C_QBLOCK = 128
D_POOLS = (2, 4, 8, 16)
D_HIST = max(D_POOLS) - 1
PAGES_PER_GROUP = 16

RAW, NORM, NORM_ROPE, SIGMOID = range(4)


def _cparams(n_axes):
    return pltpu.CompilerParams(dimension_semantics=("arbitrary",) * n_axes,
                                vmem_limit_bytes=VMEM_LIMIT)


def _row_tile(m):
    return 1024 if m % 1024 == 0 else m


def _dot(a, b):
    return jnp.dot(a, b, preferred_element_type=F32)


def _dot_nt(a, b):
    return lax.dot_general(a, b, (((1,), (1,)), ((), ())), preferred_element_type=F32)


def _rms(x, g):
    return x * lax.rsqrt(jnp.mean(x * x, axis=-1, keepdims=True) + RMS_EPS) * g


def _rope(y, cos, sin):
    return y * cos + pltpu.roll(y, HEAD_DIM // 2, 1) * sin


def _tile_pred(j, tiles):
    pred = None
    lo = prev = tiles[0]
    runs = []
    for t in tiles[1:]:
        if t != prev + 1:
            runs.append((lo, prev))
            lo = t
        prev = t
    runs.append((lo, prev))
    for lo, hi in runs:
        p = (j == lo) if lo == hi else ((j >= lo) & (j <= hi))
        pred = p if pred is None else (pred | p)
    return pred


def _norm_linear_kernel(x_ref, g_ref, w_ref, hg_ref, cos_ref, sin_ref, o_ref, h_scr, *, modes):
    j = pl.program_id(1)

    @pl.when(j == 0)
    def _():
        h_scr[...] = _rms(x_ref[...], g_ref[...]).astype(BF16)

    acc = _dot(h_scr[...], w_ref[...].astype(BF16))
    tn = acc.shape[1]

    def emit(mode):
        if mode == RAW:
            o_ref[...] = acc.astype(o_ref.dtype)
        elif mode == SIGMOID:
            o_ref[...] = jax.nn.sigmoid(acc).astype(o_ref.dtype)
        else:
            for c in range(tn // HEAD_DIM):
                sl = slice(c * HEAD_DIM, (c + 1) * HEAD_DIM)
                y = _rms(acc[:, sl], hg_ref[:, sl])
                if mode == NORM_ROPE:
                    y = _rope(y, cos_ref[...], sin_ref[...])
                o_ref[:, sl] = y.astype(o_ref.dtype)

    kinds = sorted(set(modes))
    if len(kinds) == 1:
        emit(kinds[0])
    else:
        for m in kinds:
            tiles = [t for t, mm in enumerate(modes) if mm == m]
            pl.when(_tile_pred(j, tiles))(functools.partial(emit, m))


def norm_linear(x, gain, gi, w, wi, *, col0, modes, head_gain, cos, sin, tn, name):
    m, k = x.shape
    tm = _row_tile(m)
    nt = len(modes)
    assert col0 % tn == 0 and cos.shape[0] % tm == 0
    coff = col0 // tn
    n_pos_tiles = cos.shape[0] // tm
    return pl.pallas_call(
        functools.partial(_norm_linear_kernel, modes=tuple(modes)),
        out_shape=jax.ShapeDtypeStruct((m, nt * tn), F32),
        grid=(m // tm, nt),
        in_specs=[
            pl.BlockSpec((tm, k), lambda i, j: (i, 0)),
            pl.BlockSpec((None, 1, k), lambda i, j: (gi, 0, 0)),
            pl.BlockSpec((None, k, tn), lambda i, j: (wi, 0, j + coff)),
            pl.BlockSpec((1, tn), lambda i, j: (0, j)),
            pl.BlockSpec((tm, HEAD_DIM), lambda i, j: (i % n_pos_tiles, 0)),
            pl.BlockSpec((tm, HEAD_DIM), lambda i, j: (i % n_pos_tiles, 0)),
        ],
        out_specs=pl.BlockSpec((tm, tn), lambda i, j: (i, j)),
        scratch_shapes=[pltpu.VMEM((tm, k), BF16)],
        compiler_params=_cparams(2),
        name=name,
    )(x, gain.reshape(gain.shape[0], 1, k), w, head_gain, cos, sin)


def _linear_res_kernel(a_ref, w_ref, r_ref, o_ref):
    o_ref[...] = r_ref[...] + _dot(a_ref[...].astype(BF16), w_ref[...].astype(BF16))


def linear_residual(a, w, li, res, *, tn, name):
    m, k = a.shape
    n = w.shape[-1]
    tm = _row_tile(m)
    return pl.pallas_call(
        _linear_res_kernel,
        out_shape=jax.ShapeDtypeStruct((m, n), F32),
        grid=(m // tm, n // tn),
        in_specs=[
            pl.BlockSpec((tm, k), lambda i, j: (i, 0)),
            pl.BlockSpec((None, k, tn), lambda i, j: (li, 0, j)),
            pl.BlockSpec((tm, tn), lambda i, j: (i, j)),
        ],
        out_specs=pl.BlockSpec((tm, tn), lambda i, j: (i, j)),
        compiler_params=_cparams(2),
        name=name,
    )(a, w, res)


def _swiglu_in_kernel(x_ref, g_ref, wg_ref, wu_ref, o_ref, h_scr):
    @pl.when(pl.program_id(1) == 0)
    def _():
        h_scr[...] = _rms(x_ref[...], g_ref[...]).astype(BF16)

    h = h_scr[...]
    a = _dot(h, wg_ref[...].astype(BF16))
    b = _dot(h, wu_ref[...].astype(BF16))
    o_ref[...] = (a * jax.nn.sigmoid(a) * b).astype(o_ref.dtype)


def swiglu_in(x, gain, w, li, *, tn, name):
    m, k = x.shape
    f = w.shape[-1] // 2
    tm = _row_tile(m)
    nt = f // tn
    return pl.pallas_call(
        _swiglu_in_kernel,
        out_shape=jax.ShapeDtypeStruct((m, f), BF16),
        grid=(m // tm, nt),
        in_specs=[
            pl.BlockSpec((tm, k), lambda i, j: (i, 0)),
            pl.BlockSpec((None, 1, k), lambda i, j: (li, 0, 0)),
            pl.BlockSpec((None, k, tn), lambda i, j: (li, 0, j)),
            pl.BlockSpec((None, k, tn), lambda i, j: (li, 0, j + nt)),
        ],
        out_specs=pl.BlockSpec((tm, tn), lambda i, j: (i, j)),
        scratch_shapes=[pltpu.VMEM((tm, k), BF16)],
        compiler_params=_cparams(2),
        name=name,
    )(x, gain.reshape(gain.shape[0], 1, k), w, w)


def _ple_kernel(x_ref, g_ref, p_ref, wp_ref, wg_ref, r_ref, o_ref, h_scr):
    @pl.when(pl.program_id(1) == 0)
    def _():
        h_scr[...] = _rms(x_ref[...], g_ref[...]).astype(BF16)

    gate = jax.nn.sigmoid(_dot(h_scr[...], wg_ref[...].astype(BF16)))
    proj = _dot(p_ref[...].astype(BF16), wp_ref[...].astype(BF16))
    o_ref[...] = r_ref[...] + proj * gate


def ple(x, gain, p, wp, wg, li, *, tn, name):
    m, k = x.shape
    pd = p.shape[-1]
    tm = _row_tile(m)
    assert p.shape[1] == m
    return pl.pallas_call(
        _ple_kernel,
        out_shape=jax.ShapeDtypeStruct((m, k), F32),
        grid=(m // tm, k // tn),
        in_specs=[
            pl.BlockSpec((tm, k), lambda i, j: (i, 0)),
            pl.BlockSpec((None, 1, k), lambda i, j: (li, 0, 0)),
            pl.BlockSpec((None, tm, pd), lambda i, j: (li, i, 0)),
            pl.BlockSpec((None, pd, tn), lambda i, j: (li, 0, j)),
            pl.BlockSpec((None, k, tn), lambda i, j: (li, 0, j)),
            pl.BlockSpec((tm, tn), lambda i, j: (i, j)),
        ],
        out_specs=pl.BlockSpec((tm, tn), lambda i, j: (i, j)),
        scratch_shapes=[pltpu.VMEM((tm, k), BF16)],
        compiler_params=_cparams(2),
        name=name,
    )(x, gain.reshape(gain.shape[0], 1, k), p, wp, wg, x)


def _a_attn_kernel(*refs, has_prev, n_heads):
    if has_prev:
        q_ref, kc_ref, vc_ref, kp_ref, vp_ref, o_ref, l_ref = refs
    else:
        q_ref, kc_ref, vc_ref, o_ref, l_ref = refs
    n = pl.program_id(2)
    scale = HEAD_DIM ** -0.5
    row = lax.broadcasted_iota(I32, (A_BLOCK, A_BLOCK), 0)
    col = lax.broadcasted_iota(I32, (A_BLOCK, A_BLOCK), 1)
    cur_ok = col <= row
    prev_ok = (col >= row) & (n > 0)
    for h in range(n_heads):
        sl = slice(h * HEAD_DIM, (h + 1) * HEAD_DIM)
        q = (q_ref[0, :, sl] * scale).astype(BF16)
        s_c = jnp.where(cur_ok, _dot_nt(q, kc_ref[0, :, sl].astype(BF16)), NEG)
        m = jnp.max(s_c, axis=1, keepdims=True)
        if has_prev:
            s_p = jnp.where(prev_ok, _dot_nt(q, kp_ref[0, :, sl].astype(BF16)), NEG)
            m = jnp.maximum(m, jnp.max(s_p, axis=1, keepdims=True))
        p_c = jnp.exp(s_c - m)
        den = jnp.sum(p_c, axis=1, keepdims=True)
        acc = _dot(p_c.astype(BF16), vc_ref[0, :, sl].astype(BF16))
        if has_prev:
            p_p = jnp.exp(s_p - m)
            den = den + jnp.sum(p_p, axis=1, keepdims=True)
            acc = acc + _dot(p_p.astype(BF16), vp_ref[0, :, sl].astype(BF16))
        o_ref[0, :, sl] = acc / den
        l_ref[0, :, sl] = jnp.broadcast_to(m + jnp.log(den), (A_BLOCK, HEAD_DIM))


def a_attention_prompt(z, batch, seq, g, dil):
    nc = z.shape[1]
    d = nc // 9
    n_sub = seq // dil
    nb = n_sub // A_BLOCK
    assert n_sub % A_BLOCK == 0
    has_prev = nb > 1
    zv = z.reshape(batch, n_sub, dil * nc)
    cols = nc // d

    def spec(t, prev):
        if prev:
            return pl.BlockSpec((1, A_BLOCK, d), lambda b, r, n: (b, jnp.maximum(n - 1, 0), r * cols + g * 3 + t))
        return pl.BlockSpec((1, A_BLOCK, d), lambda b, r, n: (b, n, r * cols + g * 3 + t))

    in_specs = [spec(0, False), spec(1, False), spec(2, False)]
    args = [zv, zv, zv]
    if has_prev:
        in_specs += [spec(1, True), spec(2, True)]
        args += [zv, zv]
    out_spec = pl.BlockSpec((1, A_BLOCK, d), lambda b, r, n: (b, n, r))
    o, lse = pl.pallas_call(
        functools.partial(_a_attn_kernel, has_prev=has_prev, n_heads=d // HEAD_DIM),
        out_shape=[jax.ShapeDtypeStruct((batch, n_sub, dil * d), F32)] * 2,
        grid=(batch, dil, nb),
        in_specs=in_specs,
        out_specs=[out_spec, out_spec],
        compiler_params=_cparams(3),
        name=f"a_attn_g{g}",
    )(*args)
    return o.reshape(batch * seq, d), lse.reshape(batch * seq, d)


def _a_merge_kernel(o0, l0, o1, l1, o2, l2, out_ref):
    la, lb, lc = l0[...], l1[...], l2[...]
    m = jnp.maximum(jnp.maximum(la, lb), lc)
    ea, eb, ec = jnp.exp(la - m), jnp.exp(lb - m), jnp.exp(lc - m)
    tot = ea + eb + ec
    out_ref[...] = ((ea * o0[...] + eb * o1[...] + ec * o2[...]) / tot).astype(out_ref.dtype)


def a_merge(parts):
    m, d = parts[0].shape
    tm = 256 if m % 256 == 0 else m
    spec = pl.BlockSpec((tm, d), lambda i: (i, 0))
    return pl.pallas_call(
        _a_merge_kernel,
        out_shape=jax.ShapeDtypeStruct((m, d), BF16),
        grid=(m // tm,),
        in_specs=[spec] * 6,
        out_specs=spec,
        compiler_params=_cparams(1),
        name="a_merge",
    )(*parts)


def _a_sample_kernel(z_ref, *refs, n_heads, d):
    cache_refs, o_ref = refs[:-1], refs[-1]
    scale = HEAD_DIM ** -0.5
    for h in range(n_heads):
        ms, dens, accs = [], [], []
        for g in range(len(A_PATTERNS)):
            k_ref, v_ref = cache_refs[2 * g], cache_refs[2 * g + 1]
            c0 = g * 3 * d + h * HEAD_DIM
            q = z_ref[0, :, c0:c0 + HEAD_DIM] * scale
            k_new = z_ref[0, :, c0 + d:c0 + d + HEAD_DIM]
            v_new = z_ref[0, :, c0 + 2 * d:c0 + 2 * d + HEAD_DIM]
            q8 = jnp.broadcast_to(q, (8, HEAD_DIM))
            sl = slice(h * HEAD_DIM, (h + 1) * HEAD_DIM)
            s = _dot_nt(q8.astype(BF16), k_ref[0, :, sl].astype(BF16))
            s_new = jnp.sum(q8 * k_new, axis=1, keepdims=True)
            m = jnp.maximum(jnp.max(s, axis=1, keepdims=True), s_new)
            p = jnp.exp(s - m)
            p_new = jnp.exp(s_new - m)
            dens.append(jnp.sum(p, axis=1, keepdims=True) + p_new)
            accs.append(_dot(p.astype(BF16), v_ref[0, :, sl].astype(BF16)) + p_new * v_new)
            ms.append(m)
        mt = jnp.maximum(jnp.maximum(ms[0], ms[1]), ms[2])
        es = [jnp.exp(mm - mt) for mm in ms]
        tot = es[0] * dens[0] + es[1] * dens[1] + es[2] * dens[2]
        num = es[0] * accs[0] + es[1] * accs[1] + es[2] * accs[2]
        o_ref[0, :, h * HEAD_DIM:(h + 1) * HEAD_DIM] = num / tot


def a_attention_sample(z, caches, n_batch):
    d = z.shape[1] // 9
    in_specs = [pl.BlockSpec((1, 1, z.shape[1]), lambda b: (b, 0, 0))]
    args = [z[:, None, :]]
    for (win, dil), cache in zip(A_PATTERNS, caches):
        lb = cache.shape[1]
        assert lb == win and (win // dil) == A_BLOCK
        cv = cache.reshape(n_batch, lb // dil, dil * 2 * d)
        in_specs += [pl.BlockSpec((1, A_BLOCK, d), lambda b: (b, 0, 0)),
                     pl.BlockSpec((1, A_BLOCK, d), lambda b: (b, 0, 1))]
        args += [cv, cv]
    out = pl.pallas_call(
        functools.partial(_a_sample_kernel, n_heads=d // HEAD_DIM, d=d),
        out_shape=jax.ShapeDtypeStruct((n_batch, 8, d), F32),
        grid=(n_batch,),
        in_specs=in_specs,
        out_specs=pl.BlockSpec((1, 8, d), lambda b: (b, 0, 0)),
        compiler_params=_cparams(1),
        name="a_attn_sample",
    )(*args)
    return out[:, 0]


def _b_in_kernel(x_ref, g_ref, wb_ref, wc_ref, wx_ref, bg_ref, u_ref, h_scr):
    @pl.when(pl.program_id(1) == 0)
    def _():
        h_scr[...] = _rms(x_ref[...], g_ref[...]).astype(BF16)

    h = h_scr[...]
    bg_ref[...] = _dot(h, wb_ref[...].astype(BF16))
    u_ref[...] = _dot(h, wc_ref[...].astype(BF16)) * _dot(h, wx_ref[...].astype(BF16))


def b_in(x, gain, gi, w, li, *, tn, name):
    m, k = x.shape
    d = w.shape[-1] // 3
    tm = _row_tile(m)
    nt = d // tn
    out_spec = pl.BlockSpec((tm, tn), lambda i, j: (i, j))
    return pl.pallas_call(
        _b_in_kernel,
        out_shape=[jax.ShapeDtypeStruct((m, d), F32)] * 2,
        grid=(m // tm, nt),
        in_specs=[
            pl.BlockSpec((tm, k), lambda i, j: (i, 0)),
            pl.BlockSpec((None, 1, k), lambda i, j: (gi, 0, 0)),
            pl.BlockSpec((None, k, tn), lambda i, j: (li, 0, j)),
            pl.BlockSpec((None, k, tn), lambda i, j: (li, 0, j + nt)),
            pl.BlockSpec((None, k, tn), lambda i, j: (li, 0, j + 2 * nt)),
        ],
        out_specs=[out_spec, out_spec],
        scratch_shapes=[pltpu.VMEM((tm, k), BF16)],
        compiler_params=_cparams(2),
        name=name,
    )(x, gain.reshape(gain.shape[0], 1, k), w, w, w)


def _b_out_prompt_kernel(u_ref, up_ref, bg_ref, cw_ref, w_ref, r_ref, o_ref, ext_scr, a_scr, *, tiles_per_seq):
    i = pl.program_id(0)
    tm = u_ref.shape[0]

    @pl.when(pl.program_id(1) == 0)
    def _():
        ext_scr[0:8, :] = jnp.where(i % tiles_per_seq == 0, 0.0, up_ref[...])
        ext_scr[8:, :] = u_ref[...]
        y = (cw_ref[0:1, :] * ext_scr[pl.ds(6, tm), :] + cw_ref[1:2, :] * ext_scr[pl.ds(7, tm), :]
             + cw_ref[2:3, :] * ext_scr[pl.ds(8, tm), :])
        a_scr[...] = (bg_ref[...] * y).astype(BF16)

    o_ref[...] = r_ref[...] + _dot(a_scr[...], w_ref[...].astype(BF16))


def b_out_prompt(u, bg, conv_w, w, li, res, seq, *, tn, name):
    m, d = u.shape
    tm = 512
    assert seq % tm == 0
    return pl.pallas_call(
        functools.partial(_b_out_prompt_kernel, tiles_per_seq=seq // tm),
        out_shape=jax.ShapeDtypeStruct((m, d), F32),
        grid=(m // tm, d // tn),
        in_specs=[
            pl.BlockSpec((tm, d), lambda i, j: (i, 0)),
            pl.BlockSpec((8, d), lambda i, j: (jnp.maximum(i * (tm // 8) - 1, 0), 0)),
            pl.BlockSpec((tm, d), lambda i, j: (i, 0)),
            pl.BlockSpec((None, CONV_W, d), lambda i, j: (li, 0, 0)),
            pl.BlockSpec((None, d, tn), lambda i, j: (li, 0, j)),
            pl.BlockSpec((tm, tn), lambda i, j: (i, j)),
        ],
        out_specs=pl.BlockSpec((tm, tn), lambda i, j: (i, j)),
        scratch_shapes=[pltpu.VMEM((tm + 8, d), F32), pltpu.VMEM((tm, d), BF16)],
        compiler_params=_cparams(2),
        name=name,
    )(u, u, bg, conv_w, w, res)


def _b_out_sample_kernel(u_ref, um1_ref, um2_ref, bg_ref, cw_ref, w_ref, r_ref, o_ref):
    y = cw_ref[0:1, :] * um2_ref[...] + cw_ref[1:2, :] * um1_ref[...] + cw_ref[2:3, :] * u_ref[...]
    o_ref[...] = r_ref[...] + _dot((bg_ref[...] * y).astype(BF16), w_ref[...].astype(BF16))


def b_out_sample(u, um1, um2, bg, conv_w, w, li, res, *, tn, name):
    m, d = u.shape
    full = pl.BlockSpec((m, d), lambda j: (0, 0))
    return pl.pallas_call(
        _b_out_sample_kernel,
        out_shape=jax.ShapeDtypeStruct((m, d), F32),
        grid=(d // tn,),
        in_specs=[full, full, full, full,
                  pl.BlockSpec((None, CONV_W, d), lambda j: (li, 0, 0)),
                  pl.BlockSpec((None, d, tn), lambda j: (li, 0, j)),
                  pl.BlockSpec((m, tn), lambda j: (0, j))],
        out_specs=pl.BlockSpec((m, tn), lambda j: (0, j)),
        compiler_params=_cparams(1),
        name=name,
    )(u, um1, um2, bg, conv_w, w, res)


def _d_prompt_kernel(x_ref, xp_ref, g_ref, w_ref, sc_ref, r_ref, o_ref, ht_ref, ext_scr, *, tiles_per_seq):
    i = pl.program_id(0)
    j = pl.program_id(1)
    tm = x_ref.shape[0]
    halo = D_HIST + 1
    gw = w_ref.shape[0]

    @pl.when(j == 0)
    def _():
        ext_scr[0:halo, :] = jnp.where(i % tiles_per_seq == 0, 0.0, _rms(xp_ref[...], g_ref[...]))
        ext_scr[halo:, :] = _rms(x_ref[...], g_ref[...])
        ht_ref[0] = ext_scr[pl.ds(tm, halo), :]

    pos = (i % tiles_per_seq) * tm + lax.broadcasted_iota(I32, (tm, 1), 0)
    for g, win in enumerate(D_POOLS):
        @pl.when(j == g)
        def _(g=g, win=win):
            cols = slice(g * gw, (g + 1) * gw)
            tot = ext_scr[pl.ds(halo, tm), cols]
            h = tot
            for back in range(1, win):
                tot = tot + ext_scr[pl.ds(halo - back, tm), cols]
            count = jnp.minimum(pos + 1, win).astype(F32)
            pooled = tot / count - h
            o_ref[...] = r_ref[...] + _dot(pooled.astype(BF16), w_ref[...].astype(BF16)) * sc_ref[...]


def d_mixer_prompt(x, gain, li_norm, w_group, scale, li, seq, *, name):
    m, d = x.shape
    n_groups, gw = w_group.shape[1], w_group.shape[2]
    tm = 512
    halo = D_HIST + 1
    assert seq % tm == 0 and n_groups == len(D_POOLS)
    tps = seq // tm
    return pl.pallas_call(
        functools.partial(_d_prompt_kernel, tiles_per_seq=tps),
        out_shape=[jax.ShapeDtypeStruct((m, d), F32), jax.ShapeDtypeStruct((m // seq, halo, d), F32)],
        grid=(m // tm, n_groups),
        in_specs=[
            pl.BlockSpec((tm, d), lambda i, j: (i, 0)),
            pl.BlockSpec((halo, d), lambda i, j: (jnp.maximum(i * (tm // halo) - 1, 0), 0)),
            pl.BlockSpec((None, 1, d), lambda i, j: (li_norm, 0, 0)),
            pl.BlockSpec((None, None, gw, gw), lambda i, j: (li, j, 0, 0)),
            pl.BlockSpec((None, 1, gw), lambda i, j: (li, 0, j)),
            pl.BlockSpec((tm, gw), lambda i, j: (i, j)),
        ],
        out_specs=[pl.BlockSpec((tm, gw), lambda i, j: (i, j)),
                   pl.BlockSpec((1, halo, d), lambda i, j: (i // tps, 0, 0))],
        scratch_shapes=[pltpu.VMEM((tm + halo, d), F32)],
        compiler_params=_cparams(2),
        name=name,
    )(x, x, gain.reshape(gain.shape[0], 1, d), w_group, scale.reshape(scale.shape[0], 1, d), x)


def _d_sample_kernel(x_ref, hist_ref, g_ref, w_ref, sc_ref, o_ref, h_ref):
    h = _rms(x_ref[...], g_ref[...])
    h_ref[...] = h
    gw = w_ref.shape[1]
    for g, win in enumerate(D_POOLS):
        cols = slice(g * gw, (g + 1) * gw)
        tot = h[:, cols]
        for back in range(1, win):
            tot = tot + hist_ref[D_HIST - back, :, cols]
        pooled = tot / float(win) - h[:, cols]
        o_ref[:, cols] = x_ref[:, cols] + _dot(pooled.astype(BF16), w_ref[g].astype(BF16)) * sc_ref[:, cols]


def d_mixer_sample(x, hist_t, gain, li_norm, w_group, scale, li, *, name):
    m, d = x.shape
    n_groups, gw = w_group.shape[1], w_group.shape[2]
    assert hist_t.shape[0] == D_HIST
    return pl.pallas_call(
        _d_sample_kernel,
        out_shape=[jax.ShapeDtypeStruct((m, d), F32)] * 2,
        grid=(1,),
        in_specs=[
            pl.BlockSpec((m, d), lambda i: (0, 0)),
            pl.BlockSpec(hist_t.shape, lambda i: (0, 0, 0)),
            pl.BlockSpec((None, 1, d), lambda i: (li_norm, 0, 0)),
            pl.BlockSpec((None, n_groups, gw, gw), lambda i: (li, 0, 0, 0)),
            pl.BlockSpec((None, 1, d), lambda i: (li, 0, 0)),
        ],
        out_specs=[pl.BlockSpec((m, d), lambda i: (0, 0))] * 2,
        compiler_params=_cparams(1),
        name=name,
    )(x, hist_t, gain.reshape(gain.shape[0], 1, d), w_group, scale.reshape(scale.shape[0], 1, d))


def _cmp_fs_kernel(pt_ref, page_ref, w_ref, o_ref, ring_scr):
    del pt_ref
    slot = pl.program_id(1) % PAGES_PER_GROUP
    rows = page_ref.shape[1]
    for ck in range(2 * C_KV_HEADS):
        ring_scr[ck, pl.ds(pl.multiple_of(slot * rows, rows), rows), :] = page_ref[0, :, ck * HEAD_DIM:(ck + 1) * HEAD_DIM]

    @pl.when(slot == PAGES_PER_GROUP - 1)
    def _():
        n_chunks = PAGES_PER_GROUP * rows // C_CMP_STRIDE
        for ck in range(2 * C_KV_HEADS):
            c = ck // C_KV_HEADS
            acc = jnp.zeros((n_chunks, w_ref.shape[-1]), F32)
            for p in range(C_CMP_STRIDE):
                lhs = ring_scr[ck, pl.ds(p, n_chunks, stride=C_CMP_STRIDE), :]
                acc = acc + _dot(lhs.astype(BF16), w_ref[c, p])
            o_ref[0, ck] = acc


def cmp_first_second(pool, page_ids, col_block, w_cat, n_batch, n_pages):
    rows = pool.shape[1]
    width = 2 * C_KV_HEADS * HEAD_DIM
    assert n_pages % PAGES_PER_GROUP == 0 and rows % C_CMP_STRIDE == 0
    chunks_per_group = PAGES_PER_GROUP * rows // C_CMP_STRIDE
    n_out = w_cat.shape[-1]
    return pl.pallas_call(
        _cmp_fs_kernel,
        out_shape=jax.ShapeDtypeStruct((n_batch, 2 * C_KV_HEADS, n_pages * rows // C_CMP_STRIDE, n_out), F32),
        grid_spec=pltpu.PrefetchScalarGridSpec(
            num_scalar_prefetch=1,
            grid=(n_batch, n_pages),
            in_specs=[
                pl.BlockSpec((1, rows, width), lambda b, pg, pt: (pt[b * n_pages + pg], 0, col_block)),
                pl.BlockSpec(w_cat.shape, lambda b, pg, pt: (0, 0, 0, 0)),
            ],
            out_specs=pl.BlockSpec((1, 2 * C_KV_HEADS, chunks_per_group, n_out),
                                   lambda b, pg, pt: (b, 0, pg // PAGES_PER_GROUP, 0)),
            scratch_shapes=[pltpu.VMEM((width // HEAD_DIM, PAGES_PER_GROUP * rows, HEAD_DIM), F32)],
        ),
        compiler_params=_cparams(2),
        name="c_cmp_first_second",
    )(page_ids, pool, w_cat)


def _cmp_mlp_kernel(fs_ref, pe_ref, w1_ref, w2_ref, kg_ref, o_ref):
    ck = pl.program_id(1)
    nch = fs_ref.shape[2]
    hid_w = w2_ref.shape[1]
    pe_term = _dot(pe_ref[0].astype(BF16), w1_ref[0].astype(BF16))[0:1, :]
    first = fs_ref[0, 0, :, 0:hid_w]
    second = fs_ref[0, 0, :, hid_w:2 * hid_w]
    nxt = pltpu.roll(second, nch - 1, 0)
    last = lax.broadcasted_iota(I32, (nch, 1), 0) == nch - 1
    pre = first + jnp.where(last, 0.0, nxt) + pe_term
    hid = pre * jax.nn.sigmoid(pre)
    out = _dot(hid.astype(BF16), w2_ref[0].astype(BF16))

    @pl.when(ck < C_KV_HEADS)
    def _():
        o_ref[0, 0] = _rms(out, kg_ref[...])

    @pl.when(ck >= C_KV_HEADS)
    def _():
        o_ref[0, 0] = out


def cmp_mlp(fs, pe8, w1_flat, w2, k_gain0):
    n_batch, n_ck, nch, _ = fs.shape
    return pl.pallas_call(
        _cmp_mlp_kernel,
        out_shape=jax.ShapeDtypeStruct((n_batch, n_ck, nch, HEAD_DIM), F32),
        grid=(n_batch, n_ck),
        in_specs=[
            pl.BlockSpec((1, 1, nch, fs.shape[-1]), lambda b, ck: (b, ck, 0, 0)),
            pl.BlockSpec((1,) + pe8.shape[1:], lambda b, ck: (ck // C_KV_HEADS, 0, 0)),
            pl.BlockSpec((1,) + w1_flat.shape[1:], lambda b, ck: (ck // C_KV_HEADS, 0, 0)),
            pl.BlockSpec((1,) + w2.shape[1:], lambda b, ck: (ck // C_KV_HEADS, 0, 0)),
            pl.BlockSpec((1, HEAD_DIM), lambda b, ck: (0, 0)),
        ],
        out_specs=pl.BlockSpec((1, 1, nch, HEAD_DIM), lambda b, ck: (b, ck, 0, 0)),
        compiler_params=_cparams(2),
        name="c_cmp_mlp",
    )(fs, pe8, w1_flat, w2, k_gain0)


def _masked_softmax_rows(s, ok):
    s = jnp.where(ok, s, NEG)
    m = jnp.max(s, axis=1, keepdims=True)
    e = jnp.where(ok, jnp.exp(s - m), 0.0)
    return e, jnp.maximum(jnp.sum(e, axis=1, keepdims=True), 1e-30)


def _nsa_prompt_kernel(q_ref, gate_ref, cos_ref, sin_ref, kc_ref, vc_ref, ks_ref, vs_ref, kw_ref, vw_ref,
                       ov_ref, ex_ref, o_ref, *, seq, nsel):
    qb = q_ref.shape[0]
    nch = kc_ref.shape[2]
    t0 = pl.program_id(2) * qb
    t = t0 + lax.broadcasted_iota(I32, (qb, 1), 0)
    scale = HEAD_DIM ** -0.5
    qs = [q_ref[:, g * HEAD_DIM:(g + 1) * HEAD_DIM] for g in range(C_GROUP)]

    cmp_ok = lax.broadcasted_iota(I32, (qb, nch), 1) * C_CMP_STRIDE + (C_CMP_LEN - 1) <= t
    kc = kc_ref[0, 0].astype(BF16)
    vc = vc_ref[0, 0].astype(BF16)
    o_cmp = []
    p_sum = jnp.zeros((qb, nch), F32)
    for g in range(C_GROUP):
        e, den = _masked_softmax_rows(_dot_nt((qs[g] * scale).astype(BF16), kc), cmp_ok)
        p = e / den
        o_cmp.append(_dot(p.astype(BF16), vc))
        p_sum = p_sum + p

    p_hi = p_sum.astype(BF16)
    p_lo = (p_sum - p_hi.astype(F32)).astype(BF16)
    imp = _dot(p_hi, ov_ref[...]) + _dot(p_lo, ov_ref[...])
    lane = lax.broadcasted_iota(I32, (qb, LANES), 1)
    cur = t // C_SEL_BLOCK
    causal = (lane <= cur) & (lane < nsel)
    forced = (lane == 0) | (lane == cur) | (lane == cur - 1)
    imp = jnp.where(causal, jnp.where(forced, C_FORCE_SCORE, imp), NEG)
    rank = jnp.zeros((qb, LANES), I32)
    for jp in range(nsel):
        c = imp[:, jp:jp + 1]
        beats = (c > imp) | ((c == imp) & (lane > jp))
        rank = rank + beats.astype(I32)
    sel = jnp.where((rank < C_SEL_TOPN) & causal, 1.0, 0.0).astype(BF16)
    sel_keys = _dot(sel, ex_ref[...])
    slc_ok = (sel_keys > 0.5) & (lax.broadcasted_iota(I32, (qb, seq), 1) <= t)
    slc_bias = jnp.where(slc_ok, 0.0, NEG)

    wk = C_WINDOW + qb
    start = pl.multiple_of(jnp.maximum(t0 - C_WINDOW, 0), qb)
    dist = t - (start + lax.broadcasted_iota(I32, (qb, wk), 1))
    win_bias = jnp.where((dist >= 0) & (dist < C_WINDOW), 0.0, NEG)

    ks = ks_ref[0].astype(BF16)
    vs = vs_ref[0].astype(BF16)
    kw = kw_ref[0, pl.ds(start, wk), :].astype(BF16)
    vw = vw_ref[0, pl.ds(start, wk), :].astype(BF16)
    for g in range(C_GROUP):
        qr = (_rope(qs[g], cos_ref[...], sin_ref[...]) * scale).astype(BF16)
        s = _dot_nt(qr, ks) + slc_bias
        e = jnp.exp(s - jnp.max(s, axis=1, keepdims=True))
        o_slc = _dot(e.astype(BF16), vs) / jnp.sum(e, axis=1, keepdims=True)
        s = _dot_nt(qr, kw) + win_bias
        e = jnp.exp(s - jnp.max(s, axis=1, keepdims=True))
        o_win = _dot(e.astype(BF16), vw) / jnp.sum(e, axis=1, keepdims=True)
        o = (gate_ref[:, 3 * g:3 * g + 1] * o_cmp[g] + gate_ref[:, 3 * g + 1:3 * g + 2] * o_slc
             + gate_ref[:, 3 * g + 2:3 * g + 3] * o_win)
        o_ref[:, g * HEAD_DIM:(g + 1) * HEAD_DIM] = o.astype(o_ref.dtype)


def nsa_prompt(z, gates, cos, sin, kcv, overlap, expand, batch, seq):
    m, nc = z.shape
    d = C_KV_HEADS * C_GROUP * HEAD_DIM
    qb = C_QBLOCK
    nq = seq // qb
    nsel = -(-seq // C_SEL_BLOCK)
    assert nsel <= LANES and seq % qb == 0 and seq >= C_WINDOW + qb
    nch = kcv.shape[2]
    zv = z.reshape(batch, seq, nc)
    cb = d // HEAD_DIM
    kvw = C_KV_HEADS

    def kv_spec(branch, kv):
        off = cb + (branch * 2 + kv) * kvw
        return pl.BlockSpec((1, seq, HEAD_DIM), lambda b, k, q: (b, 0, off + k))

    return pl.pallas_call(
        functools.partial(_nsa_prompt_kernel, seq=seq, nsel=nsel),
        out_shape=jax.ShapeDtypeStruct((m, d), BF16),
        grid=(batch, C_KV_HEADS, nq),
        in_specs=[
            pl.BlockSpec((qb, C_GROUP * HEAD_DIM), lambda b, k, q: (b * nq + q, k)),
            pl.BlockSpec((qb, LANES), lambda b, k, q: (b * nq + q, k)),
            pl.BlockSpec((qb, HEAD_DIM), lambda b, k, q: (q, 0)),
            pl.BlockSpec((qb, HEAD_DIM), lambda b, k, q: (q, 0)),
            pl.BlockSpec((1, 1, nch, HEAD_DIM), lambda b, k, q: (b, k, 0, 0)),
            pl.BlockSpec((1, 1, nch, HEAD_DIM), lambda b, k, q: (b, C_KV_HEADS + k, 0, 0)),
            kv_spec(1, 0), kv_spec(1, 1), kv_spec(2, 0), kv_spec(2, 1),
            pl.BlockSpec(overlap.shape, lambda b, k, q: (0, 0)),
            pl.BlockSpec(expand.shape, lambda b, k, q: (0, 0)),
        ],
        out_specs=pl.BlockSpec((qb, C_GROUP * HEAD_DIM), lambda b, k, q: (b * nq + q, k)),
        compiler_params=_cparams(3),
        name="c_nsa_prompt",
    )(z, gates, cos, sin, kcv, kcv, zv, zv, zv, zv, overlap, expand)


def _nsa_sample_cmp_kernel(q_ref, kc_ref, vc_ref, ov_ref, ocmp_ref, idx_ref, *, t, nsel):
    nch = kc_ref.shape[2]
    scale = HEAD_DIM ** -0.5
    q = (q_ref[0, 0] * scale).astype(BF16)
    cmp_ok = lax.broadcasted_iota(I32, (8, nch), 1) * C_CMP_STRIDE + (C_CMP_LEN - 1) <= t
    e, den = _masked_softmax_rows(_dot_nt(q, kc_ref[0, 0].astype(BF16)), cmp_ok)
    p = e / den
    ocmp_ref[0, 0] = _dot(p.astype(BF16), vc_ref[0, 0].astype(BF16))
    real = lax.broadcasted_iota(I32, (8, nch), 0) < C_GROUP
    p_sum = jnp.broadcast_to(jnp.sum(jnp.where(real, p, 0.0), axis=0, keepdims=True), (8, nch))
    p_hi = p_sum.astype(BF16)
    p_lo = (p_sum - p_hi.astype(F32)).astype(BF16)
    imp = _dot(p_hi, ov_ref[...]) + _dot(p_lo, ov_ref[...])
    width = imp.shape[1]
    lane = lax.broadcasted_iota(I32, (8, width), 1)
    cur = t // C_SEL_BLOCK
    causal = (lane <= cur) & (lane < nsel)
    forced = (lane == 0) | (lane == cur) | (lane == cur - 1)
    work = jnp.where(causal, jnp.where(forced, C_FORCE_SCORE, imp), NEG)
    lane_f = lane.astype(F32)
    out_lane = lax.broadcasted_iota(I32, (8, LANES), 1)
    idx = jnp.full((8, LANES), -1, I32)
    for r in range(C_SEL_TOPN):
        best = jnp.max(work, axis=1, keepdims=True)
        pick = jnp.min(jnp.where(work == best, lane_f, float(width)), axis=1, keepdims=True)
        found = jnp.where(best > 0.5 * NEG, pick, -1.0).astype(I32)
        idx = jnp.where(out_lane == r, found, idx)
        work = jnp.where(lane_f == pick, NEG, work)
    idx_ref[0, 0] = idx


def nsa_sample_cmp(q8, kcv, overlap, t, nsel):
    n_batch = q8.shape[0]
    nch = kcv.shape[2]
    blk = pl.BlockSpec((1, 1, 8, HEAD_DIM), lambda b, k: (b, k, 0, 0))
    return pl.pallas_call(
        functools.partial(_nsa_sample_cmp_kernel, t=t, nsel=nsel),
        out_shape=[jax.ShapeDtypeStruct((n_batch, C_KV_HEADS, 8, HEAD_DIM), F32),
                   jax.ShapeDtypeStruct((n_batch, C_KV_HEADS, 8, LANES), I32)],
        grid=(n_batch, C_KV_HEADS),
        in_specs=[
            blk,
            pl.BlockSpec((1, 1, nch, HEAD_DIM), lambda b, k: (b, k, 0, 0)),
            pl.BlockSpec((1, 1, nch, HEAD_DIM), lambda b, k: (b, C_KV_HEADS + k, 0, 0)),
            pl.BlockSpec(overlap.shape, lambda b, k: (0, 0)),
        ],
        out_specs=[blk, blk],
        compiler_params=_cparams(2),
        name="c_nsa_sample_cmp",
    )(q8, kcv, kcv, overlap)


def _nsa_sample_kernel(idx_ref, pt_ref, q_ref, cos_ref, sin_ref, ks_ref, vs_ref, new_ref, kw_ref, vw_ref,
                       ocmp_ref, gate_ref, o_ref, m_scr, l_scr, acc_scr, *, t, n_past_blocks):
    del pt_ref
    b, k, n = pl.program_id(0), pl.program_id(1), pl.program_id(2)
    base = (b * C_KV_HEADS + k) * C_SEL_TOPN
    blk = idx_ref[base + n]
    scale = HEAD_DIM ** -0.5
    qr_f = _rope(q_ref[0, 0], cos_ref[...], sin_ref[...]) * scale
    qr = qr_f.astype(BF16)

    @pl.when(n == 0)
    def _():
        m_scr[...] = jnp.full(m_scr.shape, NEG, F32)
        l_scr[...] = jnp.zeros(l_scr.shape, F32)
        acc_scr[...] = jnp.zeros(acc_scr.shape, F32)

    def update(s, ok, pv):
        s = jnp.where(ok, s, NEG)
        m_old = m_scr[...]
        m_new = jnp.maximum(m_old, jnp.max(s, axis=1, keepdims=True))
        alpha = jnp.exp(m_old - m_new)
        p = jnp.where(ok, jnp.exp(s - m_new[:, 0:1]), 0.0)
        l_scr[...] = alpha * l_scr[...] + jnp.sum(p, axis=1, keepdims=True)
        acc_scr[...] = alpha * acc_scr[...] + pv(p)
        m_scr[...] = m_new

    kpos = blk * C_SEL_BLOCK + lax.broadcasted_iota(I32, (8, C_SEL_BLOCK), 1)
    ok = (blk >= 0) & (blk < n_past_blocks) & (kpos <= t)
    vs = vs_ref[0].astype(BF16)
    update(_dot_nt(qr, ks_ref[0].astype(BF16)), ok, lambda p: _dot(p.astype(BF16), vs))

    @pl.when(n == C_SEL_TOPN - 1)
    def _():
        has_new = blk == n_past_blocks
        for r in range(C_SEL_TOPN - 1):
            has_new = has_new | (idx_ref[base + r] == n_past_blocks)
        k_new, v_new = new_ref[0, 0, 0:1, :], new_ref[0, 0, 1:2, :]
        s_new = jnp.sum(qr_f * k_new, axis=1, keepdims=True)
        update(s_new, jnp.broadcast_to(has_new, (8, 1)), lambda p: p * v_new)
        o_slc = acc_scr[...] / jnp.maximum(l_scr[...], 1e-30)

        lw = kw_ref.shape[1]
        kw_new, vw_new = new_ref[0, 0, 2:3, :], new_ref[0, 0, 3:4, :]
        dist = lw - lax.broadcasted_iota(I32, (8, lw), 1)
        w_ok = (dist >= 0) & (dist < C_WINDOW)
        s = jnp.where(w_ok, _dot_nt(qr, kw_ref[0].astype(BF16)), NEG)
        s_new = jnp.sum(qr_f * kw_new, axis=1, keepdims=True)
        m = jnp.maximum(jnp.max(s, axis=1, keepdims=True), s_new)
        p = jnp.where(w_ok, jnp.exp(s - m), 0.0)
        p_new = jnp.exp(s_new - m)
        den = jnp.sum(p, axis=1, keepdims=True) + p_new
        o_win = (_dot(p.astype(BF16), vw_ref[0].astype(BF16)) + p_new * vw_new) / den

        gate = gate_ref[0, 0]
        o_ref[0, 0] = gate[:, 0:1] * ocmp_ref[0, 0] + gate[:, 1:2] * o_slc + gate[:, 2:3] * o_win


def nsa_sample(top_idx, page_table, q8, cos, sin, pool_slc, new_kv, win_buf, ocmp, gate8, t, n_past_blocks):
    n_batch = q8.shape[0]
    n_pages = page_table.shape[1]
    page_rows = pool_slc.shape[1]
    blocks_per_page = page_rows // C_SEL_BLOCK
    kv_width = 2 * C_KV_HEADS * HEAD_DIM
    pool = pool_slc.reshape(pool_slc.shape[0] * blocks_per_page, C_SEL_BLOCK, kv_width)
    lw = win_buf.shape[1]
    win = win_buf.reshape(n_batch, lw, kv_width)

    def slc_spec(kv):
        def index(b, k, n, idx, pt):
            blk = jnp.clip(idx[(b * C_KV_HEADS + k) * C_SEL_TOPN + n], 0, n_past_blocks - 1)
            page = pt[b * n_pages + blk // blocks_per_page]
            return (page * blocks_per_page + blk % blocks_per_page, 0, kv * C_KV_HEADS + k)
        return pl.BlockSpec((1, C_SEL_BLOCK, HEAD_DIM), index)

    blk8 = pl.BlockSpec((1, 1, 8, HEAD_DIM), lambda b, k, n, idx, pt: (b, k, 0, 0))
    tab = pl.BlockSpec((8, HEAD_DIM), lambda b, k, n, idx, pt: (0, 0))
    return pl.pallas_call(
        functools.partial(_nsa_sample_kernel, t=t, n_past_blocks=n_past_blocks),
        out_shape=jax.ShapeDtypeStruct((n_batch, C_KV_HEADS, 8, HEAD_DIM), F32),
        grid_spec=pltpu.PrefetchScalarGridSpec(
            num_scalar_prefetch=2,
            grid=(n_batch, C_KV_HEADS, C_SEL_TOPN),
            in_specs=[
                blk8, tab, tab, slc_spec(0), slc_spec(1), blk8,
                pl.BlockSpec((1, lw, HEAD_DIM), lambda b, k, n, idx, pt: (b, 0, k)),
                pl.BlockSpec((1, lw, HEAD_DIM), lambda b, k, n, idx, pt: (b, 0, C_KV_HEADS + k)),
                blk8, blk8,
            ],
            out_specs=blk8,
            scratch_shapes=[pltpu.VMEM((8, HEAD_DIM), F32)] * 3,
        ),
        compiler_params=_cparams(3),
        name="c_nsa_sample",
    )(top_idx.reshape(-1), page_table.reshape(-1), q8, cos, sin, pool, pool, new_kv, win, win, ocmp, gate8)


def _rope_tables(pos):
    half = HEAD_DIM // 2
    inv = ROPE_THETA ** (-(jnp.arange(half, dtype=F32) * 2.0 / HEAD_DIM))
    ang = pos.astype(F32)[:, None] * inv[None, :]
    cos, sin = jnp.cos(ang), jnp.sin(ang)
    return jnp.concatenate([cos, cos], axis=1), jnp.concatenate([-sin, sin], axis=1)


def _overlap_matrix(nch, nsel, width):
    cmp_start = jnp.arange(nch) * C_CMP_STRIDE
    sel_start = jnp.arange(width) * C_SEL_BLOCK
    ov = ((cmp_start[:, None] < sel_start[None, :] + C_SEL_BLOCK)
          & (cmp_start[:, None] + C_CMP_LEN > sel_start[None, :])
          & (jnp.arange(width)[None, :] < nsel))
    return ov.astype(BF16)


def _c_weights(prm, li, n_heads):
    d = n_heads * HEAD_DIM
    nkv = 3 * 2 * C_KV_HEADS * HEAD_DIM
    k_gain = prm['c_k_norm'][li]
    ones = jnp.ones((C_KV_HEADS * HEAD_DIM,), F32)
    head_gain = jnp.concatenate([
        jnp.tile(prm['c_q_norm'][li], n_heads), ones, ones,
        jnp.tile(k_gain[1], C_KV_HEADS), ones, jnp.tile(k_gain[2], C_KV_HEADS), ones])[None, :]
    wg = prm['w_c_in'][li][:, d + nkv:].reshape(-1, C_KV_HEADS, C_GROUP * 3)
    wg = jnp.pad(wg, ((0, 0), (0, 0), (0, LANES - C_GROUP * 3))).reshape(1, -1, C_KV_HEADS * LANES)
    w1 = prm['c_cmp_w1'][li]
    w_cat = jnp.concatenate([w1[:, :C_CMP_STRIDE], w1[:, C_CMP_STRIDE:]], axis=-1).astype(BF16)
    pe8 = jnp.pad(prm['c_cmp_pe'][li].reshape(2, 1, -1), ((0, 0), (0, 7), (0, 0)))
    w1_flat = w1.reshape(2, C_CMP_LEN * HEAD_DIM, -1)
    return head_gain, wg, w_cat, pe8, w1_flat, prm['c_cmp_w2'][li], k_gain[0][None, :]


TN = 512
TN_WIDE_K = 256
C_MODES = (NORM,) * 4 + (RAW, RAW, NORM_ROPE, RAW, NORM_ROPE, RAW)
A_MODES = ((NORM_ROPE,) * 8 + (RAW,) * 4) * len(A_PATTERNS)


def _pad_rows(a, rows):
    return jnp.pad(a, ((0, rows - a.shape[0]),) + ((0, 0),) * (a.ndim - 1))


def _a_head_gain(prm, li, n_heads):
    ones = jnp.ones((n_heads * HEAD_DIM,), F32)
    parts = []
    for g in range(len(A_PATTERNS)):
        parts += [jnp.tile(prm['a_q_norm'][li, g], n_heads), jnp.tile(prm['a_k_norm'][li, g], n_heads), ones]
    return jnp.concatenate(parts)[None, :]


def _channel_mixer(x, p, prm, i, tag):
    act = swiglu_in(x, prm['ffn_norm'], prm['w_ffn_in'], i, tn=TN, name=f"ffn_in_{tag}")
    x = linear_residual(act, prm['w_ffn_out'], i, x, tn=TN_WIDE_K, name=f"ffn_out_{tag}")
    return ple(x, prm['ple_norm'], p, prm['w_ple_proj'], prm['w_ple_gate'], i, tn=TN, name=f"ple_{tag}")


def _run_prompt(x3, p, prm):
    batch, seq, d = x3.shape
    n_heads = d // HEAD_DIM
    x = x3.reshape(batch * seq, d)
    cos, sin = _rope_tables(jnp.arange(seq))
    new = {}
    depth = prm['attn_norm'].shape[0]
    for i in range(depth):
        kind, li = i % 4, i // 4
        if kind == 0:
            z = norm_linear(x, prm['attn_norm'], i, prm['w_a_qkv'], li, col0=0, modes=A_MODES,
                            head_gain=_a_head_gain(prm, li, n_heads), cos=cos, sin=sin, tn=TN, name="a_qkv_prompt")
            parts = []
            for g, (win, dil) in enumerate(A_PATTERNS):
                parts += a_attention_prompt(z, batch, seq, g, dil)
            x = linear_residual(a_merge(parts), prm['w_a_out'], li, x, tn=TN, name="a_out_prompt")
            z3 = z.reshape(batch, seq, -1)
            for g, (win, dil) in enumerate(A_PATTERNS):
                kv = z3[:, seq - min(win, seq):, (3 * g + 1) * d:(3 * g + 3) * d]
                new.setdefault(f'a_w{g + 1}', []).append(kv.reshape(batch, -1, 2, n_heads, HEAD_DIM))
        elif kind == 1:
            bg, u = b_in(x, prm['attn_norm'], i, prm['w_b_in'], li, tn=TN_WIDE_K, name="b_in_prompt")
            x = b_out_prompt(u, bg, prm['b_conv'], prm['w_b_out'], li, x, seq, tn=TN, name="b_out_prompt")
            new.setdefault('b_conv', []).append(u.reshape(batch, seq, d)[:, seq - (CONV_W - 1):])
        elif kind == 2:
            head_gain, wg, w_cat, pe8, w1_flat, w2, k_gain0 = _c_weights(prm, li, n_heads)
            z = norm_linear(x, prm['attn_norm'], i, prm['w_c_in'], li, col0=0, modes=C_MODES,
                            head_gain=head_gain, cos=cos, sin=sin, tn=TN, name="c_in_prompt")
            gates = norm_linear(x, prm['attn_norm'], i, wg, 0, col0=0, modes=(SIGMOID,),
                                head_gain=head_gain[:, :TN], cos=cos, sin=sin, tn=TN, name="c_gate_prompt")
            n_pages = seq // LANES
            pool = z.reshape(batch * n_pages, LANES, z.shape[1])
            kv_width = 2 * C_KV_HEADS * HEAD_DIM
            fs = cmp_first_second(pool, jnp.arange(batch * n_pages, dtype=I32), d // kv_width, w_cat, batch, n_pages)
            kcv = cmp_mlp(fs, pe8, w1_flat, w2, k_gain0)
            nsel = -(-seq // C_SEL_BLOCK)
            overlap = _overlap_matrix(kcv.shape[2], nsel, LANES)
            expand = (jnp.arange(seq)[None, :] // C_SEL_BLOCK == jnp.arange(LANES)[:, None]).astype(BF16)
            o = nsa_prompt(z, gates, cos, sin, kcv, overlap, expand, batch, seq)
            x = linear_residual(o, prm['w_c_out'], li, x, tn=TN, name="c_out_prompt")
            z3 = z.reshape(batch, seq, -1)
            for br, nm in enumerate(('c_cmp', 'c_slc', 'c_win')):
                rows = min(C_WINDOW, seq) if nm == 'c_win' else seq
                kv = z3[:, seq - rows:, d + br * kv_width:d + (br + 1) * kv_width]
                new.setdefault(nm, []).append(kv.reshape(batch, rows, 2, C_KV_HEADS, HEAD_DIM))
        else:
            x, tail = d_mixer_prompt(x, prm['attn_norm'], i, prm['w_d_group'], prm['d_scale'], li, seq,
                                     name="d_mixer_prompt")
            new.setdefault('d_pool', []).append(tail[:, 1:])
        x = _channel_mixer(x, p, prm, i, "prompt")
    return x.reshape(batch, seq, d), {nm: jnp.stack(v, axis=0) for nm, v in new.items()}


def _run_sample(x3, p, prm, past, page_table):
    n_batch, n_new, d = x3.shape
    assert n_new == 1
    n_heads = d // HEAD_DIM
    rows = SUBLANES_BF16
    x = _pad_rows(x3.reshape(n_batch, d), rows)
    past_len = page_table.shape[1] * past['c_cmp'].shape[2]
    cos, sin = _rope_tables(jnp.full((rows,), past_len))
    new = {}
    depth = prm['attn_norm'].shape[0]
    for i in range(depth):
        kind, li = i % 4, i // 4
        if kind == 0:
            z = norm_linear(x, prm['attn_norm'], i, prm['w_a_qkv'], li, col0=0, modes=A_MODES,
                            head_gain=_a_head_gain(prm, li, n_heads), cos=cos, sin=sin, tn=TN, name="a_qkv_sample")
            caches = [past[f'a_w{g + 1}'][li] for g in range(len(A_PATTERNS))]
            o = a_attention_sample(z, caches, n_batch)
            x = linear_residual(_pad_rows(o, rows), prm['w_a_out'], li, x, tn=TN, name="a_out_sample")
            for g, cache in enumerate(caches):
                kv = z[:n_batch, (3 * g + 1) * d:(3 * g + 3) * d].reshape(n_batch, 1, 2, n_heads, HEAD_DIM)
                new.setdefault(f'a_w{g + 1}', []).append(jnp.concatenate([cache[:, 1:], kv], axis=1))
        elif kind == 1:
            bg, u = b_in(x, prm['attn_norm'], i, prm['w_b_in'], li, tn=TN_WIDE_K, name="b_in_sample")
            hist = past['b_conv'][li]
            x = b_out_sample(u, _pad_rows(hist[:, 1], rows), _pad_rows(hist[:, 0], rows), bg, prm['b_conv'],
                             prm['w_b_out'], li, x, tn=TN, name="b_out_sample")
            new.setdefault('b_conv', []).append(jnp.concatenate([hist[:, 1:], u[:n_batch, None]], axis=1))
        elif kind == 2:
            head_gain, wg, w_cat, pe8, w1_flat, w2, k_gain0 = _c_weights(prm, li, n_heads)
            z = norm_linear(x, prm['attn_norm'], i, prm['w_c_in'], li, col0=0, modes=C_MODES,
                            head_gain=head_gain, cos=cos, sin=sin, tn=TN, name="c_in_sample")
            gates = norm_linear(x, prm['attn_norm'], i, wg, 0, col0=0, modes=(SIGMOID,),
                                head_gain=head_gain[:, :TN], cos=cos, sin=sin, tn=TN, name="c_gate_sample")
            pool_cmp = past['c_cmp'][li]
            kv_width = 2 * C_KV_HEADS * HEAD_DIM
            n_pages = page_table.shape[1]
            fs = cmp_first_second(pool_cmp.reshape(pool_cmp.shape[0], pool_cmp.shape[1], kv_width),
                                  page_table.reshape(-1), 0, w_cat, n_batch, n_pages)
            kcv = cmp_mlp(fs, pe8, w1_flat, w2, k_gain0)
            nsel = -(-(past_len + 1) // C_SEL_BLOCK)
            width = -(-nsel // LANES) * LANES
            overlap = _overlap_matrix(kcv.shape[2], nsel, width)
            zb = z[:n_batch]
            q8 = jnp.pad(zb[:, :d].reshape(n_batch, C_KV_HEADS, C_GROUP, HEAD_DIM),
                         ((0, 0), (0, 0), (0, 8 - C_GROUP), (0, 0)))
            ocmp, idx = nsa_sample_cmp(q8, kcv, overlap, past_len, nsel)
            top_idx = idx[:, :, 0, :C_SEL_TOPN]
            kvh_w = C_KV_HEADS * HEAD_DIM
            new_rows = [zb[:, d + kv_width + j * kvh_w:d + kv_width + (j + 1) * kvh_w]
                        .reshape(n_batch, C_KV_HEADS, 1, HEAD_DIM) for j in range(4)]
            new_kv = jnp.pad(jnp.concatenate(new_rows, axis=2), ((0, 0), (0, 0), (0, 4), (0, 0)))
            gate8 = gates[:n_batch].reshape(n_batch, C_KV_HEADS, LANES)[:, :, :C_GROUP * 3]
            gate8 = jnp.pad(gate8.reshape(n_batch, C_KV_HEADS, C_GROUP, 3),
                            ((0, 0), (0, 0), (0, 8 - C_GROUP), (0, HEAD_DIM - 3)))
            o = nsa_sample(top_idx, page_table, q8, cos[:8], sin[:8], past['c_slc'][li], new_kv, past['c_win'][li],
                           ocmp, gate8, past_len, past_len // C_SEL_BLOCK)
            o = o[:, :, :C_GROUP].reshape(n_batch, d)
            x = linear_residual(_pad_rows(o, rows), prm['w_c_out'], li, x, tn=TN, name="c_out_sample")
            for br, nm in enumerate(('c_cmp', 'c_slc', 'c_win')):
                kv = zb[:, d + br * kv_width:d + (br + 1) * kv_width].reshape(n_batch, 1, 2, C_KV_HEADS, HEAD_DIM)
                if nm == 'c_win':
                    kv = jnp.concatenate([past['c_win'][li][:, 1:], kv], axis=1)
                new.setdefault(nm, []).append(kv)
        else:
            hist = past['d_pool'][li]
            hist_t = jnp.pad(jnp.swapaxes(hist, 0, 1), ((0, 0), (0, rows - n_batch), (0, 0)))
            x, h = d_mixer_sample(x, hist_t, prm['attn_norm'], i, prm['w_d_group'], prm['d_scale'], li,
                                  name="d_mixer_sample")
            new.setdefault('d_pool', []).append(jnp.concatenate([hist[:, 1:], h[:n_batch, None]], axis=1))
        x = _channel_mixer(x, p, prm, i, "sample")
    return x[:n_batch].reshape(n_batch, 1, d), {nm: jnp.stack(v, axis=0) for nm, v in new.items()}


def kernel(x_prompt, x_sample, cache_a_w1, cache_a_w2, cache_a_w3, state_b_conv, cache_c_cmp, cache_c_slc, cache_c_win, state_d_pool, page_table, p_prompt, p_sample, attn_norm, ffn_norm, ple_norm, w_a_qkv, a_q_norm, a_k_norm, w_a_out, w_b_in, b_conv, w_b_out, w_c_in, c_q_norm, c_k_norm, c_cmp_pe, c_cmp_w1, c_cmp_w2, w_c_out, w_d_group, d_scale, w_ffn_in, w_ffn_out, w_ple_proj, w_ple_gate):
    prm = dict(attn_norm=attn_norm, ffn_norm=ffn_norm, ple_norm=ple_norm, w_a_qkv=w_a_qkv, a_q_norm=a_q_norm,
               a_k_norm=a_k_norm, w_a_out=w_a_out, w_b_in=w_b_in, b_conv=b_conv, w_b_out=w_b_out, w_c_in=w_c_in,
               c_q_norm=c_q_norm, c_k_norm=c_k_norm, c_cmp_pe=c_cmp_pe, c_cmp_w1=c_cmp_w1, c_cmp_w2=c_cmp_w2,
               w_c_out=w_c_out, w_d_group=w_d_group, d_scale=d_scale, w_ffn_in=w_ffn_in, w_ffn_out=w_ffn_out,
               w_ple_proj=w_ple_proj, w_ple_gate=w_ple_gate)
    past = dict(a_w1=cache_a_w1, a_w2=cache_a_w2, a_w3=cache_a_w3, b_conv=state_b_conv, c_cmp=cache_c_cmp,
                c_slc=cache_c_slc, c_win=cache_c_win, d_pool=state_d_pool)
    depth = attn_norm.shape[0]
    batch, seq, _ = x_prompt.shape
    n_dec = x_sample.shape[0]
    p_p = p_prompt.reshape(depth, batch * seq, -1)
    p_s = jnp.pad(p_sample.reshape(depth, n_dec, -1), ((0, 0), (0, SUBLANES_BF16 - n_dec), (0, 0)))
    y_prompt, sp = _run_prompt(x_prompt, p_p, prm)
    y_sample, ss = _run_sample(x_sample, p_s, prm, past, page_table)
    return (y_prompt, y_sample,
            sp['a_w1'], ss['a_w1'], sp['a_w2'], ss['a_w2'], sp['a_w3'], ss['a_w3'],
            sp['b_conv'], ss['b_conv'],
            sp['c_cmp'], ss['c_cmp'], sp['c_slc'], ss['c_slc'], sp['c_win'], ss['c_win'],
            sp['d_pool'], ss['d_pool'])
```

```python
import functools

import jax
import jax.numpy as jnp
from jax import lax
from jax.experimental import pallas as pl
from jax.experimental.pallas import tpu as pltpu

F32 = jnp.float32
BF16 = jnp.bfloat16
I32 = jnp.int32

HEAD_DIM = 128
LANES = 128
SUBLANES_BF16 = 16
RMS_EPS = 1e-6
ROPE_THETA = 10000.0
NEG = -1e30
V7X_VMEM_BYTES = 64 * 1024 * 1024
VMEM_LIMIT = V7X_VMEM_BYTES - 8 * 1024 * 1024

A_PATTERNS = ((128, 1), (512, 4), (2048, 16))
A_BLOCK = 128
A_UNROLL = 4
CONV_W = 3
C_KV_HEADS = 4
C_GROUP = 4
C_CMP_LEN = 32
C_CMP_STRIDE = 16
C_SEL_BLOCK = 64
C_SEL_TOPN = 16
C_WINDOW = 512
C_FORCE_SCORE = 1e4
C_QBLOCK = 128
D_POOLS = (2, 4, 8, 16)
D_HIST = max(D_POOLS) - 1
PAGES_PER_GROUP = 16

RAW, NORM, NORM_ROPE, SIGMOID = range(4)


def _cparams(n_axes):
    return pltpu.CompilerParams(dimension_semantics=("arbitrary",) * n_axes,
                                vmem_limit_bytes=VMEM_LIMIT)


def _row_tile(m):
    return 1024 if m % 1024 == 0 else m


def _dot(a, b):
    return jnp.dot(a, b, preferred_element_type=F32)


def _dot_nt(a, b):
    return lax.dot_general(a, b, (((1,), (1,)), ((), ())), preferred_element_type=F32)


def _with_ones(v):
    return jnp.concatenate([v.astype(BF16), jnp.ones(v.shape, BF16)], axis=1)


def _rms(x, g):
    return x * lax.rsqrt(jnp.mean(x * x, axis=-1, keepdims=True) + RMS_EPS) * g


def _rope(y, cos, sin):
    return y * cos + pltpu.roll(y, HEAD_DIM // 2, 1) * sin


def _tile_pred(j, tiles):
    pred = None
    lo = prev = tiles[0]
    runs = []
    for t in tiles[1:]:
        if t != prev + 1:
            runs.append((lo, prev))
            lo = t
        prev = t
    runs.append((lo, prev))
    for lo, hi in runs:
        p = (j == lo) if lo == hi else ((j >= lo) & (j <= hi))
        pred = p if pred is None else (pred | p)
    return pred


def _norm_linear_kernel(x_ref, g_ref, w_ref, hg_ref, cos_ref, sin_ref, o_ref, h_scr, *, modes):
    j = pl.program_id(1)

    @pl.when(j == 0)
    def _():
        h_scr[...] = _rms(x_ref[...], g_ref[...]).astype(BF16)

    acc = _dot(h_scr[...], w_ref[...].astype(BF16))
    tn = acc.shape[1]

    def emit(mode):
        if mode == RAW:
            o_ref[...] = acc.astype(o_ref.dtype)
        elif mode == SIGMOID:
            o_ref[...] = jax.nn.sigmoid(acc).astype(o_ref.dtype)
        else:
            for c in range(tn // HEAD_DIM):
                sl = slice(c * HEAD_DIM, (c + 1) * HEAD_DIM)
                y = _rms(acc[:, sl], hg_ref[:, sl])
                if mode == NORM_ROPE:
                    y = _rope(y, cos_ref[...], sin_ref[...])
                o_ref[:, sl] = y.astype(o_ref.dtype)

    kinds = sorted(set(modes))
    if len(kinds) == 1:
        emit(kinds[0])
    else:
        for m in kinds:
            tiles = [t for t, mm in enumerate(modes) if mm == m]
            pl.when(_tile_pred(j, tiles))(functools.partial(emit, m))


def norm_linear(x, gain, gi, w, wi, *, col0, modes, head_gain, cos, sin, tn, name):
    m, k = x.shape
    tm = _row_tile(m)
    nt = len(modes)
    assert col0 % tn == 0 and cos.shape[0] % tm == 0
    coff = col0 // tn
    n_pos_tiles = cos.shape[0] // tm
    return pl.pallas_call(
        functools.partial(_norm_linear_kernel, modes=tuple(modes)),
        out_shape=jax.ShapeDtypeStruct((m, nt * tn), F32),
        grid=(m // tm, nt),
        in_specs=[
            pl.BlockSpec((tm, k), lambda i, j: (i, 0)),
            pl.BlockSpec((None, 1, k), lambda i, j: (gi, 0, 0)),
            pl.BlockSpec((None, k, tn), lambda i, j: (wi, 0, j + coff)),
            pl.BlockSpec((1, tn), lambda i, j: (0, j)),
            pl.BlockSpec((tm, HEAD_DIM), lambda i, j: (i % n_pos_tiles, 0)),
            pl.BlockSpec((tm, HEAD_DIM), lambda i, j: (i % n_pos_tiles, 0)),
        ],
        out_specs=pl.BlockSpec((tm, tn), lambda i, j: (i, j)),
        scratch_shapes=[pltpu.VMEM((tm, k), BF16)],
        compiler_params=_cparams(2),
        name=name,
    )(x, gain.reshape(gain.shape[0], 1, k), w, head_gain, cos, sin)


def _linear_res_kernel(a_ref, w_ref, r_ref, o_ref):
    o_ref[...] = r_ref[...] + _dot(a_ref[...].astype(BF16), w_ref[...].astype(BF16))


def linear_residual(a, w, li, res, *, tn, name):
    m, k = a.shape
    n = w.shape[-1]
    tm = _row_tile(m)
    return pl.pallas_call(
        _linear_res_kernel,
        out_shape=jax.ShapeDtypeStruct((m, n), F32),
        grid=(m // tm, n // tn),
        in_specs=[
            pl.BlockSpec((tm, k), lambda i, j: (i, 0)),
            pl.BlockSpec((None, k, tn), lambda i, j: (li, 0, j)),
            pl.BlockSpec((tm, tn), lambda i, j: (i, j)),
        ],
        out_specs=pl.BlockSpec((tm, tn), lambda i, j: (i, j)),
        compiler_params=_cparams(2),
        name=name,
    )(a, w, res)


def _swiglu_in_kernel(x_ref, g_ref, wg_ref, wu_ref, o_ref, h_scr):
    @pl.when(pl.program_id(1) == 0)
    def _():
        h_scr[...] = _rms(x_ref[...], g_ref[...]).astype(BF16)

    h = h_scr[...]
    a = _dot(h, wg_ref[...].astype(BF16))
    b = _dot(h, wu_ref[...].astype(BF16))
    o_ref[...] = (a * jax.nn.sigmoid(a) * b).astype(o_ref.dtype)


def swiglu_in(x, gain, w, li, *, tn, name):
    m, k = x.shape
    f = w.shape[-1] // 2
    tm = _row_tile(m)
    nt = f // tn
    return pl.pallas_call(
        _swiglu_in_kernel,
        out_shape=jax.ShapeDtypeStruct((m, f), BF16),
        grid=(m // tm, nt),
        in_specs=[
            pl.BlockSpec((tm, k), lambda i, j: (i, 0)),
            pl.BlockSpec((None, 1, k), lambda i, j: (li, 0, 0)),
            pl.BlockSpec((None, k, tn), lambda i, j: (li, 0, j)),
            pl.BlockSpec((None, k, tn), lambda i, j: (li, 0, j + nt)),
        ],
        out_specs=pl.BlockSpec((tm, tn), lambda i, j: (i, j)),
        scratch_shapes=[pltpu.VMEM((tm, k), BF16)],
        compiler_params=_cparams(2),
        name=name,
    )(x, gain.reshape(gain.shape[0], 1, k), w, w)


def _ple_kernel(x_ref, g_ref, p_ref, wp_ref, wg_ref, r_ref, o_ref, h_scr):
    @pl.when(pl.program_id(1) == 0)
    def _():
        h_scr[...] = _rms(x_ref[...], g_ref[...]).astype(BF16)

    gate = jax.nn.sigmoid(_dot(h_scr[...], wg_ref[...].astype(BF16)))
    proj = _dot(p_ref[...].astype(BF16), wp_ref[...].astype(BF16))
    o_ref[...] = r_ref[...] + proj * gate


def ple(x, gain, p, wp, wg, li, *, tn, name):
    m, k = x.shape
    pd = p.shape[-1]
    tm = _row_tile(m)
    assert p.shape[1] == m
    return pl.pallas_call(
        _ple_kernel,
        out_shape=jax.ShapeDtypeStruct((m, k), F32),
        grid=(m // tm, k // tn),
        in_specs=[
            pl.BlockSpec((tm, k), lambda i, j: (i, 0)),
            pl.BlockSpec((None, 1, k), lambda i, j: (li, 0, 0)),
            pl.BlockSpec((None, tm, pd), lambda i, j: (li, i, 0)),
            pl.BlockSpec((None, pd, tn), lambda i, j: (li, 0, j)),
            pl.BlockSpec((None, k, tn), lambda i, j: (li, 0, j)),
            pl.BlockSpec((tm, tn), lambda i, j: (i, j)),
        ],
        out_specs=pl.BlockSpec((tm, tn), lambda i, j: (i, j)),
        scratch_shapes=[pltpu.VMEM((tm, k), BF16)],
        compiler_params=_cparams(2),
        name=name,
    )(x, gain.reshape(gain.shape[0], 1, k), p, wp, wg, x)


def _a_prompt_kernel(*refs, seq):
    qkv_refs, o_ref = refs[:9], refs[9]
    acc_scr, m_scr, l_scr = refs[10:]
    scale = HEAD_DIM ** -0.5
    row = lax.broadcasted_iota(I32, (A_BLOCK, A_BLOCK), 0)
    col = lax.broadcasted_iota(I32, (A_BLOCK, A_BLOCK), 1)
    cur_ok = col <= row
    prev_ok = col >= row
    wide = (A_BLOCK, HEAD_DIM)
    first = len(A_PATTERNS) - 1
    for g, (win, dil) in reversed(list(enumerate(A_PATTERNS))):
        q_ref, k_ref, v_ref = qkv_refs[3 * g:3 * g + 3]
        nb = seq // dil // A_BLOCK

        def rows_of(r, n, dil=dil):
            start = r + dil * A_BLOCK * n
            if dil == 1:
                return pl.ds(pl.multiple_of(start, A_BLOCK), A_BLOCK)
            return pl.ds(start, A_BLOCK, stride=dil)

        def block(idx, carry, g=g, nb=nb, q_ref=q_ref, k_ref=k_ref, v_ref=v_ref, rows_of=rows_of):
            r, n = idx // nb, idx % nb
            rows = rows_of(r, n)
            q = (q_ref[0, rows, :] * scale).astype(BF16)
            s_c = jnp.where(cur_ok, _dot_nt(q, k_ref[0, rows, :].astype(BF16)), NEG)
            if nb > 1:
                prows = rows_of(r, jnp.maximum(n - 1, 0))
                s_p = jnp.where(prev_ok & (n > 0), _dot_nt(q, k_ref[0, prows, :].astype(BF16)), NEG)
                m = jnp.max(jnp.maximum(s_c, s_p), axis=1, keepdims=True)
            else:
                m = jnp.max(s_c, axis=1, keepdims=True)
            ext = _dot(jnp.exp(s_c - m).astype(BF16), _with_ones(v_ref[0, rows, :]))
            if nb > 1:
                ext = ext + _dot(jnp.exp(s_p - m).astype(BF16), _with_ones(v_ref[0, prows, :]))
            acc, den = ext[:, :HEAD_DIM], ext[:, HEAD_DIM:]
            if g == first:
                acc_scr[rows, :] = acc
                m_scr[rows, :] = jnp.broadcast_to(m, wide)
                l_scr[rows, :] = den
            else:
                m_old = m_scr[rows, :]
                m_new = jnp.maximum(m_old, m)
                a_old, a_blk = jnp.exp(m_old - m_new), jnp.exp(m - m_new)
                acc_scr[rows, :] = a_old * acc_scr[rows, :] + a_blk * acc
                l_scr[rows, :] = a_old * l_scr[rows, :] + a_blk * den
                m_scr[rows, :] = m_new
            return carry

        lax.fori_loop(0, dil * nb, block, 0, unroll=A_UNROLL)
    o_ref[0] = (acc_scr[...] / l_scr[...]).astype(o_ref.dtype)


def a_attention_prompt(z, batch, seq):
    nc = z.shape[1]
    d = nc // (3 * len(A_PATTERNS))
    n_heads = d // HEAD_DIM
    assert all(win // dil == A_BLOCK and seq % (dil * A_BLOCK) == 0 for win, dil in A_PATTERNS)
    zv = z.reshape(batch, seq, nc)
    in_specs = [pl.BlockSpec((1, seq, HEAD_DIM), functools.partial(lambda b, h, c: (b, 0, c * n_heads + h), c=c))
                for c in range(3 * len(A_PATTERNS))]
    out = pl.pallas_call(
        functools.partial(_a_prompt_kernel, seq=seq),
        out_shape=jax.ShapeDtypeStruct((batch, seq, d), BF16),
        grid=(batch, n_heads),
        in_specs=in_specs,
        out_specs=pl.BlockSpec((1, seq, HEAD_DIM), lambda b, h: (b, 0, h)),
        scratch_shapes=[pltpu.VMEM((seq, HEAD_DIM), F32)] * 3,
        compiler_params=_cparams(2),
        name="a_attn_prompt",
    )(*([zv] * (3 * len(A_PATTERNS))))
    return out.reshape(batch * seq, d)


def _a_sample_kernel(z_ref, *refs, n_heads, d):
    cache_refs, o_ref = refs[:-1], refs[-1]
    scale = HEAD_DIM ** -0.5
    for h in range(n_heads):
        ms, dens, accs = [], [], []
        for g in range(len(A_PATTERNS)):
            kv_ref = cache_refs[g]
            c0 = g * 3 * d + h * HEAD_DIM
            q = z_ref[0, :, c0:c0 + HEAD_DIM] * scale
            k_new = z_ref[0, :, c0 + d:c0 + d + HEAD_DIM]
            v_new = z_ref[0, :, c0 + 2 * d:c0 + 2 * d + HEAD_DIM]
            q8 = jnp.broadcast_to(q, (8, HEAD_DIM))
            s = _dot_nt(q8.astype(BF16), kv_ref[:, 0, h, :].astype(BF16))
            s_new = jnp.sum(q8 * k_new, axis=1, keepdims=True)
            m = jnp.maximum(jnp.max(s, axis=1, keepdims=True), s_new)
            p = jnp.exp(s - m)
            p_new = jnp.exp(s_new - m)
            dens.append(jnp.sum(p, axis=1, keepdims=True) + p_new)
            accs.append(_dot(p.astype(BF16), kv_ref[:, 1, h, :].astype(BF16)) + p_new * v_new)
            ms.append(m)
        mt = jnp.maximum(jnp.maximum(ms[0], ms[1]), ms[2])
        es = [jnp.exp(mm - mt) for mm in ms]
        tot = es[0] * dens[0] + es[1] * dens[1] + es[2] * dens[2]
        num = es[0] * accs[0] + es[1] * accs[1] + es[2] * accs[2]
        o_ref[0, :, h * HEAD_DIM:(h + 1) * HEAD_DIM] = num / tot


def a_attention_sample(z, caches, li, n_batch):
    d = z.shape[1] // 9
    n_heads = d // HEAD_DIM
    in_specs = [pl.BlockSpec((1, 1, z.shape[1]), lambda b: (b, 0, 0))]
    args = [z[:, None, :]]
    for (win, dil), cache in zip(A_PATTERNS, caches):
        lb = cache.shape[2]
        assert lb == win and (win // dil) == A_BLOCK
        in_specs.append(pl.BlockSpec((None, None, A_BLOCK, None, 2, n_heads, HEAD_DIM),
                                     lambda b: (li, b, 0, 0, 0, 0, 0)))
        args.append(cache.reshape(cache.shape[0], n_batch, lb // dil, dil, 2, n_heads, HEAD_DIM))
    out = pl.pallas_call(
        functools.partial(_a_sample_kernel, n_heads=d // HEAD_DIM, d=d),
        out_shape=jax.ShapeDtypeStruct((n_batch, 8, d), F32),
        grid=(n_batch,),
        in_specs=in_specs,
        out_specs=pl.BlockSpec((1, 8, d), lambda b: (b, 0, 0)),
        compiler_params=_cparams(1),
        name="a_attn_sample",
    )(*args)
    return out[:, 0]


def _b_in_kernel(x_ref, g_ref, wb_ref, wc_ref, wx_ref, bg_ref, u_ref, h_scr):
    @pl.when(pl.program_id(1) == 0)
    def _():
        h_scr[...] = _rms(x_ref[...], g_ref[...]).astype(BF16)

    h = h_scr[...]
    bg_ref[...] = _dot(h, wb_ref[...].astype(BF16))
    u_ref[...] = _dot(h, wc_ref[...].astype(BF16)) * _dot(h, wx_ref[...].astype(BF16))


def b_in(x, gain, gi, w, li, *, tn, name):
    m, k = x.shape
    d = w.shape[-1] // 3
    tm = _row_tile(m)
    nt = d // tn
    out_spec = pl.BlockSpec((tm, tn), lambda i, j: (i, j))
    return pl.pallas_call(
        _b_in_kernel,
        out_shape=[jax.ShapeDtypeStruct((m, d), F32)] * 2,
        grid=(m // tm, nt),
        in_specs=[
            pl.BlockSpec((tm, k), lambda i, j: (i, 0)),
            pl.BlockSpec((None, 1, k), lambda i, j: (gi, 0, 0)),
            pl.BlockSpec((None, k, tn), lambda i, j: (li, 0, j)),
            pl.BlockSpec((None, k, tn), lambda i, j: (li, 0, j + nt)),
            pl.BlockSpec((None, k, tn), lambda i, j: (li, 0, j + 2 * nt)),
        ],
        out_specs=[out_spec, out_spec],
        scratch_shapes=[pltpu.VMEM((tm, k), BF16)],
        compiler_params=_cparams(2),
        name=name,
    )(x, gain.reshape(gain.shape[0], 1, k), w, w, w)


def _b_out_prompt_kernel(u_ref, up_ref, bg_ref, cw_ref, w_ref, r_ref, o_ref, ext_scr, a_scr, *, tiles_per_seq):
    i = pl.program_id(0)
    tm = u_ref.shape[0]

    @pl.when(pl.program_id(1) == 0)
    def _():
        ext_scr[0:8, :] = jnp.where(i % tiles_per_seq == 0, 0.0, up_ref[...])
        ext_scr[8:, :] = u_ref[...]
        y = (cw_ref[0:1, :] * ext_scr[pl.ds(6, tm), :] + cw_ref[1:2, :] * ext_scr[pl.ds(7, tm), :]
             + cw_ref[2:3, :] * ext_scr[pl.ds(8, tm), :])
        a_scr[...] = (bg_ref[...] * y).astype(BF16)

    o_ref[...] = r_ref[...] + _dot(a_scr[...], w_ref[...].astype(BF16))


def b_out_prompt(u, bg, conv_w, w, li, res, seq, *, tn, name):
    m, d = u.shape
    tm = 512
    assert seq % tm == 0
    return pl.pallas_call(
        functools.partial(_b_out_prompt_kernel, tiles_per_seq=seq // tm),
        out_shape=jax.ShapeDtypeStruct((m, d), F32),
        grid=(m // tm, d // tn),
        in_specs=[
            pl.BlockSpec((tm, d), lambda i, j: (i, 0)),
            pl.BlockSpec((8, d), lambda i, j: (jnp.maximum(i * (tm // 8) - 1, 0), 0)),
            pl.BlockSpec((tm, d), lambda i, j: (i, 0)),
            pl.BlockSpec((None, CONV_W, d), lambda i, j: (li, 0, 0)),
            pl.BlockSpec((None, d, tn), lambda i, j: (li, 0, j)),
            pl.BlockSpec((tm, tn), lambda i, j: (i, j)),
        ],
        out_specs=pl.BlockSpec((tm, tn), lambda i, j: (i, j)),
        scratch_shapes=[pltpu.VMEM((tm + 8, d), F32), pltpu.VMEM((tm, d), BF16)],
        compiler_params=_cparams(2),
        name=name,
    )(u, u, bg, conv_w, w, res)


def _b_out_sample_kernel(u_ref, um1_ref, um2_ref, bg_ref, cw_ref, w_ref, r_ref, o_ref):
    y = cw_ref[0:1, :] * um2_ref[...] + cw_ref[1:2, :] * um1_ref[...] + cw_ref[2:3, :] * u_ref[...]
    o_ref[...] = r_ref[...] + _dot((bg_ref[...] * y).astype(BF16), w_ref[...].astype(BF16))


def b_out_sample(u, um1, um2, bg, conv_w, w, li, res, *, tn, name):
    m, d = u.shape
    full = pl.BlockSpec((m, d), lambda j: (0, 0))
    return pl.pallas_call(
        _b_out_sample_kernel,
        out_shape=jax.ShapeDtypeStruct((m, d), F32),
        grid=(d // tn,),
        in_specs=[full, full, full, full,
                  pl.BlockSpec((None, CONV_W, d), lambda j: (li, 0, 0)),
                  pl.BlockSpec((None, d, tn), lambda j: (li, 0, j)),
                  pl.BlockSpec((m, tn), lambda j: (0, j))],
        out_specs=pl.BlockSpec((m, tn), lambda j: (0, j)),
        compiler_params=_cparams(1),
        name=name,
    )(u, um1, um2, bg, conv_w, w, res)


def _d_prompt_kernel(x_ref, xp_ref, g_ref, w_ref, sc_ref, r_ref, o_ref, ht_ref, ext_scr, *, tiles_per_seq):
    i = pl.program_id(0)
    j = pl.program_id(1)
    tm = x_ref.shape[0]
    halo = D_HIST + 1
    gw = w_ref.shape[0]

    @pl.when(j == 0)
    def _():
        ext_scr[0:halo, :] = jnp.where(i % tiles_per_seq == 0, 0.0, _rms(xp_ref[...], g_ref[...]))
        ext_scr[halo:, :] = _rms(x_ref[...], g_ref[...])
        ht_ref[0] = ext_scr[pl.ds(tm, halo), :]

    pos = (i % tiles_per_seq) * tm + lax.broadcasted_iota(I32, (tm, 1), 0)
    for g, win in enumerate(D_POOLS):
        @pl.when(j == g)
        def _(g=g, win=win):
            cols = slice(g * gw, (g + 1) * gw)
            tot = ext_scr[pl.ds(halo, tm), cols]
            h = tot
            for back in range(1, win):
                tot = tot + ext_scr[pl.ds(halo - back, tm), cols]
            count = jnp.minimum(pos + 1, win).astype(F32)
            pooled = tot / count - h
            o_ref[...] = r_ref[...] + _dot(pooled.astype(BF16), w_ref[...].astype(BF16)) * sc_ref[...]


def d_mixer_prompt(x, gain, li_norm, w_group, scale, li, seq, *, name):
    m, d = x.shape
    n_groups, gw = w_group.shape[1], w_group.shape[2]
    tm = 512
    halo = D_HIST + 1
    assert seq % tm == 0 and n_groups == len(D_POOLS)
    tps = seq // tm
    return pl.pallas_call(
        functools.partial(_d_prompt_kernel, tiles_per_seq=tps),
        out_shape=[jax.ShapeDtypeStruct((m, d), F32), jax.ShapeDtypeStruct((m // seq, halo, d), F32)],
        grid=(m // tm, n_groups),
        in_specs=[
            pl.BlockSpec((tm, d), lambda i, j: (i, 0)),
            pl.BlockSpec((halo, d), lambda i, j: (jnp.maximum(i * (tm // halo) - 1, 0), 0)),
            pl.BlockSpec((None, 1, d), lambda i, j: (li_norm, 0, 0)),
            pl.BlockSpec((None, None, gw, gw), lambda i, j: (li, j, 0, 0)),
            pl.BlockSpec((None, 1, gw), lambda i, j: (li, 0, j)),
            pl.BlockSpec((tm, gw), lambda i, j: (i, j)),
        ],
        out_specs=[pl.BlockSpec((tm, gw), lambda i, j: (i, j)),
                   pl.BlockSpec((1, halo, d), lambda i, j: (i // tps, 0, 0))],
        scratch_shapes=[pltpu.VMEM((tm + halo, d), F32)],
        compiler_params=_cparams(2),
        name=name,
    )(x, x, gain.reshape(gain.shape[0], 1, d), w_group, scale.reshape(scale.shape[0], 1, d), x)


def _d_sample_kernel(x_ref, hist_ref, g_ref, w_ref, sc_ref, o_ref, h_ref):
    h = _rms(x_ref[...], g_ref[...])
    h_ref[...] = h
    gw = w_ref.shape[1]
    for g, win in enumerate(D_POOLS):
        cols = slice(g * gw, (g + 1) * gw)
        tot = h[:, cols]
        for back in range(1, win):
            tot = tot + hist_ref[D_HIST - back, :, cols]
        pooled = tot / float(win) - h[:, cols]
        o_ref[:, cols] = x_ref[:, cols] + _dot(pooled.astype(BF16), w_ref[g].astype(BF16)) * sc_ref[:, cols]


def d_mixer_sample(x, hist_t, gain, li_norm, w_group, scale, li, *, name):
    m, d = x.shape
    n_groups, gw = w_group.shape[1], w_group.shape[2]
    assert hist_t.shape[0] == D_HIST
    return pl.pallas_call(
        _d_sample_kernel,
        out_shape=[jax.ShapeDtypeStruct((m, d), F32)] * 2,
        grid=(1,),
        in_specs=[
            pl.BlockSpec((m, d), lambda i: (0, 0)),
            pl.BlockSpec(hist_t.shape, lambda i: (0, 0, 0)),
            pl.BlockSpec((None, 1, d), lambda i: (li_norm, 0, 0)),
            pl.BlockSpec((None, n_groups, gw, gw), lambda i: (li, 0, 0, 0)),
            pl.BlockSpec((None, 1, d), lambda i: (li, 0, 0)),
        ],
        out_specs=[pl.BlockSpec((m, d), lambda i: (0, 0))] * 2,
        compiler_params=_cparams(1),
        name=name,
    )(x, hist_t, gain.reshape(gain.shape[0], 1, d), w_group, scale.reshape(scale.shape[0], 1, d))


def _cmp_fs_kernel(pt_ref, page_ref, w_ref, o_ref, ring_scr):
    del pt_ref
    slot = pl.program_id(1) % PAGES_PER_GROUP
    rows = page_ref.shape[1]
    for ck in range(2 * C_KV_HEADS):
        ring_scr[ck, pl.ds(pl.multiple_of(slot * rows, rows), rows), :] = page_ref[0, :, ck * HEAD_DIM:(ck + 1) * HEAD_DIM]

    @pl.when(slot == PAGES_PER_GROUP - 1)
    def _():
        n_chunks = PAGES_PER_GROUP * rows // C_CMP_STRIDE
        for ck in range(2 * C_KV_HEADS):
            c = ck // C_KV_HEADS
            acc = jnp.zeros((n_chunks, w_ref.shape[-1]), F32)
            for p in range(C_CMP_STRIDE):
                lhs = ring_scr[ck, pl.ds(p, n_chunks, stride=C_CMP_STRIDE), :]
                acc = acc + _dot(lhs.astype(BF16), w_ref[c, p])
            o_ref[0, ck] = acc


def cmp_first_second(pool, page_ids, col_block, w_cat, n_batch, n_pages):
    rows = pool.shape[1]
    width = 2 * C_KV_HEADS * HEAD_DIM
    assert n_pages % PAGES_PER_GROUP == 0 and rows % C_CMP_STRIDE == 0
    chunks_per_group = PAGES_PER_GROUP * rows // C_CMP_STRIDE
    n_out = w_cat.shape[-1]
    return pl.pallas_call(
        _cmp_fs_kernel,
        out_shape=jax.ShapeDtypeStruct((n_batch, 2 * C_KV_HEADS, n_pages * rows // C_CMP_STRIDE, n_out), F32),
        grid_spec=pltpu.PrefetchScalarGridSpec(
            num_scalar_prefetch=1,
            grid=(n_batch, n_pages),
            in_specs=[
                pl.BlockSpec((1, rows, width), lambda b, pg, pt: (pt[b * n_pages + pg], 0, col_block)),
                pl.BlockSpec(w_cat.shape, lambda b, pg, pt: (0, 0, 0, 0)),
            ],
            out_specs=pl.BlockSpec((1, 2 * C_KV_HEADS, chunks_per_group, n_out),
                                   lambda b, pg, pt: (b, 0, pg // PAGES_PER_GROUP, 0)),
            scratch_shapes=[pltpu.VMEM((width // HEAD_DIM, PAGES_PER_GROUP * rows, HEAD_DIM), F32)],
        ),
        compiler_params=_cparams(2),
        name="c_cmp_first_second",
    )(page_ids, pool, w_cat)


def _cmp_fs_paged_kernel(pt_ref, *refs):
    del pt_ref
    page_refs, (w_ref, o_ref) = refs[:PAGES_PER_GROUP], refs[PAGES_PER_GROUP:]
    chunks_per_page = page_refs[0].shape[0] // C_CMP_STRIDE
    for ck in range(2 * C_KV_HEADS):
        c, k = divmod(ck, C_KV_HEADS)
        acc = jnp.zeros((PAGES_PER_GROUP * chunks_per_page, w_ref.shape[-1]), F32)
        for p in range(C_CMP_STRIDE):
            lhs = jnp.concatenate([pr[pl.ds(p, chunks_per_page, stride=C_CMP_STRIDE), c, k, :] for pr in page_refs],
                                  axis=0)
            acc = acc + _dot(lhs.astype(BF16), w_ref[c, p])
        o_ref[0, ck] = acc


def cmp_first_second_paged(pool, li, page_table, w_cat):
    n_batch, n_pages = page_table.shape
    rows = pool.shape[2]
    assert n_pages % PAGES_PER_GROUP == 0 and rows % C_CMP_STRIDE == 0
    chunks_per_group = PAGES_PER_GROUP * rows // C_CMP_STRIDE
    n_out = w_cat.shape[-1]

    def page_spec(s):
        return pl.BlockSpec((None, None, rows, 2, C_KV_HEADS, HEAD_DIM),
                            lambda b, grp, pt: (li, pt[b * n_pages + grp * PAGES_PER_GROUP + s], 0, 0, 0, 0))

    return pl.pallas_call(
        _cmp_fs_paged_kernel,
        out_shape=jax.ShapeDtypeStruct((n_batch, 2 * C_KV_HEADS, n_pages * rows // C_CMP_STRIDE, n_out), F32),
        grid_spec=pltpu.PrefetchScalarGridSpec(
            num_scalar_prefetch=1,
            grid=(n_batch, n_pages // PAGES_PER_GROUP),
            in_specs=[page_spec(s) for s in range(PAGES_PER_GROUP)]
            + [pl.BlockSpec(w_cat.shape, lambda b, grp, pt: (0, 0, 0, 0))],
            out_specs=pl.BlockSpec((1, 2 * C_KV_HEADS, chunks_per_group, n_out), lambda b, grp, pt: (b, 0, grp, 0)),
        ),
        compiler_params=_cparams(2),
        name="c_cmp_first_second_paged",
    )(page_table.reshape(-1), *([pool] * PAGES_PER_GROUP), w_cat)


def _cmp_mlp_kernel(fs_ref, pe_ref, w1_ref, w2_ref, kg_ref, o_ref):
    ck = pl.program_id(1)
    nch = fs_ref.shape[2]
    hid_w = w2_ref.shape[1]
    pe_term = _dot(pe_ref[0].astype(BF16), w1_ref[0].astype(BF16))[0:1, :]
    first = fs_ref[0, 0, :, 0:hid_w]
    second = fs_ref[0, 0, :, hid_w:2 * hid_w]
    nxt = pltpu.roll(second, nch - 1, 0)
    last = lax.broadcasted_iota(I32, (nch, 1), 0) == nch - 1
    pre = first + jnp.where(last, 0.0, nxt) + pe_term
    hid = pre * jax.nn.sigmoid(pre)
    out = _dot(hid.astype(BF16), w2_ref[0].astype(BF16))

    @pl.when(ck < C_KV_HEADS)
    def _():
        o_ref[0, 0] = _rms(out, kg_ref[...])

    @pl.when(ck >= C_KV_HEADS)
    def _():
        o_ref[0, 0] = out


def cmp_mlp(fs, pe8, w1_flat, w2, k_gain0):
    n_batch, n_ck, nch, _ = fs.shape
    return pl.pallas_call(
        _cmp_mlp_kernel,
        out_shape=jax.ShapeDtypeStruct((n_batch, n_ck, nch, HEAD_DIM), F32),
        grid=(n_batch, n_ck),
        in_specs=[
            pl.BlockSpec((1, 1, nch, fs.shape[-1]), lambda b, ck: (b, ck, 0, 0)),
            pl.BlockSpec((1,) + pe8.shape[1:], lambda b, ck: (ck // C_KV_HEADS, 0, 0)),
            pl.BlockSpec((1,) + w1_flat.shape[1:], lambda b, ck: (ck // C_KV_HEADS, 0, 0)),
            pl.BlockSpec((1,) + w2.shape[1:], lambda b, ck: (ck // C_KV_HEADS, 0, 0)),
            pl.BlockSpec((1, HEAD_DIM), lambda b, ck: (0, 0)),
        ],
        out_specs=pl.BlockSpec((1, 1, nch, HEAD_DIM), lambda b, ck: (b, ck, 0, 0)),
        compiler_params=_cparams(2),
        name="c_cmp_mlp",
    )(fs, pe8, w1_flat, w2, k_gain0)


def _masked_softmax_rows(s, ok):
    s = jnp.where(ok, s, NEG)
    m = jnp.max(s, axis=1, keepdims=True)
    e = jnp.where(ok, jnp.exp(s - m), 0.0)
    return e, jnp.maximum(jnp.sum(e, axis=1, keepdims=True), 1e-30)


def _nsa_prompt_kernel(q_ref, gate_ref, cos_ref, sin_ref, kc_ref, vc_ref, ks_ref, vs_ref, kw_ref, vw_ref,
                       ov_ref, ex_ref, o_ref, *, seq, nsel):
    qb = q_ref.shape[0]
    nch = kc_ref.shape[2]
    t0 = pl.program_id(2) * qb
    t = t0 + lax.broadcasted_iota(I32, (qb, 1), 0)
    scale = HEAD_DIM ** -0.5
    qs = [q_ref[:, g * HEAD_DIM:(g + 1) * HEAD_DIM] for g in range(C_GROUP)]

    cmp_ok = lax.broadcasted_iota(I32, (qb, nch), 1) * C_CMP_STRIDE + (C_CMP_LEN - 1) <= t
    kc = kc_ref[0, 0].astype(BF16)
    vc = vc_ref[0, 0].astype(BF16)
    o_cmp = []
    p_sum = jnp.zeros((qb, nch), F32)
    for g in range(C_GROUP):
        e, den = _masked_softmax_rows(_dot_nt((qs[g] * scale).astype(BF16), kc), cmp_ok)
        p = e / den
        o_cmp.append(_dot(p.astype(BF16), vc))
        p_sum = p_sum + p

    p_hi = p_sum.astype(BF16)
    p_lo = (p_sum - p_hi.astype(F32)).astype(BF16)
    imp = _dot(p_hi, ov_ref[...]) + _dot(p_lo, ov_ref[...])
    lane = lax.broadcasted_iota(I32, (qb, LANES), 1)
    cur = t // C_SEL_BLOCK
    causal = (lane <= cur) & (lane < nsel)
    forced = (lane == 0) | (lane == cur) | (lane == cur - 1)
    imp = jnp.where(causal, jnp.where(forced, C_FORCE_SCORE, imp), NEG)
    rank = jnp.zeros((qb, LANES), I32)
    for jp in range(nsel):
        c = imp[:, jp:jp + 1]
        beats = (c > imp) | ((c == imp) & (lane > jp))
        rank = rank + beats.astype(I32)
    sel = jnp.where((rank < C_SEL_TOPN) & causal, 1.0, 0.0).astype(BF16)
    sel_keys = _dot(sel, ex_ref[...])
    slc_ok = (sel_keys > 0.5) & (lax.broadcasted_iota(I32, (qb, seq), 1) <= t)
    slc_bias = jnp.where(slc_ok, 0.0, NEG)

    wk = C_WINDOW + qb
    start = pl.multiple_of(jnp.maximum(t0 - C_WINDOW, 0), qb)
    dist = t - (start + lax.broadcasted_iota(I32, (qb, wk), 1))
    win_bias = jnp.where((dist >= 0) & (dist < C_WINDOW), 0.0, NEG)

    ks = ks_ref[0].astype(BF16)
    vs = vs_ref[0].astype(BF16)
    kw = kw_ref[0, pl.ds(start, wk), :].astype(BF16)
    vw = vw_ref[0, pl.ds(start, wk), :].astype(BF16)
    for g in range(C_GROUP):
        qr = (_rope(qs[g], cos_ref[...], sin_ref[...]) * scale).astype(BF16)
        s = _dot_nt(qr, ks) + slc_bias
        e = jnp.exp(s - jnp.max(s, axis=1, keepdims=True))
        o_slc = _dot(e.astype(BF16), vs) / jnp.sum(e, axis=1, keepdims=True)
        s = _dot_nt(qr, kw) + win_bias
        e = jnp.exp(s - jnp.max(s, axis=1, keepdims=True))
        o_win = _dot(e.astype(BF16), vw) / jnp.sum(e, axis=1, keepdims=True)
        o = (gate_ref[:, 3 * g:3 * g + 1] * o_cmp[g] + gate_ref[:, 3 * g + 1:3 * g + 2] * o_slc
             + gate_ref[:, 3 * g + 2:3 * g + 3] * o_win)
        o_ref[:, g * HEAD_DIM:(g + 1) * HEAD_DIM] = o.astype(o_ref.dtype)


def nsa_prompt(z, gates, cos, sin, kcv, overlap, expand, batch, seq):
    m, nc = z.shape
    d = C_KV_HEADS * C_GROUP * HEAD_DIM
    qb = C_QBLOCK
    nq = seq // qb
    nsel = -(-seq // C_SEL_BLOCK)
    assert nsel <= LANES and seq % qb == 0 and seq >= C_WINDOW + qb
    nch = kcv.shape[2]
    zv = z.reshape(batch, seq, nc)
    cb = d // HEAD_DIM
    kvw = C_KV_HEADS

    def kv_spec(branch, kv):
        off = cb + (branch * 2 + kv) * kvw
        return pl.BlockSpec((1, seq, HEAD_DIM), lambda b, k, q: (b, 0, off + k))

    return pl.pallas_call(
        functools.partial(_nsa_prompt_kernel, seq=seq, nsel=nsel),
        out_shape=jax.ShapeDtypeStruct((m, d), BF16),
        grid=(batch, C_KV_HEADS, nq),
        in_specs=[
            pl.BlockSpec((qb, C_GROUP * HEAD_DIM), lambda b, k, q: (b * nq + q, k)),
            pl.BlockSpec((qb, LANES), lambda b, k, q: (b * nq + q, k)),
            pl.BlockSpec((qb, HEAD_DIM), lambda b, k, q: (q, 0)),
            pl.BlockSpec((qb, HEAD_DIM), lambda b, k, q: (q, 0)),
            pl.BlockSpec((1, 1, nch, HEAD_DIM), lambda b, k, q: (b, k, 0, 0)),
            pl.BlockSpec((1, 1, nch, HEAD_DIM), lambda b, k, q: (b, C_KV_HEADS + k, 0, 0)),
            kv_spec(1, 0), kv_spec(1, 1), kv_spec(2, 0), kv_spec(2, 1),
            pl.BlockSpec(overlap.shape, lambda b, k, q: (0, 0)),
            pl.BlockSpec(expand.shape, lambda b, k, q: (0, 0)),
        ],
        out_specs=pl.BlockSpec((qb, C_GROUP * HEAD_DIM), lambda b, k, q: (b * nq + q, k)),
        compiler_params=_cparams(3),
        name="c_nsa_prompt",
    )(z, gates, cos, sin, kcv, kcv, zv, zv, zv, zv, overlap, expand)


def _nsa_sample_cmp_kernel(q_ref, kc_ref, vc_ref, ov_ref, ocmp_ref, idx_ref, *, t, nsel):
    nch = kc_ref.shape[2]
    scale = HEAD_DIM ** -0.5
    q = (q_ref[0, 0] * scale).astype(BF16)
    cmp_ok = lax.broadcasted_iota(I32, (8, nch), 1) * C_CMP_STRIDE + (C_CMP_LEN - 1) <= t
    e, den = _masked_softmax_rows(_dot_nt(q, kc_ref[0, 0].astype(BF16)), cmp_ok)
    p = e / den
    ocmp_ref[0, 0] = _dot(p.astype(BF16), vc_ref[0, 0].astype(BF16))
    real = lax.broadcasted_iota(I32, (8, nch), 0) < C_GROUP
    p_sum = jnp.broadcast_to(jnp.sum(jnp.where(real, p, 0.0), axis=0, keepdims=True), (8, nch))
    p_hi = p_sum.astype(BF16)
    p_lo = (p_sum - p_hi.astype(F32)).astype(BF16)
    imp = _dot(p_hi, ov_ref[...]) + _dot(p_lo, ov_ref[...])
    width = imp.shape[1]
    lane = lax.broadcasted_iota(I32, (8, width), 1)
    cur = t // C_SEL_BLOCK
    causal = (lane <= cur) & (lane < nsel)
    forced = (lane == 0) | (lane == cur) | (lane == cur - 1)
    work = jnp.where(causal, jnp.where(forced, C_FORCE_SCORE, imp), NEG)
    lane_f = lane.astype(F32)
    out_lane = lax.broadcasted_iota(I32, (8, LANES), 1)
    idx = jnp.full((8, LANES), -1, I32)
    for r in range(C_SEL_TOPN):
        best = jnp.max(work, axis=1, keepdims=True)
        pick = jnp.min(jnp.where(work == best, lane_f, float(width)), axis=1, keepdims=True)
        found = jnp.where(best > 0.5 * NEG, pick, -1.0).astype(I32)
        idx = jnp.where(out_lane == r, found, idx)
        work = jnp.where(lane_f == pick, NEG, work)
    idx_ref[0, 0] = idx


def nsa_sample_cmp(q8, kcv, overlap, t, nsel):
    n_batch = q8.shape[0]
    nch = kcv.shape[2]
    blk = pl.BlockSpec((1, 1, 8, HEAD_DIM), lambda b, k: (b, k, 0, 0))
    return pl.pallas_call(
        functools.partial(_nsa_sample_cmp_kernel, t=t, nsel=nsel),
        out_shape=[jax.ShapeDtypeStruct((n_batch, C_KV_HEADS, 8, HEAD_DIM), F32),
                   jax.ShapeDtypeStruct((n_batch, C_KV_HEADS, 8, LANES), I32)],
        grid=(n_batch, C_KV_HEADS),
        in_specs=[
            blk,
            pl.BlockSpec((1, 1, nch, HEAD_DIM), lambda b, k: (b, k, 0, 0)),
            pl.BlockSpec((1, 1, nch, HEAD_DIM), lambda b, k: (b, C_KV_HEADS + k, 0, 0)),
            pl.BlockSpec(overlap.shape, lambda b, k: (0, 0)),
        ],
        out_specs=[blk, blk],
        compiler_params=_cparams(2),
        name="c_nsa_sample_cmp",
    )(q8, kcv, kcv, overlap)


def _nsa_sample_kernel(idx_ref, pt_ref, q_ref, cos_ref, sin_ref, slc_ref, new_ref, win_ref,
                       ocmp_ref, gate_ref, o_ref, m_scr, l_scr, acc_scr, *, t, n_past_blocks):
    del pt_ref
    b, k, n = pl.program_id(0), pl.program_id(1), pl.program_id(2)

    def head_rows(ref, kv):
        return ref[:, kv, pl.ds(k, 1), :][:, 0, :].astype(BF16)

    base = (b * C_KV_HEADS + k) * C_SEL_TOPN
    blk = idx_ref[base + n]
    scale = HEAD_DIM ** -0.5
    qr_f = _rope(q_ref[0, 0], cos_ref[...], sin_ref[...]) * scale
    qr = qr_f.astype(BF16)

    @pl.when(n == 0)
    def _():
        m_scr[...] = jnp.full(m_scr.shape, NEG, F32)
        l_scr[...] = jnp.zeros(l_scr.shape, F32)
        acc_scr[...] = jnp.zeros(acc_scr.shape, F32)

    def update(s, ok, pv):
        s = jnp.where(ok, s, NEG)
        m_old = m_scr[...]
        m_new = jnp.maximum(m_old, jnp.max(s, axis=1, keepdims=True))
        alpha = jnp.exp(m_old - m_new)
        p = jnp.where(ok, jnp.exp(s - m_new[:, 0:1]), 0.0)
        l_scr[...] = alpha * l_scr[...] + jnp.sum(p, axis=1, keepdims=True)
        acc_scr[...] = alpha * acc_scr[...] + pv(p)
        m_scr[...] = m_new

    kpos = blk * C_SEL_BLOCK + lax.broadcasted_iota(I32, (8, C_SEL_BLOCK), 1)
    ok = (blk >= 0) & (blk < n_past_blocks) & (kpos <= t)
    vs = head_rows(slc_ref, 1)
    update(_dot_nt(qr, head_rows(slc_ref, 0)), ok, lambda p: _dot(p.astype(BF16), vs))

    @pl.when(n == C_SEL_TOPN - 1)
    def _():
        has_new = blk == n_past_blocks
        for r in range(C_SEL_TOPN - 1):
            has_new = has_new | (idx_ref[base + r] == n_past_blocks)
        k_new, v_new = new_ref[0, 0, 0:1, :], new_ref[0, 0, 1:2, :]
        s_new = jnp.sum(qr_f * k_new, axis=1, keepdims=True)
        update(s_new, jnp.broadcast_to(has_new, (8, 1)), lambda p: p * v_new)
        o_slc = acc_scr[...] / jnp.maximum(l_scr[...], 1e-30)

        lw = win_ref.shape[0]
        kw_new, vw_new = new_ref[0, 0, 2:3, :], new_ref[0, 0, 3:4, :]
        dist = lw - lax.broadcasted_iota(I32, (8, lw), 1)
        w_ok = (dist >= 0) & (dist < C_WINDOW)
        s = jnp.where(w_ok, _dot_nt(qr, head_rows(win_ref, 0)), NEG)
        s_new = jnp.sum(qr_f * kw_new, axis=1, keepdims=True)
        m = jnp.maximum(jnp.max(s, axis=1, keepdims=True), s_new)
        p = jnp.where(w_ok, jnp.exp(s - m), 0.0)
        p_new = jnp.exp(s_new - m)
        den = jnp.sum(p, axis=1, keepdims=True) + p_new
        o_win = (_dot(p.astype(BF16), head_rows(win_ref, 1)) + p_new * vw_new) / den

        gate = gate_ref[0, 0]
        o_ref[0, 0] = gate[:, 0:1] * ocmp_ref[0, 0] + gate[:, 1:2] * o_slc + gate[:, 2:3] * o_win


def nsa_sample(top_idx, page_table, q8, cos, sin, pool_slc, win_buf, li, new_kv, ocmp, gate8, t, n_past_blocks):
    n_batch = q8.shape[0]
    n_pages = page_table.shape[1]
    n_layers, n_pool, page_rows = pool_slc.shape[:3]
    blocks_per_page = page_rows // C_SEL_BLOCK
    kv_dims = (2, C_KV_HEADS, HEAD_DIM)
    pool = pool_slc.reshape((n_layers, n_pool, blocks_per_page, C_SEL_BLOCK) + kv_dims)
    lw = win_buf.shape[2]

    def slc_index(b, k, n, idx, pt):
        blk = jnp.clip(idx[(b * C_KV_HEADS + k) * C_SEL_TOPN + n], 0, n_past_blocks - 1)
        return (li, pt[b * n_pages + blk // blocks_per_page], blk % blocks_per_page, 0, 0, 0, 0)

    slc_spec = pl.BlockSpec((None, None, None, C_SEL_BLOCK) + kv_dims, slc_index)
    win_spec = pl.BlockSpec((None, None, lw) + kv_dims, lambda b, k, n, idx, pt: (li, b, 0, 0, 0, 0))
    blk8 = pl.BlockSpec((1, 1, 8, HEAD_DIM), lambda b, k, n, idx, pt: (b, k, 0, 0))
    tab = pl.BlockSpec((8, HEAD_DIM), lambda b, k, n, idx, pt: (0, 0))
    return pl.pallas_call(
        functools.partial(_nsa_sample_kernel, t=t, n_past_blocks=n_past_blocks),
        out_shape=jax.ShapeDtypeStruct((n_batch, C_KV_HEADS, 8, HEAD_DIM), F32),
        grid_spec=pltpu.PrefetchScalarGridSpec(
            num_scalar_prefetch=2,
            grid=(n_batch, C_KV_HEADS, C_SEL_TOPN),
            in_specs=[blk8, tab, tab, slc_spec, blk8, win_spec, blk8, blk8],
            out_specs=blk8,
            scratch_shapes=[pltpu.VMEM((8, HEAD_DIM), F32)] * 3,
        ),
        compiler_params=_cparams(3),
        name="c_nsa_sample",
    )(top_idx.reshape(-1), page_table.reshape(-1), q8, cos, sin, pool, new_kv, win_buf, ocmp, gate8)


def _rope_tables(pos):
    half = HEAD_DIM // 2
    inv = ROPE_THETA ** (-(jnp.arange(half, dtype=F32) * 2.0 / HEAD_DIM))
    ang = pos.astype(F32)[:, None] * inv[None, :]
    cos, sin = jnp.cos(ang), jnp.sin(ang)
    return jnp.concatenate([cos, cos], axis=1), jnp.concatenate([-sin, sin], axis=1)


def _overlap_matrix(nch, nsel, width):
    cmp_start = jnp.arange(nch) * C_CMP_STRIDE
    sel_start = jnp.arange(width) * C_SEL_BLOCK
    ov = ((cmp_start[:, None] < sel_start[None, :] + C_SEL_BLOCK)
          & (cmp_start[:, None] + C_CMP_LEN > sel_start[None, :])
          & (jnp.arange(width)[None, :] < nsel))
    return ov.astype(BF16)


def _c_weights(prm, li, n_heads):
    d = n_heads * HEAD_DIM
    nkv = 3 * 2 * C_KV_HEADS * HEAD_DIM
    k_gain = prm['c_k_norm'][li]
    ones = jnp.ones((C_KV_HEADS * HEAD_DIM,), F32)
    head_gain = jnp.concatenate([
        jnp.tile(prm['c_q_norm'][li], n_heads), ones, ones,
        jnp.tile(k_gain[1], C_KV_HEADS), ones, jnp.tile(k_gain[2], C_KV_HEADS), ones])[None, :]
    wg = prm['w_c_in'][li][:, d + nkv:].reshape(-1, C_KV_HEADS, C_GROUP * 3)
    wg = jnp.pad(wg, ((0, 0), (0, 0), (0, LANES - C_GROUP * 3))).reshape(1, -1, C_KV_HEADS * LANES)
    w1 = prm['c_cmp_w1'][li]
    w_cat = jnp.concatenate([w1[:, :C_CMP_STRIDE], w1[:, C_CMP_STRIDE:]], axis=-1).astype(BF16)
    pe8 = jnp.pad(prm['c_cmp_pe'][li].reshape(2, 1, -1), ((0, 0), (0, 7), (0, 0)))
    w1_flat = w1.reshape(2, C_CMP_LEN * HEAD_DIM, -1)
    return head_gain, wg, w_cat, pe8, w1_flat, prm['c_cmp_w2'][li], k_gain[0][None, :]


TN = 512
TN_WIDE_K = 256
C_MODES = (NORM,) * 4 + (RAW, RAW, NORM_ROPE, RAW, NORM_ROPE, RAW)
A_MODES = ((NORM_ROPE,) * 8 + (RAW,) * 4) * len(A_PATTERNS)


def _pad_rows(a, rows):
    return jnp.pad(a, ((0, rows - a.shape[0]),) + ((0, 0),) * (a.ndim - 1))


def _a_head_gain(prm, li, n_heads):
    ones = jnp.ones((n_heads * HEAD_DIM,), F32)
    parts = []
    for g in range(len(A_PATTERNS)):
        parts += [jnp.tile(prm['a_q_norm'][li, g], n_heads), jnp.tile(prm['a_k_norm'][li, g], n_heads), ones]
    return jnp.concatenate(parts)[None, :]


def _channel_mixer(x, p, prm, i, tag):
    act = swiglu_in(x, prm['ffn_norm'], prm['w_ffn_in'], i, tn=TN, name=f"ffn_in_{tag}")
    x = linear_residual(act, prm['w_ffn_out'], i, x, tn=TN_WIDE_K, name=f"ffn_out_{tag}")
    return ple(x, prm['ple_norm'], p, prm['w_ple_proj'], prm['w_ple_gate'], i, tn=TN, name=f"ple_{tag}")


def _run_prompt(x3, p, prm):
    batch, seq, d = x3.shape
    n_heads = d // HEAD_DIM
    x = x3.reshape(batch * seq, d)
    cos, sin = _rope_tables(jnp.arange(seq))
    new = {}
    depth = prm['attn_norm'].shape[0]
    for i in range(depth):
        kind, li = i % 4, i // 4
        if kind == 0:
            z = norm_linear(x, prm['attn_norm'], i, prm['w_a_qkv'], li, col0=0, modes=A_MODES,
                            head_gain=_a_head_gain(prm, li, n_heads), cos=cos, sin=sin, tn=TN, name="a_qkv_prompt")
            x = linear_residual(a_attention_prompt(z, batch, seq), prm['w_a_out'], li, x, tn=TN, name="a_out_prompt")
            z3 = z.reshape(batch, seq, -1)
            for g, (win, dil) in enumerate(A_PATTERNS):
                kv = z3[:, seq - min(win, seq):, (3 * g + 1) * d:(3 * g + 3) * d]
                new.setdefault(f'a_w{g + 1}', []).append(kv.reshape(batch, -1, 2, n_heads, HEAD_DIM))
        elif kind == 1:
            bg, u = b_in(x, prm['attn_norm'], i, prm['w_b_in'], li, tn=TN_WIDE_K, name="b_in_prompt")
            x = b_out_prompt(u, bg, prm['b_conv'], prm['w_b_out'], li, x, seq, tn=TN, name="b_out_prompt")
            new.setdefault('b_conv', []).append(u.reshape(batch, seq, d)[:, seq - (CONV_W - 1):])
        elif kind == 2:
            head_gain, wg, w_cat, pe8, w1_flat, w2, k_gain0 = _c_weights(prm, li, n_heads)
            z = norm_linear(x, prm['attn_norm'], i, prm['w_c_in'], li, col0=0, modes=C_MODES,
                            head_gain=head_gain, cos=cos, sin=sin, tn=TN, name="c_in_prompt")
            gates = norm_linear(x, prm['attn_norm'], i, wg, 0, col0=0, modes=(SIGMOID,),
                                head_gain=head_gain[:, :TN], cos=cos, sin=sin, tn=TN, name="c_gate_prompt")
            n_pages = seq // LANES
            pool = z.reshape(batch * n_pages, LANES, z.shape[1])
            kv_width = 2 * C_KV_HEADS * HEAD_DIM
            fs = cmp_first_second(pool, jnp.arange(batch * n_pages, dtype=I32), d // kv_width, w_cat, batch, n_pages)
            kcv = cmp_mlp(fs, pe8, w1_flat, w2, k_gain0)
            nsel = -(-seq // C_SEL_BLOCK)
            overlap = _overlap_matrix(kcv.shape[2], nsel, LANES)
            expand = (jnp.arange(seq)[None, :] // C_SEL_BLOCK == jnp.arange(LANES)[:, None]).astype(BF16)
            o = nsa_prompt(z, gates, cos, sin, kcv, overlap, expand, batch, seq)
            x = linear_residual(o, prm['w_c_out'], li, x, tn=TN, name="c_out_prompt")
            z3 = z.reshape(batch, seq, -1)
            for br, nm in enumerate(('c_cmp', 'c_slc', 'c_win')):
                rows = min(C_WINDOW, seq) if nm == 'c_win' else seq
                kv = z3[:, seq - rows:, d + br * kv_width:d + (br + 1) * kv_width]
                new.setdefault(nm, []).append(kv.reshape(batch, rows, 2, C_KV_HEADS, HEAD_DIM))
        else:
            x, tail = d_mixer_prompt(x, prm['attn_norm'], i, prm['w_d_group'], prm['d_scale'], li, seq,
                                     name="d_mixer_prompt")
            new.setdefault('d_pool', []).append(tail[:, 1:])
        x = _channel_mixer(x, p, prm, i, "prompt")
    return x.reshape(batch, seq, d), {nm: jnp.stack(v, axis=0) for nm, v in new.items()}


def _run_sample(x3, p, prm, past, page_table):
    n_batch, n_new, d = x3.shape
    assert n_new == 1
    n_heads = d // HEAD_DIM
    rows = SUBLANES_BF16
    x = _pad_rows(x3.reshape(n_batch, d), rows)
    past_len = page_table.shape[1] * past['c_cmp'].shape[2]
    cos, sin = _rope_tables(jnp.full((rows,), past_len))
    new = {}
    depth = prm['attn_norm'].shape[0]
    for i in range(depth):
        kind, li = i % 4, i // 4
        if kind == 0:
            z = norm_linear(x, prm['attn_norm'], i, prm['w_a_qkv'], li, col0=0, modes=A_MODES,
                            head_gain=_a_head_gain(prm, li, n_heads), cos=cos, sin=sin, tn=TN, name="a_qkv_sample")
            caches = [past[f'a_w{g + 1}'] for g in range(len(A_PATTERNS))]
            o = a_attention_sample(z, caches, li, n_batch)
            x = linear_residual(_pad_rows(o, rows), prm['w_a_out'], li, x, tn=TN, name="a_out_sample")
            for g, cache in enumerate(caches):
                kv = z[:n_batch, (3 * g + 1) * d:(3 * g + 3) * d].reshape(n_batch, 1, 2, n_heads, HEAD_DIM)
                new.setdefault(f'a_w{g + 1}', []).append(jnp.concatenate([cache[li, :, 1:], kv], axis=1))
        elif kind == 1:
            bg, u = b_in(x, prm['attn_norm'], i, prm['w_b_in'], li, tn=TN_WIDE_K, name="b_in_sample")
            hist = past['b_conv'][li]
            x = b_out_sample(u, _pad_rows(hist[:, 1], rows), _pad_rows(hist[:, 0], rows), bg, prm['b_conv'],
                             prm['w_b_out'], li, x, tn=TN, name="b_out_sample")
            new.setdefault('b_conv', []).append(jnp.concatenate([hist[:, 1:], u[:n_batch, None]], axis=1))
        elif kind == 2:
            head_gain, wg, w_cat, pe8, w1_flat, w2, k_gain0 = _c_weights(prm, li, n_heads)
            z = norm_linear(x, prm['attn_norm'], i, prm['w_c_in'], li, col0=0, modes=C_MODES,
                            head_gain=head_gain, cos=cos, sin=sin, tn=TN, name="c_in_sample")
            gates = norm_linear(x, prm['attn_norm'], i, wg, 0, col0=0, modes=(SIGMOID,),
                                head_gain=head_gain[:, :TN], cos=cos, sin=sin, tn=TN, name="c_gate_sample")
            kv_width = 2 * C_KV_HEADS * HEAD_DIM
            kcv = cmp_mlp(cmp_first_second_paged(past['c_cmp'], li, page_table, w_cat), pe8, w1_flat, w2, k_gain0)
            nsel = -(-(past_len + 1) // C_SEL_BLOCK)
            width = -(-nsel // LANES) * LANES
            overlap = _overlap_matrix(kcv.shape[2], nsel, width)
            zb = z[:n_batch]
            q8 = jnp.pad(zb[:, :d].reshape(n_batch, C_KV_HEADS, C_GROUP, HEAD_DIM),
                         ((0, 0), (0, 0), (0, 8 - C_GROUP), (0, 0)))
            ocmp, idx = nsa_sample_cmp(q8, kcv, overlap, past_len, nsel)
            top_idx = idx[:, :, 0, :C_SEL_TOPN]
            kvh_w = C_KV_HEADS * HEAD_DIM
            new_rows = [zb[:, d + kv_width + j * kvh_w:d + kv_width + (j + 1) * kvh_w]
                        .reshape(n_batch, C_KV_HEADS, 1, HEAD_DIM) for j in range(4)]
            new_kv = jnp.pad(jnp.concatenate(new_rows, axis=2), ((0, 0), (0, 0), (0, 4), (0, 0)))
            gate8 = gates[:n_batch].reshape(n_batch, C_KV_HEADS, LANES)[:, :, :C_GROUP * 3]
            gate8 = jnp.pad(gate8.reshape(n_batch, C_KV_HEADS, C_GROUP, 3),
                            ((0, 0), (0, 0), (0, 8 - C_GROUP), (0, HEAD_DIM - 3)))
            o = nsa_sample(top_idx, page_table, q8, cos[:8], sin[:8], past['c_slc'], past['c_win'], li, new_kv,
                           ocmp, gate8, past_len, past_len // C_SEL_BLOCK)
            o = o[:, :, :C_GROUP].reshape(n_batch, d)
            x = linear_residual(_pad_rows(o, rows), prm['w_c_out'], li, x, tn=TN, name="c_out_sample")
            for br, nm in enumerate(('c_cmp', 'c_slc', 'c_win')):
                kv = zb[:, d + br * kv_width:d + (br + 1) * kv_width].reshape(n_batch, 1, 2, C_KV_HEADS, HEAD_DIM)
                if nm == 'c_win':
                    kv = jnp.concatenate([past['c_win'][li][:, 1:], kv], axis=1)
                new.setdefault(nm, []).append(kv)
        else:
            hist = past['d_pool'][li]
            hist_t = jnp.pad(jnp.swapaxes(hist, 0, 1), ((0, 0), (0, rows - n_batch), (0, 0)))
            x, h = d_mixer_sample(x, hist_t, prm['attn_norm'], i, prm['w_d_group'], prm['d_scale'], li,
                                  name="d_mixer_sample")
            new.setdefault('d_pool', []).append(jnp.concatenate([hist[:, 1:], h[:n_batch, None]], axis=1))
        x = _channel_mixer(x, p, prm, i, "sample")
    return x[:n_batch].reshape(n_batch, 1, d), {nm: jnp.stack(v, axis=0) for nm, v in new.items()}


def kernel(x_prompt, x_sample, cache_a_w1, cache_a_w2, cache_a_w3, state_b_conv, cache_c_cmp, cache_c_slc, cache_c_win, state_d_pool, page_table, p_prompt, p_sample, attn_norm, ffn_norm, ple_norm, w_a_qkv, a_q_norm, a_k_norm, w_a_out, w_b_in, b_conv, w_b_out, w_c_in, c_q_norm, c_k_norm, c_cmp_pe, c_cmp_w1, c_cmp_w2, w_c_out, w_d_group, d_scale, w_ffn_in, w_ffn_out, w_ple_proj, w_ple_gate):
    prm = dict(attn_norm=attn_norm, ffn_norm=ffn_norm, ple_norm=ple_norm, w_a_qkv=w_a_qkv, a_q_norm=a_q_norm,
               a_k_norm=a_k_norm, w_a_out=w_a_out, w_b_in=w_b_in, b_conv=b_conv, w_b_out=w_b_out, w_c_in=w_c_in,
               c_q_norm=c_q_norm, c_k_norm=c_k_norm, c_cmp_pe=c_cmp_pe, c_cmp_w1=c_cmp_w1, c_cmp_w2=c_cmp_w2,
               w_c_out=w_c_out, w_d_group=w_d_group, d_scale=d_scale, w_ffn_in=w_ffn_in, w_ffn_out=w_ffn_out,
               w_ple_proj=w_ple_proj, w_ple_gate=w_ple_gate)
    past = dict(a_w1=cache_a_w1, a_w2=cache_a_w2, a_w3=cache_a_w3, b_conv=state_b_conv, c_cmp=cache_c_cmp,
                c_slc=cache_c_slc, c_win=cache_c_win, d_pool=state_d_pool)
    depth = attn_norm.shape[0]
    batch, seq, _ = x_prompt.shape
    n_dec = x_sample.shape[0]
    p_p = p_prompt.reshape(depth, batch * seq, -1)
    p_s = jnp.pad(p_sample.reshape(depth, n_dec, -1), ((0, 0), (0, SUBLANES_BF16 - n_dec), (0, 0)))
    y_prompt, sp = _run_prompt(x_prompt, p_p, prm)
    y_sample, ss = _run_sample(x_sample, p_s, prm, past, page_table)
    return (y_prompt, y_sample,
            sp['a_w1'], ss['a_w1'], sp['a_w2'], ss['a_w2'], sp['a_w3'], ss['a_w3'],
            sp['b_conv'], ss['b_conv'],
            sp['c_cmp'], ss['c_cmp'], sp['c_slc'], ss['c_slc'], sp['c_win'], ss['c_win'],
            sp['d_pool'], ss['d_pool'])
```

```python
import functools

import jax
import jax.numpy as jnp
from jax import lax
from jax.experimental import pallas as pl
from jax.experimental.pallas import tpu as pltpu

F32 = jnp.float32
BF16 = jnp.bfloat16
I32 = jnp.int32

HEAD_DIM = 128
LANES = 128
SUBLANES_BF16 = 16
RMS_EPS = 1e-6
ROPE_THETA = 10000.0
NEG = -1e30
V7X_VMEM_BYTES = 64 * 1024 * 1024
VMEM_LIMIT = V7X_VMEM_BYTES - 8 * 1024 * 1024

A_PATTERNS = ((128, 1), (512, 4), (2048, 16))
A_BLOCK = 128
A_UNROLL = 4
CONV_W = 3
C_KV_HEADS = 4
C_GROUP = 4
C_CMP_LEN = 32
C_CMP_STRIDE = 16
C_SEL_BLOCK = 64
C_SEL_TOPN = 16
C_WINDOW = 512
C_FORCE_SCORE = 1e4
C_QBLOCK = 128
C_KEY_TILE = 512
LOG2_E = 1.4426950408889634
D_POOLS = (2, 4, 8, 16)
D_HIST = max(D_POOLS) - 1
PAGES_PER_GROUP = 16

RAW, NORM, NORM_ROPE, SIGMOID = range(4)


def _cparams(n_axes):
    return pltpu.CompilerParams(dimension_semantics=("arbitrary",) * n_axes,
                                vmem_limit_bytes=VMEM_LIMIT)


def _row_tile(m):
    return 1024 if m % 1024 == 0 else m


def _dot(a, b):
    return jnp.dot(a, b, preferred_element_type=F32)


def _dot_nt(a, b):
    return lax.dot_general(a, b, (((1,), (1,)), ((), ())), preferred_element_type=F32)


def _with_ones(v):
    return jnp.concatenate([v.astype(BF16), jnp.ones(v.shape, BF16)], axis=1)


def _rms(x, g):
    return x * lax.rsqrt(jnp.mean(x * x, axis=-1, keepdims=True) + RMS_EPS) * g


def _rope(y, cos, sin):
    return y * cos + pltpu.roll(y, HEAD_DIM // 2, 1) * sin


def _tile_pred(j, tiles):
    pred = None
    lo = prev = tiles[0]
    runs = []
    for t in tiles[1:]:
        if t != prev + 1:
            runs.append((lo, prev))
            lo = t
        prev = t
    runs.append((lo, prev))
    for lo, hi in runs:
        p = (j == lo) if lo == hi else ((j >= lo) & (j <= hi))
        pred = p if pred is None else (pred | p)
    return pred


def _norm_linear_kernel(x_ref, g_ref, w_ref, hg_ref, cos_ref, sin_ref, o_ref, h_scr, *, modes):
    j = pl.program_id(1)

    @pl.when(j == 0)
    def _():
        h_scr[...] = _rms(x_ref[...], g_ref[...]).astype(BF16)

    acc = _dot(h_scr[...], w_ref[...].astype(BF16))
    tn = acc.shape[1]

    def emit(mode):
        if mode == RAW:
            o_ref[...] = acc.astype(o_ref.dtype)
        elif mode == SIGMOID:
            o_ref[...] = jax.nn.sigmoid(acc).astype(o_ref.dtype)
        else:
            for c in range(tn // HEAD_DIM):
                sl = slice(c * HEAD_DIM, (c + 1) * HEAD_DIM)
                y = _rms(acc[:, sl], hg_ref[:, sl])
                if mode == NORM_ROPE:
                    y = _rope(y, cos_ref[...], sin_ref[...])
                o_ref[:, sl] = y.astype(o_ref.dtype)

    kinds = sorted(set(modes))
    if len(kinds) == 1:
        emit(kinds[0])
    else:
        for m in kinds:
            tiles = [t for t, mm in enumerate(modes) if mm == m]
            pl.when(_tile_pred(j, tiles))(functools.partial(emit, m))


def norm_linear(x, gain, gi, w, wi, *, col0, modes, head_gain, cos, sin, tn, name):
    m, k = x.shape
    tm = _row_tile(m)
    nt = len(modes)
    assert col0 % tn == 0 and cos.shape[0] % tm == 0
    coff = col0 // tn
    n_pos_tiles = cos.shape[0] // tm
    return pl.pallas_call(
        functools.partial(_norm_linear_kernel, modes=tuple(modes)),
        out_shape=jax.ShapeDtypeStruct((m, nt * tn), F32),
        grid=(m // tm, nt),
        in_specs=[
            pl.BlockSpec((tm, k), lambda i, j: (i, 0)),
            pl.BlockSpec((None, 1, k), lambda i, j: (gi, 0, 0)),
            pl.BlockSpec((None, k, tn), lambda i, j: (wi, 0, j + coff)),
            pl.BlockSpec((1, tn), lambda i, j: (0, j)),
            pl.BlockSpec((tm, HEAD_DIM), lambda i, j: (i % n_pos_tiles, 0)),
            pl.BlockSpec((tm, HEAD_DIM), lambda i, j: (i % n_pos_tiles, 0)),
        ],
        out_specs=pl.BlockSpec((tm, tn), lambda i, j: (i, j)),
        scratch_shapes=[pltpu.VMEM((tm, k), BF16)],
        compiler_params=_cparams(2),
        name=name,
    )(x, gain.reshape(gain.shape[0], 1, k), w, head_gain, cos, sin)


def _linear_res_kernel(a_ref, w_ref, r_ref, o_ref):
    o_ref[...] = r_ref[...] + _dot(a_ref[...].astype(BF16), w_ref[...].astype(BF16))


def linear_residual(a, w, li, res, *, tn, name):
    m, k = a.shape
    n = w.shape[-1]
    tm = _row_tile(m)
    return pl.pallas_call(
        _linear_res_kernel,
        out_shape=jax.ShapeDtypeStruct((m, n), F32),
        grid=(m // tm, n // tn),
        in_specs=[
            pl.BlockSpec((tm, k), lambda i, j: (i, 0)),
            pl.BlockSpec((None, k, tn), lambda i, j: (li, 0, j)),
            pl.BlockSpec((tm, tn), lambda i, j: (i, j)),
        ],
        out_specs=pl.BlockSpec((tm, tn), lambda i, j: (i, j)),
        compiler_params=_cparams(2),
        name=name,
    )(a, w, res)


def _swiglu_in_kernel(x_ref, g_ref, wg_ref, wu_ref, o_ref, h_scr):
    @pl.when(pl.program_id(1) == 0)
    def _():
        h_scr[...] = _rms(x_ref[...], g_ref[...]).astype(BF16)

    h = h_scr[...]
    a = _dot(h, wg_ref[...].astype(BF16))
    b = _dot(h, wu_ref[...].astype(BF16))
    o_ref[...] = (a * jax.nn.sigmoid(a) * b).astype(o_ref.dtype)


def swiglu_in(x, gain, w, li, *, tn, name):
    m, k = x.shape
    f = w.shape[-1] // 2
    tm = _row_tile(m)
    nt = f // tn
    return pl.pallas_call(
        _swiglu_in_kernel,
        out_shape=jax.ShapeDtypeStruct((m, f), BF16),
        grid=(m // tm, nt),
        in_specs=[
            pl.BlockSpec((tm, k), lambda i, j: (i, 0)),
            pl.BlockSpec((None, 1, k), lambda i, j: (li, 0, 0)),
            pl.BlockSpec((None, k, tn), lambda i, j: (li, 0, j)),
            pl.BlockSpec((None, k, tn), lambda i, j: (li, 0, j + nt)),
        ],
        out_specs=pl.BlockSpec((tm, tn), lambda i, j: (i, j)),
        scratch_shapes=[pltpu.VMEM((tm, k), BF16)],
        compiler_params=_cparams(2),
        name=name,
    )(x, gain.reshape(gain.shape[0], 1, k), w, w)


def _ple_kernel(x_ref, g_ref, p_ref, wp_ref, wg_ref, r_ref, o_ref, h_scr):
    @pl.when(pl.program_id(1) == 0)
    def _():
        h_scr[...] = _rms(x_ref[...], g_ref[...]).astype(BF16)

    gate = jax.nn.sigmoid(_dot(h_scr[...], wg_ref[...].astype(BF16)))
    proj = _dot(p_ref[...].astype(BF16), wp_ref[...].astype(BF16))
    o_ref[...] = r_ref[...] + proj * gate


def ple(x, gain, p, wp, wg, li, *, tn, name):
    m, k = x.shape
    pd = p.shape[-1]
    tm = _row_tile(m)
    assert p.shape[1] == m
    return pl.pallas_call(
        _ple_kernel,
        out_shape=jax.ShapeDtypeStruct((m, k), F32),
        grid=(m // tm, k // tn),
        in_specs=[
            pl.BlockSpec((tm, k), lambda i, j: (i, 0)),
            pl.BlockSpec((None, 1, k), lambda i, j: (li, 0, 0)),
            pl.BlockSpec((None, tm, pd), lambda i, j: (li, i, 0)),
            pl.BlockSpec((None, pd, tn), lambda i, j: (li, 0, j)),
            pl.BlockSpec((None, k, tn), lambda i, j: (li, 0, j)),
            pl.BlockSpec((tm, tn), lambda i, j: (i, j)),
        ],
        out_specs=pl.BlockSpec((tm, tn), lambda i, j: (i, j)),
        scratch_shapes=[pltpu.VMEM((tm, k), BF16)],
        compiler_params=_cparams(2),
        name=name,
    )(x, gain.reshape(gain.shape[0], 1, k), p, wp, wg, x)


def _a_prompt_kernel(*refs, seq):
    qkv_refs, o_ref = refs[:9], refs[9]
    acc_scr, m_scr, l_scr = refs[10:]
    scale = HEAD_DIM ** -0.5
    row = lax.broadcasted_iota(I32, (A_BLOCK, A_BLOCK), 0)
    col = lax.broadcasted_iota(I32, (A_BLOCK, A_BLOCK), 1)
    cur_ok = col <= row
    prev_ok = col >= row
    wide = (A_BLOCK, HEAD_DIM)
    first = len(A_PATTERNS) - 1
    for g, (win, dil) in reversed(list(enumerate(A_PATTERNS))):
        q_ref, k_ref, v_ref = qkv_refs[3 * g:3 * g + 3]
        nb = seq // dil // A_BLOCK

        def rows_of(r, n, dil=dil):
            start = r + dil * A_BLOCK * n
            if dil == 1:
                return pl.ds(pl.multiple_of(start, A_BLOCK), A_BLOCK)
            return pl.ds(start, A_BLOCK, stride=dil)

        def block(idx, carry, g=g, nb=nb, q_ref=q_ref, k_ref=k_ref, v_ref=v_ref, rows_of=rows_of):
            r, n = idx // nb, idx % nb
            rows = rows_of(r, n)
            q = (q_ref[0, rows, :] * scale).astype(BF16)
            s_c = jnp.where(cur_ok, _dot_nt(q, k_ref[0, rows, :].astype(BF16)), NEG)
            if nb > 1:
                prows = rows_of(r, jnp.maximum(n - 1, 0))
                s_p = jnp.where(prev_ok & (n > 0), _dot_nt(q, k_ref[0, prows, :].astype(BF16)), NEG)
                m = jnp.max(jnp.maximum(s_c, s_p), axis=1, keepdims=True)
            else:
                m = jnp.max(s_c, axis=1, keepdims=True)
            ext = _dot(jnp.exp(s_c - m).astype(BF16), _with_ones(v_ref[0, rows, :]))
            if nb > 1:
                ext = ext + _dot(jnp.exp(s_p - m).astype(BF16), _with_ones(v_ref[0, prows, :]))
            acc, den = ext[:, :HEAD_DIM], ext[:, HEAD_DIM:]
            if g == first:
                acc_scr[rows, :] = acc
                m_scr[rows, :] = jnp.broadcast_to(m, wide)
                l_scr[rows, :] = den
            else:
                m_old = m_scr[rows, :]
                m_new = jnp.maximum(m_old, m)
                a_old, a_blk = jnp.exp(m_old - m_new), jnp.exp(m - m_new)
                acc_scr[rows, :] = a_old * acc_scr[rows, :] + a_blk * acc
                l_scr[rows, :] = a_old * l_scr[rows, :] + a_blk * den
                m_scr[rows, :] = m_new
            return carry

        lax.fori_loop(0, dil * nb, block, 0, unroll=A_UNROLL)
    o_ref[0] = (acc_scr[...] / l_scr[...]).astype(o_ref.dtype)


def a_attention_prompt(z, batch, seq):
    nc = z.shape[1]
    d = nc // (3 * len(A_PATTERNS))
    n_heads = d // HEAD_DIM
    assert all(win // dil == A_BLOCK and seq % (dil * A_BLOCK) == 0 for win, dil in A_PATTERNS)
    zv = z.reshape(batch, seq, nc)
    in_specs = [pl.BlockSpec((1, seq, HEAD_DIM), functools.partial(lambda b, h, c: (b, 0, c * n_heads + h), c=c))
                for c in range(3 * len(A_PATTERNS))]
    out = pl.pallas_call(
        functools.partial(_a_prompt_kernel, seq=seq),
        out_shape=jax.ShapeDtypeStruct((batch, seq, d), BF16),
        grid=(batch, n_heads),
        in_specs=in_specs,
        out_specs=pl.BlockSpec((1, seq, HEAD_DIM), lambda b, h: (b, 0, h)),
        scratch_shapes=[pltpu.VMEM((seq, HEAD_DIM), F32)] * 3,
        compiler_params=_cparams(2),
        name="a_attn_prompt",
    )(*([zv] * (3 * len(A_PATTERNS))))
    return out.reshape(batch * seq, d)


def _a_sample_kernel(z_ref, *refs, n_heads, d):
    cache_refs, o_ref = refs[:-1], refs[-1]
    scale = HEAD_DIM ** -0.5
    for h in range(n_heads):
        ms, dens, accs = [], [], []
        for g in range(len(A_PATTERNS)):
            kv_ref = cache_refs[g]
            c0 = g * 3 * d + h * HEAD_DIM
            q = z_ref[0, :, c0:c0 + HEAD_DIM] * scale
            k_new = z_ref[0, :, c0 + d:c0 + d + HEAD_DIM]
            v_new = z_ref[0, :, c0 + 2 * d:c0 + 2 * d + HEAD_DIM]
            q8 = jnp.broadcast_to(q, (8, HEAD_DIM))
            s = _dot_nt(q8.astype(BF16), kv_ref[:, 0, h, :].astype(BF16))
            s_new = jnp.sum(q8 * k_new, axis=1, keepdims=True)
            m = jnp.maximum(jnp.max(s, axis=1, keepdims=True), s_new)
            p = jnp.exp(s - m)
            p_new = jnp.exp(s_new - m)
            dens.append(jnp.sum(p, axis=1, keepdims=True) + p_new)
            accs.append(_dot(p.astype(BF16), kv_ref[:, 1, h, :].astype(BF16)) + p_new * v_new)
            ms.append(m)
        mt = jnp.maximum(jnp.maximum(ms[0], ms[1]), ms[2])
        es = [jnp.exp(mm - mt) for mm in ms]
        tot = es[0] * dens[0] + es[1] * dens[1] + es[2] * dens[2]
        num = es[0] * accs[0] + es[1] * accs[1] + es[2] * accs[2]
        o_ref[0, :, h * HEAD_DIM:(h + 1) * HEAD_DIM] = num / tot


def a_attention_sample(z, caches, li, n_batch):
    d = z.shape[1] // 9
    n_heads = d // HEAD_DIM
    in_specs = [pl.BlockSpec((1, 1, z.shape[1]), lambda b: (b, 0, 0))]
    args = [z[:, None, :]]
    for (win, dil), cache in zip(A_PATTERNS, caches):
        lb = cache.shape[2]
        assert lb == win and (win // dil) == A_BLOCK
        in_specs.append(pl.BlockSpec((None, None, A_BLOCK, None, 2, n_heads, HEAD_DIM),
                                     lambda b: (li, b, 0, 0, 0, 0, 0)))
        args.append(cache.reshape(cache.shape[0], n_batch, lb // dil, dil, 2, n_heads, HEAD_DIM))
    out = pl.pallas_call(
        functools.partial(_a_sample_kernel, n_heads=d // HEAD_DIM, d=d),
        out_shape=jax.ShapeDtypeStruct((n_batch, 8, d), F32),
        grid=(n_batch,),
        in_specs=in_specs,
        out_specs=pl.BlockSpec((1, 8, d), lambda b: (b, 0, 0)),
        compiler_params=_cparams(1),
        name="a_attn_sample",
    )(*args)
    return out[:, 0]


def _b_in_kernel(x_ref, g_ref, wb_ref, wc_ref, wx_ref, bg_ref, u_ref, h_scr):
    @pl.when(pl.program_id(1) == 0)
    def _():
        h_scr[...] = _rms(x_ref[...], g_ref[...]).astype(BF16)

    h = h_scr[...]
    bg_ref[...] = _dot(h, wb_ref[...].astype(BF16))
    u_ref[...] = _dot(h, wc_ref[...].astype(BF16)) * _dot(h, wx_ref[...].astype(BF16))


def b_in(x, gain, gi, w, li, *, tn, name):
    m, k = x.shape
    d = w.shape[-1] // 3
    tm = _row_tile(m)
    nt = d // tn
    out_spec = pl.BlockSpec((tm, tn), lambda i, j: (i, j))
    return pl.pallas_call(
        _b_in_kernel,
        out_shape=[jax.ShapeDtypeStruct((m, d), F32)] * 2,
        grid=(m // tm, nt),
        in_specs=[
            pl.BlockSpec((tm, k), lambda i, j: (i, 0)),
            pl.BlockSpec((None, 1, k), lambda i, j: (gi, 0, 0)),
            pl.BlockSpec((None, k, tn), lambda i, j: (li, 0, j)),
            pl.BlockSpec((None, k, tn), lambda i, j: (li, 0, j + nt)),
            pl.BlockSpec((None, k, tn), lambda i, j: (li, 0, j + 2 * nt)),
        ],
        out_specs=[out_spec, out_spec],
        scratch_shapes=[pltpu.VMEM((tm, k), BF16)],
        compiler_params=_cparams(2),
        name=name,
    )(x, gain.reshape(gain.shape[0], 1, k), w, w, w)


def _b_out_prompt_kernel(u_ref, up_ref, bg_ref, cw_ref, w_ref, r_ref, o_ref, ext_scr, a_scr, *, tiles_per_seq):
    i = pl.program_id(0)
    tm = u_ref.shape[0]

    @pl.when(pl.program_id(1) == 0)
    def _():
        ext_scr[0:8, :] = jnp.where(i % tiles_per_seq == 0, 0.0, up_ref[...])
        ext_scr[8:, :] = u_ref[...]
        y = (cw_ref[0:1, :] * ext_scr[pl.ds(6, tm), :] + cw_ref[1:2, :] * ext_scr[pl.ds(7, tm), :]
             + cw_ref[2:3, :] * ext_scr[pl.ds(8, tm), :])
        a_scr[...] = (bg_ref[...] * y).astype(BF16)

    o_ref[...] = r_ref[...] + _dot(a_scr[...], w_ref[...].astype(BF16))


def b_out_prompt(u, bg, conv_w, w, li, res, seq, *, tn, name):
    m, d = u.shape
    tm = 512
    assert seq % tm == 0
    return pl.pallas_call(
        functools.partial(_b_out_prompt_kernel, tiles_per_seq=seq // tm),
        out_shape=jax.ShapeDtypeStruct((m, d), F32),
        grid=(m // tm, d // tn),
        in_specs=[
            pl.BlockSpec((tm, d), lambda i, j: (i, 0)),
            pl.BlockSpec((8, d), lambda i, j: (jnp.maximum(i * (tm // 8) - 1, 0), 0)),
            pl.BlockSpec((tm, d), lambda i, j: (i, 0)),
            pl.BlockSpec((None, CONV_W, d), lambda i, j: (li, 0, 0)),
            pl.BlockSpec((None, d, tn), lambda i, j: (li, 0, j)),
            pl.BlockSpec((tm, tn), lambda i, j: (i, j)),
        ],
        out_specs=pl.BlockSpec((tm, tn), lambda i, j: (i, j)),
        scratch_shapes=[pltpu.VMEM((tm + 8, d), F32), pltpu.VMEM((tm, d), BF16)],
        compiler_params=_cparams(2),
        name=name,
    )(u, u, bg, conv_w, w, res)


def _b_out_sample_kernel(u_ref, um1_ref, um2_ref, bg_ref, cw_ref, w_ref, r_ref, o_ref):
    y = cw_ref[0:1, :] * um2_ref[...] + cw_ref[1:2, :] * um1_ref[...] + cw_ref[2:3, :] * u_ref[...]
    o_ref[...] = r_ref[...] + _dot((bg_ref[...] * y).astype(BF16), w_ref[...].astype(BF16))


def b_out_sample(u, um1, um2, bg, conv_w, w, li, res, *, tn, name):
    m, d = u.shape
    full = pl.BlockSpec((m, d), lambda j: (0, 0))
    return pl.pallas_call(
        _b_out_sample_kernel,
        out_shape=jax.ShapeDtypeStruct((m, d), F32),
        grid=(d // tn,),
        in_specs=[full, full, full, full,
                  pl.BlockSpec((None, CONV_W, d), lambda j: (li, 0, 0)),
                  pl.BlockSpec((None, d, tn), lambda j: (li, 0, j)),
                  pl.BlockSpec((m, tn), lambda j: (0, j))],
        out_specs=pl.BlockSpec((m, tn), lambda j: (0, j)),
        compiler_params=_cparams(1),
        name=name,
    )(u, um1, um2, bg, conv_w, w, res)


def _d_prompt_kernel(x_ref, xp_ref, g_ref, w_ref, sc_ref, r_ref, o_ref, ht_ref, ext_scr, *, tiles_per_seq):
    i = pl.program_id(0)
    j = pl.program_id(1)
    tm = x_ref.shape[0]
    halo = D_HIST + 1
    gw = w_ref.shape[0]

    @pl.when(j == 0)
    def _():
        ext_scr[0:halo, :] = jnp.where(i % tiles_per_seq == 0, 0.0, _rms(xp_ref[...], g_ref[...]))
        ext_scr[halo:, :] = _rms(x_ref[...], g_ref[...])
        ht_ref[0] = ext_scr[pl.ds(tm, halo), :]

    pos = (i % tiles_per_seq) * tm + lax.broadcasted_iota(I32, (tm, 1), 0)
    for g, win in enumerate(D_POOLS):
        @pl.when(j == g)
        def _(g=g, win=win):
            cols = slice(g * gw, (g + 1) * gw)
            tot = ext_scr[pl.ds(halo, tm), cols]
            h = tot
            for back in range(1, win):
                tot = tot + ext_scr[pl.ds(halo - back, tm), cols]
            count = jnp.minimum(pos + 1, win).astype(F32)
            pooled = tot / count - h
            o_ref[...] = r_ref[...] + _dot(pooled.astype(BF16), w_ref[...].astype(BF16)) * sc_ref[...]


def d_mixer_prompt(x, gain, li_norm, w_group, scale, li, seq, *, name):
    m, d = x.shape
    n_groups, gw = w_group.shape[1], w_group.shape[2]
    tm = 512
    halo = D_HIST + 1
    assert seq % tm == 0 and n_groups == len(D_POOLS)
    tps = seq // tm
    return pl.pallas_call(
        functools.partial(_d_prompt_kernel, tiles_per_seq=tps),
        out_shape=[jax.ShapeDtypeStruct((m, d), F32), jax.ShapeDtypeStruct((m // seq, halo, d), F32)],
        grid=(m // tm, n_groups),
        in_specs=[
            pl.BlockSpec((tm, d), lambda i, j: (i, 0)),
            pl.BlockSpec((halo, d), lambda i, j: (jnp.maximum(i * (tm // halo) - 1, 0), 0)),
            pl.BlockSpec((None, 1, d), lambda i, j: (li_norm, 0, 0)),
            pl.BlockSpec((None, None, gw, gw), lambda i, j: (li, j, 0, 0)),
            pl.BlockSpec((None, 1, gw), lambda i, j: (li, 0, j)),
            pl.BlockSpec((tm, gw), lambda i, j: (i, j)),
        ],
        out_specs=[pl.BlockSpec((tm, gw), lambda i, j: (i, j)),
                   pl.BlockSpec((1, halo, d), lambda i, j: (i // tps, 0, 0))],
        scratch_shapes=[pltpu.VMEM((tm + halo, d), F32)],
        compiler_params=_cparams(2),
        name=name,
    )(x, x, gain.reshape(gain.shape[0], 1, d), w_group, scale.reshape(scale.shape[0], 1, d), x)


def _d_sample_kernel(x_ref, hist_ref, g_ref, w_ref, sc_ref, o_ref, h_ref):
    h = _rms(x_ref[...], g_ref[...])
    h_ref[...] = h
    gw = w_ref.shape[1]
    for g, win in enumerate(D_POOLS):
        cols = slice(g * gw, (g + 1) * gw)
        tot = h[:, cols]
        for back in range(1, win):
            tot = tot + hist_ref[D_HIST - back, :, cols]
        pooled = tot / float(win) - h[:, cols]
        o_ref[:, cols] = x_ref[:, cols] + _dot(pooled.astype(BF16), w_ref[g].astype(BF16)) * sc_ref[:, cols]


def d_mixer_sample(x, hist_t, gain, li_norm, w_group, scale, li, *, name):
    m, d = x.shape
    n_groups, gw = w_group.shape[1], w_group.shape[2]
    assert hist_t.shape[0] == D_HIST
    return pl.pallas_call(
        _d_sample_kernel,
        out_shape=[jax.ShapeDtypeStruct((m, d), F32)] * 2,
        grid=(1,),
        in_specs=[
            pl.BlockSpec((m, d), lambda i: (0, 0)),
            pl.BlockSpec(hist_t.shape, lambda i: (0, 0, 0)),
            pl.BlockSpec((None, 1, d), lambda i: (li_norm, 0, 0)),
            pl.BlockSpec((None, n_groups, gw, gw), lambda i: (li, 0, 0, 0)),
            pl.BlockSpec((None, 1, d), lambda i: (li, 0, 0)),
        ],
        out_specs=[pl.BlockSpec((m, d), lambda i: (0, 0))] * 2,
        compiler_params=_cparams(1),
        name=name,
    )(x, hist_t, gain.reshape(gain.shape[0], 1, d), w_group, scale.reshape(scale.shape[0], 1, d))


def _cmp_fs_kernel(pt_ref, page_ref, w_ref, o_ref, ring_scr):
    del pt_ref
    slot = pl.program_id(1) % PAGES_PER_GROUP
    rows = page_ref.shape[1]
    for ck in range(2 * C_KV_HEADS):
        ring_scr[ck, pl.ds(pl.multiple_of(slot * rows, rows), rows), :] = page_ref[0, :, ck * HEAD_DIM:(ck + 1) * HEAD_DIM]

    @pl.when(slot == PAGES_PER_GROUP - 1)
    def _():
        n_chunks = PAGES_PER_GROUP * rows // C_CMP_STRIDE
        for ck in range(2 * C_KV_HEADS):
            c = ck // C_KV_HEADS
            acc = jnp.zeros((n_chunks, w_ref.shape[-1]), F32)
            for pp in range(C_CMP_STRIDE // 2):
                lhs = jnp.concatenate([ring_scr[ck, pl.ds(2 * pp + i, n_chunks, stride=C_CMP_STRIDE), :]
                                       for i in range(2)], axis=1)
                acc = acc + _dot(lhs.astype(BF16), w_ref[c, pp])
            o_ref[0, ck] = acc


def cmp_first_second(pool, page_ids, col_block, w_cat, n_batch, n_pages):
    rows = pool.shape[1]
    width = 2 * C_KV_HEADS * HEAD_DIM
    assert n_pages % PAGES_PER_GROUP == 0 and rows % C_CMP_STRIDE == 0
    chunks_per_group = PAGES_PER_GROUP * rows // C_CMP_STRIDE
    n_out = w_cat.shape[-1]
    return pl.pallas_call(
        _cmp_fs_kernel,
        out_shape=jax.ShapeDtypeStruct((n_batch, 2 * C_KV_HEADS, n_pages * rows // C_CMP_STRIDE, n_out), F32),
        grid_spec=pltpu.PrefetchScalarGridSpec(
            num_scalar_prefetch=1,
            grid=(n_batch, n_pages),
            in_specs=[
                pl.BlockSpec((1, rows, width), lambda b, pg, pt: (pt[b * n_pages + pg], 0, col_block)),
                pl.BlockSpec(w_cat.shape, lambda b, pg, pt: (0, 0, 0, 0)),
            ],
            out_specs=pl.BlockSpec((1, 2 * C_KV_HEADS, chunks_per_group, n_out),
                                   lambda b, pg, pt: (b, 0, pg // PAGES_PER_GROUP, 0)),
            scratch_shapes=[pltpu.VMEM((width // HEAD_DIM, PAGES_PER_GROUP * rows, HEAD_DIM), F32)],
        ),
        compiler_params=_cparams(2),
        name="c_cmp_first_second",
    )(page_ids, pool, w_cat)


def _cmp_fs_paged_kernel(pt_ref, *refs):
    del pt_ref
    page_refs, (w_ref, o_ref, acc_scr) = refs[:PAGES_PER_GROUP], refs[PAGES_PER_GROUP:]
    chunks_per_page = page_refs[0].shape[0] // C_CMP_STRIDE
    n_chunks = PAGES_PER_GROUP * chunks_per_page
    n_out = w_ref.shape[-1]

    def chunk_rows(p, c):
        return jnp.concatenate([pr[pl.ds(p, chunks_per_page, stride=C_CMP_STRIDE), c, :, :]
                                .reshape(chunks_per_page * C_KV_HEADS, HEAD_DIM) for pr in page_refs], axis=0)

    for c in range(2):
        acc = jnp.zeros((n_chunks * C_KV_HEADS, n_out), F32)
        for pp in range(C_CMP_STRIDE // 2):
            lhs = jnp.concatenate([chunk_rows(2 * pp, c), chunk_rows(2 * pp + 1, c)], axis=1)
            acc = acc + _dot(lhs.astype(BF16), w_ref[c, pp])
        for j in range(n_out // LANES):
            acc_scr[j] = acc[:, j * LANES:(j + 1) * LANES]
        for k in range(C_KV_HEADS):
            for j in range(n_out // LANES):
                o_ref[0, c * C_KV_HEADS + k, :, j * LANES:(j + 1) * LANES] = (
                    acc_scr[j, pl.ds(k, n_chunks, stride=C_KV_HEADS), :])


def cmp_first_second_paged(pool, li, page_table, w_cat):
    n_batch, n_pages = page_table.shape
    rows = pool.shape[2]
    assert n_pages % PAGES_PER_GROUP == 0 and rows % C_CMP_STRIDE == 0
    chunks_per_group = PAGES_PER_GROUP * rows // C_CMP_STRIDE
    n_out = w_cat.shape[-1]

    def page_spec(s):
        return pl.BlockSpec((None, None, rows, 2, C_KV_HEADS, HEAD_DIM),
                            lambda b, grp, pt: (li, pt[b * n_pages + grp * PAGES_PER_GROUP + s], 0, 0, 0, 0))

    return pl.pallas_call(
        _cmp_fs_paged_kernel,
        out_shape=jax.ShapeDtypeStruct((n_batch, 2 * C_KV_HEADS, n_pages * rows // C_CMP_STRIDE, n_out), F32),
        grid_spec=pltpu.PrefetchScalarGridSpec(
            num_scalar_prefetch=1,
            grid=(n_batch, n_pages // PAGES_PER_GROUP),
            in_specs=[page_spec(s) for s in range(PAGES_PER_GROUP)]
            + [pl.BlockSpec(w_cat.shape, lambda b, grp, pt: (0, 0, 0, 0))],
            out_specs=pl.BlockSpec((1, 2 * C_KV_HEADS, chunks_per_group, n_out), lambda b, grp, pt: (b, 0, grp, 0)),
            scratch_shapes=[pltpu.VMEM((n_out // LANES, chunks_per_group * C_KV_HEADS, LANES), F32)],
        ),
        compiler_params=_cparams(2),
        name="c_cmp_first_second_paged",
    )(page_table.reshape(-1), *([pool] * PAGES_PER_GROUP), w_cat)


def _cmp_mlp_kernel(fs_ref, pe_ref, w1_ref, w2_ref, kg_ref, o_ref):
    ck = pl.program_id(1)
    nch = fs_ref.shape[2]
    hid_w = w2_ref.shape[1]
    pe_term = _dot(pe_ref[0].astype(BF16), w1_ref[0].astype(BF16))[0:1, :]
    first = fs_ref[0, 0, :, 0:hid_w]
    second = fs_ref[0, 0, :, hid_w:2 * hid_w]
    nxt = pltpu.roll(second, nch - 1, 0)
    last = lax.broadcasted_iota(I32, (nch, 1), 0) == nch - 1
    pre = first + jnp.where(last, 0.0, nxt) + pe_term
    hid = pre * jax.nn.sigmoid(pre)
    out = _dot(hid.astype(BF16), w2_ref[0].astype(BF16))

    @pl.when(ck < C_KV_HEADS)
    def _():
        o_ref[0, 0] = _rms(out, kg_ref[...])

    @pl.when(ck >= C_KV_HEADS)
    def _():
        o_ref[0, 0] = out


def cmp_mlp(fs, pe8, w1_flat, w2, k_gain0):
    n_batch, n_ck, nch, _ = fs.shape
    return pl.pallas_call(
        _cmp_mlp_kernel,
        out_shape=jax.ShapeDtypeStruct((n_batch, n_ck, nch, HEAD_DIM), F32),
        grid=(n_batch, n_ck),
        in_specs=[
            pl.BlockSpec((1, 1, nch, fs.shape[-1]), lambda b, ck: (b, ck, 0, 0)),
            pl.BlockSpec((1,) + pe8.shape[1:], lambda b, ck: (ck // C_KV_HEADS, 0, 0)),
            pl.BlockSpec((1,) + w1_flat.shape[1:], lambda b, ck: (ck // C_KV_HEADS, 0, 0)),
            pl.BlockSpec((1,) + w2.shape[1:], lambda b, ck: (ck // C_KV_HEADS, 0, 0)),
            pl.BlockSpec((1, HEAD_DIM), lambda b, ck: (0, 0)),
        ],
        out_specs=pl.BlockSpec((1, 1, nch, HEAD_DIM), lambda b, ck: (b, ck, 0, 0)),
        compiler_params=_cparams(2),
        name="c_cmp_mlp",
    )(fs, pe8, w1_flat, w2, k_gain0)


def _masked_softmax_rows(s, ok):
    s = jnp.where(ok, s, NEG)
    m = jnp.max(s, axis=1, keepdims=True)
    e = jnp.where(ok, jnp.exp(s - m), 0.0)
    return e, jnp.maximum(jnp.sum(e, axis=1, keepdims=True), 1e-30)


def _nsa_prompt_kernel(q_ref, gate_ref, cos_ref, sin_ref, kc_ref, vc_ref, ks_ref, vs_ref, kw_ref, vw_ref,
                       ov_ref, ex_ref, o_ref, s_scr, mx_scr, ext_scr, *, nsel):
    qb = q_ref.shape[0]
    nch = kc_ref.shape[2]
    tk = s_scr.shape[2]
    t0 = pl.program_id(2) * qb

    def stacked_bias(ok):
        return jnp.concatenate([jnp.where(ok, 0.0, NEG)] * C_GROUP, axis=0)

    def t_rows(n):
        return t0 + lax.broadcasted_iota(I32, (qb, n), 0)

    qscale = HEAD_DIM ** -0.5 * LOG2_E
    qs = [q_ref[:, g * HEAD_DIM:(g + 1) * HEAD_DIM] for g in range(C_GROUP)]
    qn4 = jnp.concatenate([(q * qscale).astype(BF16) for q in qs], axis=0)
    qr4 = jnp.concatenate([(_rope(q, cos_ref[...], sin_ref[...]) * qscale).astype(BF16) for q in qs], axis=0)

    wk = C_WINDOW + qb
    start = pl.multiple_of(jnp.maximum(t0 - C_WINDOW, 0), qb)
    dist = t_rows(wk) - (start + lax.broadcasted_iota(I32, (qb, wk), 1))
    s = _dot_nt(qr4, kw_ref[0, pl.ds(start, wk), :].astype(BF16)) + stacked_bias((dist >= 0) & (dist < C_WINDOW))
    p = jnp.exp2(s - jnp.max(s, axis=1, keepdims=True)).astype(BF16)
    ext = _dot(p, _with_ones(vw_ref[0, pl.ds(start, wk), :]))
    o_win4 = ext[:, :HEAD_DIM] / ext[:, HEAD_DIM:]

    cmp_bias = stacked_bias(lax.broadcasted_iota(I32, (qb, nch), 1) * C_CMP_STRIDE + (C_CMP_LEN - 1) <= t_rows(nch))
    s = _dot_nt(qn4, kc_ref[0, 0].astype(BF16)) + cmp_bias
    e = jnp.where(cmp_bias < 0.0, 0.0, jnp.exp2(s - jnp.max(s, axis=1, keepdims=True)))
    ext = _dot(e.astype(BF16), _with_ones(vc_ref[0, 0]))
    den = jnp.maximum(ext[:, HEAD_DIM:], 1e-30)
    o_cmp4 = ext[:, :HEAD_DIM] / den
    p = e / den
    p_sum = p[0:qb]
    for g in range(1, C_GROUP):
        p_sum = p_sum + p[g * qb:(g + 1) * qb]

    p_hi = p_sum.astype(BF16)
    p_lo = (p_sum - p_hi.astype(F32)).astype(BF16)
    nr = -(-nsel // 8) * 8
    imp = (_dot_nt(ov_ref[...], p_hi) + _dot_nt(ov_ref[...], p_lo))[:nr]
    blk = lax.broadcasted_iota(I32, (nr, qb), 0)
    cur = (t0 + lax.broadcasted_iota(I32, (nr, qb), 1)) // C_SEL_BLOCK
    causal = (blk <= cur) & (blk < nsel)
    forced = (blk == 0) | (blk == cur) | (blk == cur - 1)
    imp = jnp.where(causal, jnp.where(forced, C_FORCE_SCORE, imp), NEG)
    rank = jnp.zeros((nr, qb), I32)
    for jp in range(nsel):
        c = jnp.broadcast_to(imp[jp:jp + 1, :], (nr, qb))
        beats = (c > imp) | ((c == imp) & (blk > jp))
        rank = rank + beats.astype(I32)
    sel_t = jnp.where((rank < C_SEL_TOPN) & causal, 1.0, 0.0)
    sel_t = jnp.concatenate([sel_t, jnp.zeros((LANES - nr, qb), F32)], axis=0)
    sel = sel_t.T.astype(BF16)

    n_tiles = (t0 + qb + tk - 1) // tk
    mx_scr[...] = jnp.full(mx_scr.shape, NEG, F32)

    def score_tile(kt, carry):
        k0 = pl.multiple_of(kt * tk, tk)
        sel_keys = _dot(sel, ex_ref[kt])
        ok = (sel_keys > 0.5) & (k0 + lax.broadcasted_iota(I32, (qb, tk), 1) <= t_rows(tk))
        s = _dot_nt(qr4, ks_ref[0, pl.ds(k0, tk), :].astype(BF16)) + stacked_bias(ok)
        s_scr[kt] = s
        mx = mx_scr[...]
        for c in range(tk // LANES):
            mx = jnp.maximum(mx, s[:, c * LANES:(c + 1) * LANES])
        mx_scr[...] = mx
        return carry

    lax.fori_loop(0, n_tiles, score_tile, 0)
    m_slc = jnp.max(mx_scr[...], axis=1, keepdims=True)

    ext_scr[...] = jnp.zeros(ext_scr.shape, F32)

    def value_tile(kt, carry):
        k0 = pl.multiple_of(kt * tk, tk)
        p = jnp.exp2(s_scr[kt] - m_slc).astype(BF16)
        ext_scr[...] += _dot(p, _with_ones(vs_ref[0, pl.ds(k0, tk), :]))
        return carry

    lax.fori_loop(0, n_tiles, value_tile, 0)
    o_slc4 = ext_scr[:, :HEAD_DIM] / ext_scr[:, HEAD_DIM:]

    for g in range(C_GROUP):
        rows = slice(g * qb, (g + 1) * qb)
        o = (gate_ref[:, 3 * g:3 * g + 1] * o_cmp4[rows] + gate_ref[:, 3 * g + 1:3 * g + 2] * o_slc4[rows]
             + gate_ref[:, 3 * g + 2:3 * g + 3] * o_win4[rows])
        o_ref[:, g * HEAD_DIM:(g + 1) * HEAD_DIM] = o.astype(o_ref.dtype)


def nsa_prompt(z, gates, cos, sin, kcv, overlap_t, expand, batch, seq):
    m, nc = z.shape
    d = C_KV_HEADS * C_GROUP * HEAD_DIM
    qb = C_QBLOCK
    nq = seq // qb
    nsel = -(-seq // C_SEL_BLOCK)
    tk = C_KEY_TILE
    assert nsel <= LANES and seq % qb == 0 and seq >= C_WINDOW + qb and seq % tk == 0
    assert expand.shape == (seq // tk, LANES, tk)
    nch = kcv.shape[2]
    assert nch == HEAD_DIM and qb == LANES and overlap_t.shape == (LANES, nch)
    zv = z.reshape(batch, seq, nc)
    cb = d // HEAD_DIM
    kvw = C_KV_HEADS

    def kv_spec(branch, kv):
        off = cb + (branch * 2 + kv) * kvw
        return pl.BlockSpec((1, seq, HEAD_DIM), lambda b, k, q: (b, 0, off + k))

    return pl.pallas_call(
        functools.partial(_nsa_prompt_kernel, nsel=nsel),
        out_shape=jax.ShapeDtypeStruct((m, d), BF16),
        scratch_shapes=[pltpu.VMEM((seq // tk, C_GROUP * qb, tk), F32),
                        pltpu.VMEM((C_GROUP * qb, LANES), F32),
                        pltpu.VMEM((C_GROUP * qb, 2 * HEAD_DIM), F32)],
        grid=(batch, C_KV_HEADS, nq),
        in_specs=[
            pl.BlockSpec((qb, C_GROUP * HEAD_DIM), lambda b, k, q: (b * nq + q, k)),
            pl.BlockSpec((qb, LANES), lambda b, k, q: (b * nq + q, k)),
            pl.BlockSpec((qb, HEAD_DIM), lambda b, k, q: (q, 0)),
            pl.BlockSpec((qb, HEAD_DIM), lambda b, k, q: (q, 0)),
            pl.BlockSpec((1, 1, nch, HEAD_DIM), lambda b, k, q: (b, k, 0, 0)),
            pl.BlockSpec((1, 1, nch, HEAD_DIM), lambda b, k, q: (b, C_KV_HEADS + k, 0, 0)),
            kv_spec(1, 0), kv_spec(1, 1), kv_spec(2, 0), kv_spec(2, 1),
            pl.BlockSpec(overlap_t.shape, lambda b, k, q: (0, 0)),
            pl.BlockSpec(expand.shape, lambda b, k, q: (0, 0, 0)),
        ],
        out_specs=pl.BlockSpec((qb, C_GROUP * HEAD_DIM), lambda b, k, q: (b * nq + q, k)),
        compiler_params=_cparams(3),
        name="c_nsa_prompt",
    )(z, gates, cos, sin, kcv, kcv, zv, zv, zv, zv, overlap_t, expand)


def _nsa_sample_cmp_kernel(q_ref, kc_ref, vc_ref, ov_ref, ocmp_ref, idx_ref, *, t, nsel):
    nch = kc_ref.shape[2]
    scale = HEAD_DIM ** -0.5
    q = (q_ref[0, 0] * scale).astype(BF16)
    cmp_ok = lax.broadcasted_iota(I32, (8, nch), 1) * C_CMP_STRIDE + (C_CMP_LEN - 1) <= t
    e, den = _masked_softmax_rows(_dot_nt(q, kc_ref[0, 0].astype(BF16)), cmp_ok)
    p = e / den
    ocmp_ref[0, 0] = _dot(p.astype(BF16), vc_ref[0, 0].astype(BF16))
    real = lax.broadcasted_iota(I32, (8, nch), 0) < C_GROUP
    p_sum = jnp.broadcast_to(jnp.sum(jnp.where(real, p, 0.0), axis=0, keepdims=True), (8, nch))
    p_hi = p_sum.astype(BF16)
    p_lo = (p_sum - p_hi.astype(F32)).astype(BF16)
    imp = _dot(p_hi, ov_ref[...]) + _dot(p_lo, ov_ref[...])
    width = imp.shape[1]
    lane = lax.broadcasted_iota(I32, (8, width), 1)
    cur = t // C_SEL_BLOCK
    causal = (lane <= cur) & (lane < nsel)
    forced = (lane == 0) | (lane == cur) | (lane == cur - 1)
    work = jnp.where(causal, jnp.where(forced, C_FORCE_SCORE, imp), NEG)
    lane_f = lane.astype(F32)
    out_lane = lax.broadcasted_iota(I32, (8, LANES), 1)
    idx = jnp.full((8, LANES), -1, I32)
    for r in range(C_SEL_TOPN):
        best = jnp.max(work, axis=1, keepdims=True)
        pick = jnp.min(jnp.where(work == best, lane_f, float(width)), axis=1, keepdims=True)
        found = jnp.where(best > 0.5 * NEG, pick, -1.0).astype(I32)
        idx = jnp.where(out_lane == r, found, idx)
        work = jnp.where(lane_f == pick, NEG, work)
    idx_ref[0, 0] = idx


def nsa_sample_cmp(q8, kcv, overlap, t, nsel):
    n_batch = q8.shape[0]
    nch = kcv.shape[2]
    blk = pl.BlockSpec((1, 1, 8, HEAD_DIM), lambda b, k: (b, k, 0, 0))
    return pl.pallas_call(
        functools.partial(_nsa_sample_cmp_kernel, t=t, nsel=nsel),
        out_shape=[jax.ShapeDtypeStruct((n_batch, C_KV_HEADS, 8, HEAD_DIM), F32),
                   jax.ShapeDtypeStruct((n_batch, C_KV_HEADS, 8, LANES), I32)],
        grid=(n_batch, C_KV_HEADS),
        in_specs=[
            blk,
            pl.BlockSpec((1, 1, nch, HEAD_DIM), lambda b, k: (b, k, 0, 0)),
            pl.BlockSpec((1, 1, nch, HEAD_DIM), lambda b, k: (b, C_KV_HEADS + k, 0, 0)),
            pl.BlockSpec(overlap.shape, lambda b, k: (0, 0)),
        ],
        out_specs=[blk, blk],
        compiler_params=_cparams(2),
        name="c_nsa_sample_cmp",
    )(q8, kcv, kcv, overlap)


def _nsa_sample_kernel(idx_ref, pt_ref, q_ref, cos_ref, sin_ref, *refs, t, n_past_blocks):
    del pt_ref
    slc_refs, (new_ref, win_ref, ocmp_ref, gate_ref, o_ref) = refs[:C_SEL_TOPN], refs[C_SEL_TOPN:]
    b, k = pl.program_id(0), pl.program_id(1)

    def head_rows(ref, kv):
        return ref[:, kv, pl.ds(k, 1), :][:, 0, :].astype(BF16)

    base = (b * C_KV_HEADS + k) * C_SEL_TOPN
    scale = HEAD_DIM ** -0.5
    qr_f = _rope(q_ref[0, 0], cos_ref[...], sin_ref[...]) * scale
    qr = qr_f.astype(BF16)

    nk = C_SEL_TOPN * C_SEL_BLOCK
    s = _dot_nt(qr, jnp.concatenate([head_rows(r, 0) for r in slc_refs], axis=0))
    lane = lax.broadcasted_iota(I32, (8, nk), 1)
    blk = jnp.full((8, nk), -1, I32)
    has_new = False
    for n in range(C_SEL_TOPN):
        blk_n = idx_ref[base + n]
        blk = jnp.where((lane >= n * C_SEL_BLOCK) & (lane < (n + 1) * C_SEL_BLOCK), blk_n, blk)
        has_new = has_new | (blk_n == n_past_blocks)
    ok = (blk >= 0) & (blk < n_past_blocks) & (blk * C_SEL_BLOCK + (lane & (C_SEL_BLOCK - 1)) <= t)
    k_new, v_new = new_ref[0, 0, 0:1, :], new_ref[0, 0, 1:2, :]
    s = jnp.where(ok, s, NEG)
    s_new = jnp.where(has_new, jnp.sum(qr_f * k_new, axis=1, keepdims=True), NEG)
    m = jnp.maximum(jnp.max(s, axis=1, keepdims=True), s_new)
    p = jnp.where(ok, jnp.exp(s - m), 0.0)
    p_new = jnp.where(has_new, jnp.exp(s_new - m), 0.0)
    den = jnp.maximum(jnp.sum(p, axis=1, keepdims=True) + p_new, 1e-30)
    vs = jnp.concatenate([head_rows(r, 1) for r in slc_refs], axis=0)
    o_slc = (_dot(p.astype(BF16), vs) + p_new * v_new) / den

    lw = win_ref.shape[0]
    kw_new, vw_new = new_ref[0, 0, 2:3, :], new_ref[0, 0, 3:4, :]
    dist = lw - lax.broadcasted_iota(I32, (8, lw), 1)
    w_ok = (dist >= 0) & (dist < C_WINDOW)
    s = jnp.where(w_ok, _dot_nt(qr, head_rows(win_ref, 0)), NEG)
    s_new = jnp.sum(qr_f * kw_new, axis=1, keepdims=True)
    m = jnp.maximum(jnp.max(s, axis=1, keepdims=True), s_new)
    p = jnp.where(w_ok, jnp.exp(s - m), 0.0)
    p_new = jnp.exp(s_new - m)
    den = jnp.sum(p, axis=1, keepdims=True) + p_new
    o_win = (_dot(p.astype(BF16), head_rows(win_ref, 1)) + p_new * vw_new) / den

    gate = gate_ref[0, 0]
    o_ref[0, 0] = gate[:, 0:1] * ocmp_ref[0, 0] + gate[:, 1:2] * o_slc + gate[:, 2:3] * o_win


def nsa_sample(top_idx, page_table, q8, cos, sin, pool_slc, win_buf, li, new_kv, ocmp, gate8, t, n_past_blocks):
    n_batch = q8.shape[0]
    n_pages = page_table.shape[1]
    n_layers, n_pool, page_rows = pool_slc.shape[:3]
    blocks_per_page = page_rows // C_SEL_BLOCK
    kv_dims = (2, C_KV_HEADS, HEAD_DIM)
    pool = pool_slc.reshape((n_layers, n_pool, blocks_per_page, C_SEL_BLOCK) + kv_dims)
    lw = win_buf.shape[2]

    def slc_spec(n):
        def index(b, k, idx, pt):
            blk = jnp.clip(idx[(b * C_KV_HEADS + k) * C_SEL_TOPN + n], 0, n_past_blocks - 1)
            return (li, pt[b * n_pages + blk // blocks_per_page], blk % blocks_per_page, 0, 0, 0, 0)
        return pl.BlockSpec((None, None, None, C_SEL_BLOCK) + kv_dims, index)

    win_spec = pl.BlockSpec((None, None, lw) + kv_dims, lambda b, k, idx, pt: (li, b, 0, 0, 0, 0))
    blk8 = pl.BlockSpec((1, 1, 8, HEAD_DIM), lambda b, k, idx, pt: (b, k, 0, 0))
    tab = pl.BlockSpec((8, HEAD_DIM), lambda b, k, idx, pt: (0, 0))
    return pl.pallas_call(
        functools.partial(_nsa_sample_kernel, t=t, n_past_blocks=n_past_blocks),
        out_shape=jax.ShapeDtypeStruct((n_batch, C_KV_HEADS, 8, HEAD_DIM), F32),
        grid_spec=pltpu.PrefetchScalarGridSpec(
            num_scalar_prefetch=2,
            grid=(n_batch, C_KV_HEADS),
            in_specs=[blk8, tab, tab] + [slc_spec(n) for n in range(C_SEL_TOPN)] + [blk8, win_spec, blk8, blk8],
            out_specs=blk8,
        ),
        compiler_params=_cparams(2),
        name="c_nsa_sample",
    )(top_idx.reshape(-1), page_table.reshape(-1), q8, cos, sin, *([pool] * C_SEL_TOPN), new_kv, win_buf, ocmp, gate8)


def _shift_append_kernel(cache_ref, new_ref, out_ref, sems, *, li):
    n_batch, length = out_ref.shape[:2]
    copies = []
    for b in range(n_batch):
        copies.append(pltpu.make_async_copy(cache_ref.at[li, b, pl.ds(1, length - 1)],
                                            out_ref.at[b, pl.ds(0, length - 1)], sems.at[0, b]))
        copies.append(pltpu.make_async_copy(new_ref.at[b], out_ref.at[b, pl.ds(length - 1, 1)], sems.at[1, b]))
    for c in copies:
        c.start()
    for c in copies:
        c.wait()


def shift_append(cache, li, new, name):
    n_batch = cache.shape[1]
    assert new.shape == (n_batch, 1) + cache.shape[3:]
    return pl.pallas_call(
        functools.partial(_shift_append_kernel, li=li),
        out_shape=jax.ShapeDtypeStruct(cache.shape[1:], cache.dtype),
        in_specs=[pl.BlockSpec(memory_space=pl.ANY), pl.BlockSpec(memory_space=pl.ANY)],
        out_specs=pl.BlockSpec(memory_space=pl.ANY),
        scratch_shapes=[pltpu.SemaphoreType.DMA((2, n_batch))],
        name=name,
    )(cache, new)


def _rope_tables(pos):
    half = HEAD_DIM // 2
    inv = ROPE_THETA ** (-(jnp.arange(half, dtype=F32) * 2.0 / HEAD_DIM))
    ang = pos.astype(F32)[:, None] * inv[None, :]
    cos, sin = jnp.cos(ang), jnp.sin(ang)
    return jnp.concatenate([cos, cos], axis=1), jnp.concatenate([-sin, sin], axis=1)


def _overlap_matrix(nch, nsel, width):
    cmp_start = jnp.arange(nch) * C_CMP_STRIDE
    sel_start = jnp.arange(width) * C_SEL_BLOCK
    ov = ((cmp_start[:, None] < sel_start[None, :] + C_SEL_BLOCK)
          & (cmp_start[:, None] + C_CMP_LEN > sel_start[None, :])
          & (jnp.arange(width)[None, :] < nsel))
    return ov.astype(BF16)


def _c_weights(prm, li, n_heads):
    d = n_heads * HEAD_DIM
    nkv = 3 * 2 * C_KV_HEADS * HEAD_DIM
    k_gain = prm['c_k_norm'][li]
    ones = jnp.ones((C_KV_HEADS * HEAD_DIM,), F32)
    head_gain = jnp.concatenate([
        jnp.tile(prm['c_q_norm'][li], n_heads), ones, ones,
        jnp.tile(k_gain[1], C_KV_HEADS), ones, jnp.tile(k_gain[2], C_KV_HEADS), ones])[None, :]
    wg = prm['w_c_in'][li][:, d + nkv:].reshape(-1, C_KV_HEADS, C_GROUP * 3)
    wg = jnp.pad(wg, ((0, 0), (0, 0), (0, LANES - C_GROUP * 3))).reshape(1, -1, C_KV_HEADS * LANES)
    w1 = prm['c_cmp_w1'][li]
    w_cat = jnp.concatenate([w1[:, :C_CMP_STRIDE], w1[:, C_CMP_STRIDE:]], axis=-1).astype(BF16)
    w_cat = w_cat.reshape(2, C_CMP_STRIDE // 2, 2 * HEAD_DIM, w_cat.shape[-1])
    pe8 = jnp.pad(prm['c_cmp_pe'][li].reshape(2, 1, -1), ((0, 0), (0, 7), (0, 0)))
    w1_flat = w1.reshape(2, C_CMP_LEN * HEAD_DIM, -1)
    return head_gain, wg, w_cat, pe8, w1_flat, prm['c_cmp_w2'][li], k_gain[0][None, :]


TN = 512
TN_WIDE_K = 256
C_MODES = (NORM,) * 4 + (RAW, RAW, NORM_ROPE, RAW, NORM_ROPE, RAW)
A_MODES = ((NORM_ROPE,) * 8 + (RAW,) * 4) * len(A_PATTERNS)


def _pad_rows(a, rows):
    return jnp.pad(a, ((0, rows - a.shape[0]),) + ((0, 0),) * (a.ndim - 1))


def _a_head_gain(prm, li, n_heads):
    ones = jnp.ones((n_heads * HEAD_DIM,), F32)
    parts = []
    for g in range(len(A_PATTERNS)):
        parts += [jnp.tile(prm['a_q_norm'][li, g], n_heads), jnp.tile(prm['a_k_norm'][li, g], n_heads), ones]
    return jnp.concatenate(parts)[None, :]


def _channel_mixer(x, p, prm, i, tag):
    act = swiglu_in(x, prm['ffn_norm'], prm['w_ffn_in'], i, tn=TN, name=f"ffn_in_{tag}")
    x = linear_residual(act, prm['w_ffn_out'], i, x, tn=TN_WIDE_K, name=f"ffn_out_{tag}")
    return ple(x, prm['ple_norm'], p, prm['w_ple_proj'], prm['w_ple_gate'], i, tn=TN, name=f"ple_{tag}")


def _run_prompt(x3, p, prm):
    batch, seq, d = x3.shape
    n_heads = d // HEAD_DIM
    x = x3.reshape(batch * seq, d)
    cos, sin = _rope_tables(jnp.arange(seq))
    new = {}
    depth = prm['attn_norm'].shape[0]
    for i in range(depth):
        kind, li = i % 4, i // 4
        if kind == 0:
            z = norm_linear(x, prm['attn_norm'], i, prm['w_a_qkv'], li, col0=0, modes=A_MODES,
                            head_gain=_a_head_gain(prm, li, n_heads), cos=cos, sin=sin, tn=TN, name="a_qkv_prompt")
            x = linear_residual(a_attention_prompt(z, batch, seq), prm['w_a_out'], li, x, tn=TN, name="a_out_prompt")
            z3 = z.reshape(batch, seq, -1)
            for g, (win, dil) in enumerate(A_PATTERNS):
                kv = z3[:, seq - min(win, seq):, (3 * g + 1) * d:(3 * g + 3) * d]
                new.setdefault(f'a_w{g + 1}', []).append(kv.reshape(batch, -1, 2, n_heads, HEAD_DIM))
        elif kind == 1:
            bg, u = b_in(x, prm['attn_norm'], i, prm['w_b_in'], li, tn=TN_WIDE_K, name="b_in_prompt")
            x = b_out_prompt(u, bg, prm['b_conv'], prm['w_b_out'], li, x, seq, tn=TN, name="b_out_prompt")
            new.setdefault('b_conv', []).append(u.reshape(batch, seq, d)[:, seq - (CONV_W - 1):])
        elif kind == 2:
            head_gain, wg, w_cat, pe8, w1_flat, w2, k_gain0 = _c_weights(prm, li, n_heads)
            z = norm_linear(x, prm['attn_norm'], i, prm['w_c_in'], li, col0=0, modes=C_MODES,
                            head_gain=head_gain, cos=cos, sin=sin, tn=TN, name="c_in_prompt")
            gates = norm_linear(x, prm['attn_norm'], i, wg, 0, col0=0, modes=(SIGMOID,),
                                head_gain=head_gain[:, :TN], cos=cos, sin=sin, tn=TN, name="c_gate_prompt")
            n_pages = seq // LANES
            pool = z.reshape(batch * n_pages, LANES, z.shape[1])
            kv_width = 2 * C_KV_HEADS * HEAD_DIM
            fs = cmp_first_second(pool, jnp.arange(batch * n_pages, dtype=I32), d // kv_width, w_cat, batch, n_pages)
            kcv = cmp_mlp(fs, pe8, w1_flat, w2, k_gain0)
            nsel = -(-seq // C_SEL_BLOCK)
            overlap = _overlap_matrix(kcv.shape[2], nsel, LANES)
            key_blk = (jnp.arange(seq) // C_SEL_BLOCK).reshape(seq // C_KEY_TILE, 1, C_KEY_TILE)
            expand = (key_blk == jnp.arange(LANES)[None, :, None]).astype(BF16)
            o = nsa_prompt(z, gates, cos, sin, kcv, overlap.T, expand, batch, seq)
            x = linear_residual(o, prm['w_c_out'], li, x, tn=TN, name="c_out_prompt")
            z3 = z.reshape(batch, seq, -1)
            for br, nm in enumerate(('c_cmp', 'c_slc', 'c_win')):
                rows = min(C_WINDOW, seq) if nm == 'c_win' else seq
                kv = z3[:, seq - rows:, d + br * kv_width:d + (br + 1) * kv_width]
                new.setdefault(nm, []).append(kv.reshape(batch, rows, 2, C_KV_HEADS, HEAD_DIM))
        else:
            x, tail = d_mixer_prompt(x, prm['attn_norm'], i, prm['w_d_group'], prm['d_scale'], li, seq,
                                     name="d_mixer_prompt")
            new.setdefault('d_pool', []).append(tail[:, 1:])
        x = _channel_mixer(x, p, prm, i, "prompt")
    return x.reshape(batch, seq, d), {nm: jnp.stack(v, axis=0) for nm, v in new.items()}


def _run_sample(x3, p, prm, past, page_table):
    n_batch, n_new, d = x3.shape
    assert n_new == 1
    n_heads = d // HEAD_DIM
    rows = SUBLANES_BF16
    x = _pad_rows(x3.reshape(n_batch, d), rows)
    past_len = page_table.shape[1] * past['c_cmp'].shape[2]
    cos, sin = _rope_tables(jnp.full((rows,), past_len))
    new = {}
    depth = prm['attn_norm'].shape[0]
    for i in range(depth):
        kind, li = i % 4, i // 4
        if kind == 0:
            z = norm_linear(x, prm['attn_norm'], i, prm['w_a_qkv'], li, col0=0, modes=A_MODES,
                            head_gain=_a_head_gain(prm, li, n_heads), cos=cos, sin=sin, tn=TN, name="a_qkv_sample")
            caches = [past[f'a_w{g + 1}'] for g in range(len(A_PATTERNS))]
            o = a_attention_sample(z, caches, li, n_batch)
            x = linear_residual(_pad_rows(o, rows), prm['w_a_out'], li, x, tn=TN, name="a_out_sample")
            for g, cache in enumerate(caches):
                kv = z[:n_batch, (3 * g + 1) * d:(3 * g + 3) * d].reshape(n_batch, 1, 2, n_heads, HEAD_DIM)
                new.setdefault(f'a_w{g + 1}', []).append(shift_append(cache, li, kv, f"a_w{g + 1}_shift"))
        elif kind == 1:
            bg, u = b_in(x, prm['attn_norm'], i, prm['w_b_in'], li, tn=TN_WIDE_K, name="b_in_sample")
            hist = past['b_conv'][li]
            x = b_out_sample(u, _pad_rows(hist[:, 1], rows), _pad_rows(hist[:, 0], rows), bg, prm['b_conv'],
                             prm['w_b_out'], li, x, tn=TN, name="b_out_sample")
            new.setdefault('b_conv', []).append(jnp.concatenate([hist[:, 1:], u[:n_batch, None]], axis=1))
        elif kind == 2:
            head_gain, wg, w_cat, pe8, w1_flat, w2, k_gain0 = _c_weights(prm, li, n_heads)
            z = norm_linear(x, prm['attn_norm'], i, prm['w_c_in'], li, col0=0, modes=C_MODES,
                            head_gain=head_gain, cos=cos, sin=sin, tn=TN, name="c_in_sample")
            gates = norm_linear(x, prm['attn_norm'], i, wg, 0, col0=0, modes=(SIGMOID,),
                                head_gain=head_gain[:, :TN], cos=cos, sin=sin, tn=TN, name="c_gate_sample")
            kv_width = 2 * C_KV_HEADS * HEAD_DIM
            kcv = cmp_mlp(cmp_first_second_paged(past['c_cmp'], li, page_table, w_cat), pe8, w1_flat, w2, k_gain0)
            nsel = -(-(past_len + 1) // C_SEL_BLOCK)
            width = -(-nsel // LANES) * LANES
            overlap = _overlap_matrix(kcv.shape[2], nsel, width)
            zb = z[:n_batch]
            q8 = jnp.pad(zb[:, :d].reshape(n_batch, C_KV_HEADS, C_GROUP, HEAD_DIM),
                         ((0, 0), (0, 0), (0, 8 - C_GROUP), (0, 0)))
            ocmp, idx = nsa_sample_cmp(q8, kcv, overlap, past_len, nsel)
            top_idx = idx[:, :, 0, :C_SEL_TOPN]
            kvh_w = C_KV_HEADS * HEAD_DIM
            new_rows = [zb[:, d + kv_width + j * kvh_w:d + kv_width + (j + 1) * kvh_w]
                        .reshape(n_batch, C_KV_HEADS, 1, HEAD_DIM) for j in range(4)]
            new_kv = jnp.pad(jnp.concatenate(new_rows, axis=2), ((0, 0), (0, 0), (0, 4), (0, 0)))
            gate8 = gates[:n_batch].reshape(n_batch, C_KV_HEADS, LANES)[:, :, :C_GROUP * 3]
            gate8 = jnp.pad(gate8.reshape(n_batch, C_KV_HEADS, C_GROUP, 3),
                            ((0, 0), (0, 0), (0, 8 - C_GROUP), (0, HEAD_DIM - 3)))
            o = nsa_sample(top_idx, page_table, q8, cos[:8], sin[:8], past['c_slc'], past['c_win'], li, new_kv,
                           ocmp, gate8, past_len, past_len // C_SEL_BLOCK)
            o = o[:, :, :C_GROUP].reshape(n_batch, d)
            x = linear_residual(_pad_rows(o, rows), prm['w_c_out'], li, x, tn=TN, name="c_out_sample")
            for br, nm in enumerate(('c_cmp', 'c_slc', 'c_win')):
                kv = zb[:, d + br * kv_width:d + (br + 1) * kv_width].reshape(n_batch, 1, 2, C_KV_HEADS, HEAD_DIM)
                if nm == 'c_win':
                    kv = shift_append(past['c_win'], li, kv, "c_win_shift")
                new.setdefault(nm, []).append(kv)
        else:
            hist = past['d_pool'][li]
            hist_t = jnp.pad(jnp.swapaxes(hist, 0, 1), ((0, 0), (0, rows - n_batch), (0, 0)))
            x, h = d_mixer_sample(x, hist_t, prm['attn_norm'], i, prm['w_d_group'], prm['d_scale'], li,
                                  name="d_mixer_sample")
            new.setdefault('d_pool', []).append(jnp.concatenate([hist[:, 1:], h[:n_batch, None]], axis=1))
        x = _channel_mixer(x, p, prm, i, "sample")
    return x[:n_batch].reshape(n_batch, 1, d), {nm: jnp.stack(v, axis=0) for nm, v in new.items()}


def kernel(x_prompt, x_sample, cache_a_w1, cache_a_w2, cache_a_w3, state_b_conv, cache_c_cmp, cache_c_slc, cache_c_win, state_d_pool, page_table, p_prompt, p_sample, attn_norm, ffn_norm, ple_norm, w_a_qkv, a_q_norm, a_k_norm, w_a_out, w_b_in, b_conv, w_b_out, w_c_in, c_q_norm, c_k_norm, c_cmp_pe, c_cmp_w1, c_cmp_w2, w_c_out, w_d_group, d_scale, w_ffn_in, w_ffn_out, w_ple_proj, w_ple_gate):
    prm = dict(attn_norm=attn_norm, ffn_norm=ffn_norm, ple_norm=ple_norm, w_a_qkv=w_a_qkv, a_q_norm=a_q_norm,
               a_k_norm=a_k_norm, w_a_out=w_a_out, w_b_in=w_b_in, b_conv=b_conv, w_b_out=w_b_out, w_c_in=w_c_in,
               c_q_norm=c_q_norm, c_k_norm=c_k_norm, c_cmp_pe=c_cmp_pe, c_cmp_w1=c_cmp_w1, c_cmp_w2=c_cmp_w2,
               w_c_out=w_c_out, w_d_group=w_d_group, d_scale=d_scale, w_ffn_in=w_ffn_in, w_ffn_out=w_ffn_out,
               w_ple_proj=w_ple_proj, w_ple_gate=w_ple_gate)
    past = dict(a_w1=cache_a_w1, a_w2=cache_a_w2, a_w3=cache_a_w3, b_conv=state_b_conv, c_cmp=cache_c_cmp,
                c_slc=cache_c_slc, c_win=cache_c_win, d_pool=state_d_pool)
    depth = attn_norm.shape[0]
    batch, seq, _ = x_prompt.shape
    n_dec = x_sample.shape[0]
    p_p = p_prompt.reshape(depth, batch * seq, -1)
    p_s = jnp.pad(p_sample.reshape(depth, n_dec, -1), ((0, 0), (0, SUBLANES_BF16 - n_dec), (0, 0)))
    y_prompt, sp = _run_prompt(x_prompt, p_p, prm)
    y_sample, ss = _run_sample(x_sample, p_s, prm, past, page_table)
    return (y_prompt, y_sample,
            sp['a_w1'], ss['a_w1'], sp['a_w2'], ss['a_w2'], sp['a_w3'], ss['a_w3'],
            sp['b_conv'], ss['b_conv'],
            sp['c_cmp'], ss['c_cmp'], sp['c_slc'], ss['c_slc'], sp['c_win'], ss['c_win'],
            sp['d_pool'], ss['d_pool'])
```

```python
import functools

import jax
import jax.numpy as jnp
from jax import lax
from jax.experimental import pallas as pl
from jax.experimental.pallas import tpu as pltpu

F32 = jnp.float32
BF16 = jnp.bfloat16
I32 = jnp.int32

HEAD_DIM = 128
LANES = 128
SUBLANES_BF16 = 16
RMS_EPS = 1e-6
ROPE_THETA = 10000.0
NEG = -1e30
V7X_VMEM_BYTES = 64 * 1024 * 1024
VMEM_LIMIT = V7X_VMEM_BYTES - 8 * 1024 * 1024

A_PATTERNS = ((128, 1), (512, 4), (2048, 16))
A_BLOCK = 128
A_UNROLL = 4
CONV_W = 3
C_KV_HEADS = 4
C_GROUP = 4
C_CMP_LEN = 32
C_CMP_STRIDE = 16
C_SEL_BLOCK = 64
C_SEL_TOPN = 16
C_WINDOW = 512
C_FORCE_SCORE = 1e4
C_QBLOCK = 128
C_KEY_TILE = 512
LOG2_E = 1.4426950408889634
D_POOLS = (2, 4, 8, 16)
D_HIST = max(D_POOLS) - 1
PAGES_PER_GROUP = 16
SHIFT_ROWS = 256

RAW, NORM, NORM_ROPE, SIGMOID = range(4)


def _cparams(n_axes):
    return pltpu.CompilerParams(dimension_semantics=("arbitrary",) * n_axes,
                                vmem_limit_bytes=VMEM_LIMIT)


def _row_tile(m):
    return 1024 if m % 1024 == 0 else m


def _dot(a, b):
    return jnp.dot(a, b, preferred_element_type=F32)


def _dot_nt(a, b):
    return lax.dot_general(a, b, (((1,), (1,)), ((), ())), preferred_element_type=F32)


def _with_ones(v):
    return jnp.concatenate([v.astype(BF16), jnp.ones(v.shape, BF16)], axis=1)


def _rms(x, g):
    return x * lax.rsqrt(jnp.mean(x * x, axis=-1, keepdims=True) + RMS_EPS) * g


def _rope(y, cos, sin):
    return y * cos + pltpu.roll(y, HEAD_DIM // 2, 1) * sin


def _tile_pred(j, tiles):
    pred = None
    lo = prev = tiles[0]
    runs = []
    for t in tiles[1:]:
        if t != prev + 1:
            runs.append((lo, prev))
            lo = t
        prev = t
    runs.append((lo, prev))
    for lo, hi in runs:
        p = (j == lo) if lo == hi else ((j >= lo) & (j <= hi))
        pred = p if pred is None else (pred | p)
    return pred


def _norm_linear_kernel(x_ref, g_ref, w_ref, hg_ref, cos_ref, sin_ref, o_ref, h_scr, *, modes):
    j = pl.program_id(1)

    @pl.when(j == 0)
    def _():
        h_scr[...] = _rms(x_ref[...], g_ref[...]).astype(BF16)

    acc = _dot(h_scr[...], w_ref[...].astype(BF16))
    tn = acc.shape[1]

    def emit(mode):
        if mode == RAW:
            o_ref[...] = acc.astype(o_ref.dtype)
        elif mode == SIGMOID:
            o_ref[...] = jax.nn.sigmoid(acc).astype(o_ref.dtype)
        else:
            for c in range(tn // HEAD_DIM):
                sl = slice(c * HEAD_DIM, (c + 1) * HEAD_DIM)
                y = _rms(acc[:, sl], hg_ref[:, sl])
                if mode == NORM_ROPE:
                    y = _rope(y, cos_ref[...], sin_ref[...])
                o_ref[:, sl] = y.astype(o_ref.dtype)

    kinds = sorted(set(modes))
    if len(kinds) == 1:
        emit(kinds[0])
    else:
        for m in kinds:
            tiles = [t for t, mm in enumerate(modes) if mm == m]
            pl.when(_tile_pred(j, tiles))(functools.partial(emit, m))


def norm_linear(x, gain, gi, w, wi, *, col0, modes, head_gain, cos, sin, tn, name):
    m, k = x.shape
    tm = _row_tile(m)
    nt = len(modes)
    assert col0 % tn == 0 and cos.shape[0] % tm == 0
    coff = col0 // tn
    n_pos_tiles = cos.shape[0] // tm
    return pl.pallas_call(
        functools.partial(_norm_linear_kernel, modes=tuple(modes)),
        out_shape=jax.ShapeDtypeStruct((m, nt * tn), F32),
        grid=(m // tm, nt),
        in_specs=[
            pl.BlockSpec((tm, k), lambda i, j: (i, 0)),
            pl.BlockSpec((None, 1, k), lambda i, j: (gi, 0, 0)),
            pl.BlockSpec((None, k, tn), lambda i, j: (wi, 0, j + coff)),
            pl.BlockSpec((1, tn), lambda i, j: (0, j)),
            pl.BlockSpec((tm, HEAD_DIM), lambda i, j: (i % n_pos_tiles, 0)),
            pl.BlockSpec((tm, HEAD_DIM), lambda i, j: (i % n_pos_tiles, 0)),
        ],
        out_specs=pl.BlockSpec((tm, tn), lambda i, j: (i, j)),
        scratch_shapes=[pltpu.VMEM((tm, k), BF16)],
        compiler_params=_cparams(2),
        name=name,
    )(x, gain.reshape(gain.shape[0], 1, k), w, head_gain, cos, sin)


def _linear_res_kernel(a_ref, w_ref, r_ref, o_ref):
    o_ref[...] = r_ref[...] + _dot(a_ref[...].astype(BF16), w_ref[...].astype(BF16))


def linear_residual(a, w, li, res, *, tn, name):
    m, k = a.shape
    n = w.shape[-1]
    tm = _row_tile(m)
    return pl.pallas_call(
        _linear_res_kernel,
        out_shape=jax.ShapeDtypeStruct((m, n), F32),
        grid=(m // tm, n // tn),
        in_specs=[
            pl.BlockSpec((tm, k), lambda i, j: (i, 0)),
            pl.BlockSpec((None, k, tn), lambda i, j: (li, 0, j)),
            pl.BlockSpec((tm, tn), lambda i, j: (i, j)),
        ],
        out_specs=pl.BlockSpec((tm, tn), lambda i, j: (i, j)),
        compiler_params=_cparams(2),
        name=name,
    )(a, w, res)


def _swiglu_in_kernel(x_ref, g_ref, wg_ref, wu_ref, o_ref, h_scr):
    @pl.when(pl.program_id(1) == 0)
    def _():
        h_scr[...] = _rms(x_ref[...], g_ref[...]).astype(BF16)

    h = h_scr[...]
    a = _dot(h, wg_ref[...].astype(BF16))
    b = _dot(h, wu_ref[...].astype(BF16))
    o_ref[...] = (a * jax.nn.sigmoid(a) * b).astype(o_ref.dtype)


def swiglu_in(x, gain, w, li, *, tn, name):
    m, k = x.shape
    f = w.shape[-1] // 2
    tm = _row_tile(m)
    nt = f // tn
    return pl.pallas_call(
        _swiglu_in_kernel,
        out_shape=jax.ShapeDtypeStruct((m, f), BF16),
        grid=(m // tm, nt),
        in_specs=[
            pl.BlockSpec((tm, k), lambda i, j: (i, 0)),
            pl.BlockSpec((None, 1, k), lambda i, j: (li, 0, 0)),
            pl.BlockSpec((None, k, tn), lambda i, j: (li, 0, j)),
            pl.BlockSpec((None, k, tn), lambda i, j: (li, 0, j + nt)),
        ],
        out_specs=pl.BlockSpec((tm, tn), lambda i, j: (i, j)),
        scratch_shapes=[pltpu.VMEM((tm, k), BF16)],
        compiler_params=_cparams(2),
        name=name,
    )(x, gain.reshape(gain.shape[0], 1, k), w, w)


def _ple_kernel(x_ref, g_ref, p_ref, wp_ref, wg_ref, o_ref, h_scr):
    j = pl.program_id(1)
    tn = o_ref.shape[1]

    @pl.when(j == 0)
    def _():
        h_scr[...] = _rms(x_ref[...], g_ref[...]).astype(BF16)

    gate = jax.nn.sigmoid(_dot(h_scr[...], wg_ref[...].astype(BF16)))
    proj = _dot(p_ref[...].astype(BF16), wp_ref[...].astype(BF16))
    o_ref[...] = x_ref[:, pl.ds(pl.multiple_of(j * tn, tn), tn)] + proj * gate


def ple(x, gain, p, wp, wg, li, *, tn, name):
    m, k = x.shape
    pd = p.shape[-1]
    tm = _row_tile(m)
    assert p.shape[1] == m
    return pl.pallas_call(
        _ple_kernel,
        out_shape=jax.ShapeDtypeStruct((m, k), F32),
        grid=(m // tm, k // tn),
        in_specs=[
            pl.BlockSpec((tm, k), lambda i, j: (i, 0)),
            pl.BlockSpec((None, 1, k), lambda i, j: (li, 0, 0)),
            pl.BlockSpec((None, tm, pd), lambda i, j: (li, i, 0)),
            pl.BlockSpec((None, pd, tn), lambda i, j: (li, 0, j)),
            pl.BlockSpec((None, k, tn), lambda i, j: (li, 0, j)),
        ],
        out_specs=pl.BlockSpec((tm, tn), lambda i, j: (i, j)),
        scratch_shapes=[pltpu.VMEM((tm, k), BF16)],
        compiler_params=_cparams(2),
        name=name,
    )(x, gain.reshape(gain.shape[0], 1, k), p, wp, wg)


def _a_prompt_kernel(*refs, seq):
    qkv_refs, o_ref = refs[:9], refs[9]
    acc_scr, m_scr, l_scr = refs[10:]
    scale = HEAD_DIM ** -0.5
    row = lax.broadcasted_iota(I32, (A_BLOCK, A_BLOCK), 0)
    col = lax.broadcasted_iota(I32, (A_BLOCK, A_BLOCK), 1)
    cur_ok = col <= row
    prev_ok = col >= row
    wide = (A_BLOCK, HEAD_DIM)
    first = len(A_PATTERNS) - 1
    for g, (win, dil) in reversed(list(enumerate(A_PATTERNS))):
        q_ref, k_ref, v_ref = qkv_refs[3 * g:3 * g + 3]
        nb = seq // dil // A_BLOCK

        def rows_of(r, n, dil=dil):
            start = r + dil * A_BLOCK * n
            if dil == 1:
                return pl.ds(pl.multiple_of(start, A_BLOCK), A_BLOCK)
            return pl.ds(start, A_BLOCK, stride=dil)

        def attend(idx, nb=nb, q_ref=q_ref, k_ref=k_ref, v_ref=v_ref, rows_of=rows_of):
            r, n = idx // nb, idx % nb
            rows = rows_of(r, n)
            q = (q_ref[0, rows, :] * scale).astype(BF16)
            s_c = jnp.where(cur_ok, _dot_nt(q, k_ref[0, rows, :].astype(BF16)), NEG)
            if nb > 1:
                prows = rows_of(r, jnp.maximum(n - 1, 0))
                s_p = jnp.where(prev_ok & (n > 0), _dot_nt(q, k_ref[0, prows, :].astype(BF16)), NEG)
                m = jnp.max(jnp.maximum(s_c, s_p), axis=1, keepdims=True)
            else:
                m = jnp.max(s_c, axis=1, keepdims=True)
            ext = _dot(jnp.exp(s_c - m).astype(BF16), _with_ones(v_ref[0, rows, :]))
            if nb > 1:
                ext = ext + _dot(jnp.exp(s_p - m).astype(BF16), _with_ones(v_ref[0, prows, :]))
            return rows, m, ext[:, :HEAD_DIM], ext[:, HEAD_DIM:]

        def block(idx, carry, g=g, attend=attend):
            rows, m, acc, den = attend(idx)
            if g == first:
                acc_scr[rows, :] = acc
                m_scr[rows, :] = jnp.broadcast_to(m, wide)
                l_scr[rows, :] = den
            else:
                m_old = m_scr[rows, :]
                m_new = jnp.maximum(m_old, m)
                a_old, a_blk = jnp.exp(m_old - m_new), jnp.exp(m - m_new)
                acc_scr[rows, :] = a_old * acc_scr[rows, :] + a_blk * acc
                l_scr[rows, :] = a_old * l_scr[rows, :] + a_blk * den
                m_scr[rows, :] = m_new
            return carry

        lax.fori_loop(0, dil * nb, block, 0, unroll=A_UNROLL)
    o_ref[0] = (acc_scr[...] / l_scr[...]).astype(o_ref.dtype)


def a_attention_prompt(z, batch, seq):
    nc = z.shape[1]
    d = nc // (3 * len(A_PATTERNS))
    n_heads = d // HEAD_DIM
    assert all(win // dil == A_BLOCK and seq % (dil * A_BLOCK) == 0 for win, dil in A_PATTERNS)
    zv = z.reshape(batch, seq, nc)
    in_specs = [pl.BlockSpec((1, seq, HEAD_DIM), functools.partial(lambda b, h, c: (b, 0, c * n_heads + h), c=c))
                for c in range(3 * len(A_PATTERNS))]
    out = pl.pallas_call(
        functools.partial(_a_prompt_kernel, seq=seq),
        out_shape=jax.ShapeDtypeStruct((batch, seq, d), BF16),
        grid=(batch, n_heads),
        in_specs=in_specs,
        out_specs=pl.BlockSpec((1, seq, HEAD_DIM), lambda b, h: (b, 0, h)),
        scratch_shapes=[pltpu.VMEM((seq, HEAD_DIM), F32)] * 3,
        compiler_params=_cparams(2),
        name="a_attn_prompt",
    )(*([zv] * (3 * len(A_PATTERNS))))
    return out.reshape(batch * seq, d)


def _a_sample_kernel(z_ref, *refs, n_heads, d):
    cache_refs, o_ref = refs[:-1], refs[-1]
    scale = HEAD_DIM ** -0.5
    for h in range(n_heads):
        ms, dens, accs = [], [], []
        for g in range(len(A_PATTERNS)):
            kv_ref = cache_refs[g]
            c0 = g * 3 * d + h * HEAD_DIM
            q = z_ref[0, :, c0:c0 + HEAD_DIM] * scale
            k_new = z_ref[0, :, c0 + d:c0 + d + HEAD_DIM]
            v_new = z_ref[0, :, c0 + 2 * d:c0 + 2 * d + HEAD_DIM]
            q8 = jnp.broadcast_to(q, (8, HEAD_DIM))
            s = _dot_nt(q8.astype(BF16), kv_ref[:, 0, h, :].astype(BF16))
            s_new = jnp.sum(q8 * k_new, axis=1, keepdims=True)
            m = jnp.maximum(jnp.max(s, axis=1, keepdims=True), s_new)
            p = jnp.exp(s - m)
            p_new = jnp.exp(s_new - m)
            dens.append(jnp.sum(p, axis=1, keepdims=True) + p_new)
            accs.append(_dot(p.astype(BF16), kv_ref[:, 1, h, :].astype(BF16)) + p_new * v_new)
            ms.append(m)
        mt = jnp.maximum(jnp.maximum(ms[0], ms[1]), ms[2])
        es = [jnp.exp(mm - mt) for mm in ms]
        tot = es[0] * dens[0] + es[1] * dens[1] + es[2] * dens[2]
        num = es[0] * accs[0] + es[1] * accs[1] + es[2] * accs[2]
        o_ref[0, :, h * HEAD_DIM:(h + 1) * HEAD_DIM] = num / tot


def a_attention_sample(z, caches, li, n_batch):
    d = z.shape[1] // 9
    n_heads = d // HEAD_DIM
    in_specs = [pl.BlockSpec((1, 1, z.shape[1]), lambda b: (b, 0, 0))]
    args = [z[:, None, :]]
    for (win, dil), cache in zip(A_PATTERNS, caches):
        lb = cache.shape[2]
        assert lb == win and (win // dil) == A_BLOCK
        in_specs.append(pl.BlockSpec((None, None, A_BLOCK, None, 2, n_heads, HEAD_DIM),
                                     lambda b: (li, b, 0, 0, 0, 0, 0)))
        args.append(cache.reshape(cache.shape[0], n_batch, lb // dil, dil, 2, n_heads, HEAD_DIM))
    out = pl.pallas_call(
        functools.partial(_a_sample_kernel, n_heads=d // HEAD_DIM, d=d),
        out_shape=jax.ShapeDtypeStruct((n_batch, 8, d), F32),
        grid=(n_batch,),
        in_specs=in_specs,
        out_specs=pl.BlockSpec((1, 8, d), lambda b: (b, 0, 0)),
        compiler_params=_cparams(1),
        name="a_attn_sample",
    )(*args)
    return out[:, 0]


def _b_in_kernel(x_ref, g_ref, wb_ref, wc_ref, wx_ref, bg_ref, u_ref, h_scr):
    @pl.when(pl.program_id(1) == 0)
    def _():
        h_scr[...] = _rms(x_ref[...], g_ref[...]).astype(BF16)

    h = h_scr[...]
    bg_ref[...] = _dot(h, wb_ref[...].astype(BF16))
    u_ref[...] = _dot(h, wc_ref[...].astype(BF16)) * _dot(h, wx_ref[...].astype(BF16))


def b_in(x, gain, gi, w, li, *, tn, name):
    m, k = x.shape
    d = w.shape[-1] // 3
    tm = _row_tile(m)
    nt = d // tn
    out_spec = pl.BlockSpec((tm, tn), lambda i, j: (i, j))
    return pl.pallas_call(
        _b_in_kernel,
        out_shape=[jax.ShapeDtypeStruct((m, d), F32)] * 2,
        grid=(m // tm, nt),
        in_specs=[
            pl.BlockSpec((tm, k), lambda i, j: (i, 0)),
            pl.BlockSpec((None, 1, k), lambda i, j: (gi, 0, 0)),
            pl.BlockSpec((None, k, tn), lambda i, j: (li, 0, j)),
            pl.BlockSpec((None, k, tn), lambda i, j: (li, 0, j + nt)),
            pl.BlockSpec((None, k, tn), lambda i, j: (li, 0, j + 2 * nt)),
        ],
        out_specs=[out_spec, out_spec],
        scratch_shapes=[pltpu.VMEM((tm, k), BF16)],
        compiler_params=_cparams(2),
        name=name,
    )(x, gain.reshape(gain.shape[0], 1, k), w, w, w)


def _b_out_prompt_kernel(u_ref, up_ref, bg_ref, cw_ref, w_ref, r_ref, o_ref, ext_scr, a_scr, *, tiles_per_seq):
    i = pl.program_id(0)
    tm = u_ref.shape[0]

    @pl.when(pl.program_id(1) == 0)
    def _():
        ext_scr[0:8, :] = jnp.where(i % tiles_per_seq == 0, 0.0, up_ref[...])
        ext_scr[8:, :] = u_ref[...]
        y = (cw_ref[0:1, :] * ext_scr[pl.ds(6, tm), :] + cw_ref[1:2, :] * ext_scr[pl.ds(7, tm), :]
             + cw_ref[2:3, :] * ext_scr[pl.ds(8, tm), :])
        a_scr[...] = (bg_ref[...] * y).astype(BF16)

    o_ref[...] = r_ref[...] + _dot(a_scr[...], w_ref[...].astype(BF16))


def b_out_prompt(u, bg, conv_w, w, li, res, seq, *, tn, name):
    m, d = u.shape
    tm = 512
    assert seq % tm == 0
    return pl.pallas_call(
        functools.partial(_b_out_prompt_kernel, tiles_per_seq=seq // tm),
        out_shape=jax.ShapeDtypeStruct((m, d), F32),
        grid=(m // tm, d // tn),
        in_specs=[
            pl.BlockSpec((tm, d), lambda i, j: (i, 0)),
            pl.BlockSpec((8, d), lambda i, j: (jnp.maximum(i * (tm // 8) - 1, 0), 0)),
            pl.BlockSpec((tm, d), lambda i, j: (i, 0)),
            pl.BlockSpec((None, CONV_W, d), lambda i, j: (li, 0, 0)),
            pl.BlockSpec((None, d, tn), lambda i, j: (li, 0, j)),
            pl.BlockSpec((tm, tn), lambda i, j: (i, j)),
        ],
        out_specs=pl.BlockSpec((tm, tn), lambda i, j: (i, j)),
        scratch_shapes=[pltpu.VMEM((tm + 8, d), F32), pltpu.VMEM((tm, d), BF16)],
        compiler_params=_cparams(2),
        name=name,
    )(u, u, bg, conv_w, w, res)


def _b_out_sample_kernel(u_ref, um1_ref, um2_ref, bg_ref, cw_ref, w_ref, r_ref, o_ref):
    y = cw_ref[0:1, :] * um2_ref[...] + cw_ref[1:2, :] * um1_ref[...] + cw_ref[2:3, :] * u_ref[...]
    o_ref[...] = r_ref[...] + _dot((bg_ref[...] * y).astype(BF16), w_ref[...].astype(BF16))


def b_out_sample(u, um1, um2, bg, conv_w, w, li, res, *, tn, name):
    m, d = u.shape
    full = pl.BlockSpec((m, d), lambda j: (0, 0))
    return pl.pallas_call(
        _b_out_sample_kernel,
        out_shape=jax.ShapeDtypeStruct((m, d), F32),
        grid=(d // tn,),
        in_specs=[full, full, full, full,
                  pl.BlockSpec((None, CONV_W, d), lambda j: (li, 0, 0)),
                  pl.BlockSpec((None, d, tn), lambda j: (li, 0, j)),
                  pl.BlockSpec((m, tn), lambda j: (0, j))],
        out_specs=pl.BlockSpec((m, tn), lambda j: (0, j)),
        compiler_params=_cparams(1),
        name=name,
    )(u, um1, um2, bg, conv_w, w, res)


def _d_prompt_kernel(x_ref, xp_ref, g_ref, w_ref, sc_ref, r_ref, o_ref, ht_ref, ext_scr, *, tiles_per_seq):
    i = pl.program_id(0)
    j = pl.program_id(1)
    tm = x_ref.shape[0]
    halo = D_HIST + 1
    gw = w_ref.shape[0]

    @pl.when(j == 0)
    def _():
        ext_scr[0:halo, :] = jnp.where(i % tiles_per_seq == 0, 0.0, _rms(xp_ref[...], g_ref[...]))
        ext_scr[halo:, :] = _rms(x_ref[...], g_ref[...])
        ht_ref[0] = ext_scr[pl.ds(tm, halo), :]

    pos = (i % tiles_per_seq) * tm + lax.broadcasted_iota(I32, (tm, 1), 0)
    for g, win in enumerate(D_POOLS):
        @pl.when(j == g)
        def _(g=g, win=win):
            cols = slice(g * gw, (g + 1) * gw)
            tot = ext_scr[pl.ds(halo, tm), cols]
            h = tot
            for back in range(1, win):
                tot = tot + ext_scr[pl.ds(halo - back, tm), cols]
            count = jnp.minimum(pos + 1, win).astype(F32)
            pooled = tot / count - h
            o_ref[...] = r_ref[...] + _dot(pooled.astype(BF16), w_ref[...].astype(BF16)) * sc_ref[...]


def d_mixer_prompt(x, gain, li_norm, w_group, scale, li, seq, *, name):
    m, d = x.shape
    n_groups, gw = w_group.shape[1], w_group.shape[2]
    tm = 512
    halo = D_HIST + 1
    assert seq % tm == 0 and n_groups == len(D_POOLS)
    tps = seq // tm
    return pl.pallas_call(
        functools.partial(_d_prompt_kernel, tiles_per_seq=tps),
        out_shape=[jax.ShapeDtypeStruct((m, d), F32), jax.ShapeDtypeStruct((m // seq, halo, d), F32)],
        grid=(m // tm, n_groups),
        in_specs=[
            pl.BlockSpec((tm, d), lambda i, j: (i, 0)),
            pl.BlockSpec((halo, d), lambda i, j: (jnp.maximum(i * (tm // halo) - 1, 0), 0)),
            pl.BlockSpec((None, 1, d), lambda i, j: (li_norm, 0, 0)),
            pl.BlockSpec((None, None, gw, gw), lambda i, j: (li, j, 0, 0)),
            pl.BlockSpec((None, 1, gw), lambda i, j: (li, 0, j)),
            pl.BlockSpec((tm, gw), lambda i, j: (i, j)),
        ],
        out_specs=[pl.BlockSpec((tm, gw), lambda i, j: (i, j)),
                   pl.BlockSpec((1, halo, d), lambda i, j: (i // tps, 0, 0))],
        scratch_shapes=[pltpu.VMEM((tm + halo, d), F32)],
        compiler_params=_cparams(2),
        name=name,
    )(x, x, gain.reshape(gain.shape[0], 1, d), w_group, scale.reshape(scale.shape[0], 1, d), x)


def _d_sample_kernel(x_ref, hist_ref, g_ref, w_ref, sc_ref, o_ref, h_ref):
    h = _rms(x_ref[...], g_ref[...])
    h_ref[...] = h
    gw = w_ref.shape[1]
    for g, win in enumerate(D_POOLS):
        cols = slice(g * gw, (g + 1) * gw)
        tot = h[:, cols]
        for back in range(1, win):
            tot = tot + hist_ref[D_HIST - back, :, cols]
        pooled = tot / float(win) - h[:, cols]
        o_ref[:, cols] = x_ref[:, cols] + _dot(pooled.astype(BF16), w_ref[g].astype(BF16)) * sc_ref[:, cols]


def d_mixer_sample(x, hist_t, gain, li_norm, w_group, scale, li, *, name):
    m, d = x.shape
    n_groups, gw = w_group.shape[1], w_group.shape[2]
    assert hist_t.shape[0] == D_HIST
    return pl.pallas_call(
        _d_sample_kernel,
        out_shape=[jax.ShapeDtypeStruct((m, d), F32)] * 2,
        grid=(1,),
        in_specs=[
            pl.BlockSpec((m, d), lambda i: (0, 0)),
            pl.BlockSpec(hist_t.shape, lambda i: (0, 0, 0)),
            pl.BlockSpec((None, 1, d), lambda i: (li_norm, 0, 0)),
            pl.BlockSpec((None, n_groups, gw, gw), lambda i: (li, 0, 0, 0)),
            pl.BlockSpec((None, 1, d), lambda i: (li, 0, 0)),
        ],
        out_specs=[pl.BlockSpec((m, d), lambda i: (0, 0))] * 2,
        compiler_params=_cparams(1),
        name=name,
    )(x, hist_t, gain.reshape(gain.shape[0], 1, d), w_group, scale.reshape(scale.shape[0], 1, d))


def _cmp_fs_kernel(pt_ref, page_ref, w_ref, o_ref, ring_scr):
    del pt_ref
    slot = pl.program_id(1) % PAGES_PER_GROUP
    rows = page_ref.shape[1]
    for ck in range(2 * C_KV_HEADS):
        ring_scr[ck, pl.ds(pl.multiple_of(slot * rows, rows), rows), :] = page_ref[0, :, ck * HEAD_DIM:(ck + 1) * HEAD_DIM]

    @pl.when(slot == PAGES_PER_GROUP - 1)
    def _():
        n_chunks = PAGES_PER_GROUP * rows // C_CMP_STRIDE
        for ck in range(2 * C_KV_HEADS):
            c = ck // C_KV_HEADS
            acc = jnp.zeros((n_chunks, w_ref.shape[-1]), F32)
            for pp in range(C_CMP_STRIDE // 2):
                lhs = jnp.concatenate([ring_scr[ck, pl.ds(2 * pp + i, n_chunks, stride=C_CMP_STRIDE), :]
                                       for i in range(2)], axis=1)
                acc = acc + _dot(lhs.astype(BF16), w_ref[c, pp])
            o_ref[0, ck] = acc


def cmp_first_second(pool, page_ids, col_block, w_cat, n_batch, n_pages):
    rows = pool.shape[1]
    width = 2 * C_KV_HEADS * HEAD_DIM
    assert n_pages % PAGES_PER_GROUP == 0 and rows % C_CMP_STRIDE == 0
    chunks_per_group = PAGES_PER_GROUP * rows // C_CMP_STRIDE
    n_out = w_cat.shape[-1]
    return pl.pallas_call(
        _cmp_fs_kernel,
        out_shape=jax.ShapeDtypeStruct((n_batch, 2 * C_KV_HEADS, n_pages * rows // C_CMP_STRIDE, n_out), F32),
        grid_spec=pltpu.PrefetchScalarGridSpec(
            num_scalar_prefetch=1,
            grid=(n_batch, n_pages),
            in_specs=[
                pl.BlockSpec((1, rows, width), lambda b, pg, pt: (pt[b * n_pages + pg], 0, col_block)),
                pl.BlockSpec(w_cat.shape, lambda b, pg, pt: (0, 0, 0, 0)),
            ],
            out_specs=pl.BlockSpec((1, 2 * C_KV_HEADS, chunks_per_group, n_out),
                                   lambda b, pg, pt: (b, 0, pg // PAGES_PER_GROUP, 0)),
            scratch_shapes=[pltpu.VMEM((width // HEAD_DIM, PAGES_PER_GROUP * rows, HEAD_DIM), F32)],
        ),
        compiler_params=_cparams(2),
        name="c_cmp_first_second",
    )(page_ids, pool, w_cat)


def _cmp_fs_paged_kernel(pt_ref, *refs):
    del pt_ref
    page_refs, (w_ref, o_ref, acc_scr) = refs[:PAGES_PER_GROUP], refs[PAGES_PER_GROUP:]
    chunks_per_page = page_refs[0].shape[0] // C_CMP_STRIDE
    n_chunks = PAGES_PER_GROUP * chunks_per_page
    n_out = w_ref.shape[-1]

    def chunk_rows(p, c):
        return jnp.concatenate([pr[pl.ds(p, chunks_per_page, stride=C_CMP_STRIDE), c, :, :]
                                .reshape(chunks_per_page * C_KV_HEADS, HEAD_DIM) for pr in page_refs], axis=0)

    for c in range(2):
        acc = jnp.zeros((n_chunks * C_KV_HEADS, n_out), F32)
        for pp in range(C_CMP_STRIDE // 2):
            lhs = jnp.concatenate([chunk_rows(2 * pp, c), chunk_rows(2 * pp + 1, c)], axis=1)
            acc = acc + _dot(lhs.astype(BF16), w_ref[c, pp])
        for j in range(n_out // LANES):
            acc_scr[j] = acc[:, j * LANES:(j + 1) * LANES]
        for k in range(C_KV_HEADS):
            for j in range(n_out // LANES):
                o_ref[0, c * C_KV_HEADS + k, :, j * LANES:(j + 1) * LANES] = (
                    acc_scr[j, pl.ds(k, n_chunks, stride=C_KV_HEADS), :])


def cmp_first_second_paged(pool, li, page_table, w_cat):
    n_batch, n_pages = page_table.shape
    rows = pool.shape[2]
    assert n_pages % PAGES_PER_GROUP == 0 and rows % C_CMP_STRIDE == 0
    chunks_per_group = PAGES_PER_GROUP * rows // C_CMP_STRIDE
    n_out = w_cat.shape[-1]

    def page_spec(s):
        return pl.BlockSpec((None, None, rows, 2, C_KV_HEADS, HEAD_DIM),
                            lambda b, grp, pt: (li, pt[b * n_pages + grp * PAGES_PER_GROUP + s], 0, 0, 0, 0))

    return pl.pallas_call(
        _cmp_fs_paged_kernel,
        out_shape=jax.ShapeDtypeStruct((n_batch, 2 * C_KV_HEADS, n_pages * rows // C_CMP_STRIDE, n_out), F32),
        grid_spec=pltpu.PrefetchScalarGridSpec(
            num_scalar_prefetch=1,
            grid=(n_batch, n_pages // PAGES_PER_GROUP),
            in_specs=[page_spec(s) for s in range(PAGES_PER_GROUP)]
            + [pl.BlockSpec(w_cat.shape, lambda b, grp, pt: (0, 0, 0, 0))],
            out_specs=pl.BlockSpec((1, 2 * C_KV_HEADS, chunks_per_group, n_out), lambda b, grp, pt: (b, 0, grp, 0)),
            scratch_shapes=[pltpu.VMEM((n_out // LANES, chunks_per_group * C_KV_HEADS, LANES), F32)],
        ),
        compiler_params=_cparams(2),
        name="c_cmp_first_second_paged",
    )(page_table.reshape(-1), *([pool] * PAGES_PER_GROUP), w_cat)


def _cmp_mlp_kernel(fs_ref, pe_ref, w1_ref, w2_ref, kg_ref, o_ref):
    ck = pl.program_id(1)
    nch = fs_ref.shape[2]
    hid_w = w2_ref.shape[1]
    pe_term = _dot(pe_ref[0].astype(BF16), w1_ref[0].astype(BF16))[0:1, :]
    first = fs_ref[0, 0, :, 0:hid_w]
    second = fs_ref[0, 0, :, hid_w:2 * hid_w]
    nxt = pltpu.roll(second, nch - 1, 0)
    last = lax.broadcasted_iota(I32, (nch, 1), 0) == nch - 1
    pre = first + jnp.where(last, 0.0, nxt) + pe_term
    hid = pre * jax.nn.sigmoid(pre)
    out = _dot(hid.astype(BF16), w2_ref[0].astype(BF16))

    @pl.when(ck < C_KV_HEADS)
    def _():
        o_ref[0, 0] = _rms(out, kg_ref[...])

    @pl.when(ck >= C_KV_HEADS)
    def _():
        o_ref[0, 0] = out


def cmp_mlp(fs, pe8, w1_flat, w2, k_gain0):
    n_batch, n_ck, nch, _ = fs.shape
    return pl.pallas_call(
        _cmp_mlp_kernel,
        out_shape=jax.ShapeDtypeStruct((n_batch, n_ck, nch, HEAD_DIM), F32),
        grid=(n_batch, n_ck),
        in_specs=[
            pl.BlockSpec((1, 1, nch, fs.shape[-1]), lambda b, ck: (b, ck, 0, 0)),
            pl.BlockSpec((1,) + pe8.shape[1:], lambda b, ck: (ck // C_KV_HEADS, 0, 0)),
            pl.BlockSpec((1,) + w1_flat.shape[1:], lambda b, ck: (ck // C_KV_HEADS, 0, 0)),
            pl.BlockSpec((1,) + w2.shape[1:], lambda b, ck: (ck // C_KV_HEADS, 0, 0)),
            pl.BlockSpec((1, HEAD_DIM), lambda b, ck: (0, 0)),
        ],
        out_specs=pl.BlockSpec((1, 1, nch, HEAD_DIM), lambda b, ck: (b, ck, 0, 0)),
        compiler_params=_cparams(2),
        name="c_cmp_mlp",
    )(fs, pe8, w1_flat, w2, k_gain0)


def _masked_softmax_rows(s, ok):
    s = jnp.where(ok, s, NEG)
    m = jnp.max(s, axis=1, keepdims=True)
    e = jnp.where(ok, jnp.exp(s - m), 0.0)
    return e, jnp.maximum(jnp.sum(e, axis=1, keepdims=True), 1e-30)


def _nsa_prompt_kernel(q_ref, gate_ref, cos_ref, sin_ref, kc_ref, vc_ref, ks_ref, vs_ref, kw_ref, vw_ref,
                       ov_ref, ex_ref, o_ref, s_scr, mx_scr, ext_scr, *, nsel):
    qb = q_ref.shape[0]
    nch = kc_ref.shape[2]
    tk = s_scr.shape[2]
    t0 = pl.program_id(2) * qb

    def stacked_bias(ok):
        return jnp.concatenate([jnp.where(ok, 0.0, NEG)] * C_GROUP, axis=0)

    def t_rows(n):
        return t0 + lax.broadcasted_iota(I32, (qb, n), 0)

    qscale = HEAD_DIM ** -0.5 * LOG2_E
    qs = [q_ref[:, g * HEAD_DIM:(g + 1) * HEAD_DIM] for g in range(C_GROUP)]
    qn4 = jnp.concatenate([(q * qscale).astype(BF16) for q in qs], axis=0)
    qr4 = jnp.concatenate([(_rope(q, cos_ref[...], sin_ref[...]) * qscale).astype(BF16) for q in qs], axis=0)

    wk = C_WINDOW + qb
    start = pl.multiple_of(jnp.maximum(t0 - C_WINDOW, 0), qb)
    dist = t_rows(wk) - (start + lax.broadcasted_iota(I32, (qb, wk), 1))
    s = _dot_nt(qr4, kw_ref[0, pl.ds(start, wk), :].astype(BF16)) + stacked_bias((dist >= 0) & (dist < C_WINDOW))
    p = jnp.exp2(s - jnp.max(s, axis=1, keepdims=True)).astype(BF16)
    ext = _dot(p, _with_ones(vw_ref[0, pl.ds(start, wk), :]))
    o_win4 = ext[:, :HEAD_DIM] / ext[:, HEAD_DIM:]

    cmp_bias = stacked_bias(lax.broadcasted_iota(I32, (qb, nch), 1) * C_CMP_STRIDE + (C_CMP_LEN - 1) <= t_rows(nch))
    s = _dot_nt(qn4, kc_ref[0, 0].astype(BF16)) + cmp_bias
    e = jnp.where(cmp_bias < 0.0, 0.0, jnp.exp2(s - jnp.max(s, axis=1, keepdims=True)))
    ext = _dot(e.astype(BF16), _with_ones(vc_ref[0, 0]))
    den = jnp.maximum(ext[:, HEAD_DIM:], 1e-30)
    o_cmp4 = ext[:, :HEAD_DIM] / den
    p = e / den
    p_sum = p[0:qb]
    for g in range(1, C_GROUP):
        p_sum = p_sum + p[g * qb:(g + 1) * qb]

    p_hi = p_sum.astype(BF16)
    p_lo = (p_sum - p_hi.astype(F32)).astype(BF16)
    nr = -(-nsel // 8) * 8
    imp = (_dot_nt(ov_ref[...], p_hi) + _dot_nt(ov_ref[...], p_lo))[:nr]
    blk = lax.broadcasted_iota(I32, (nr, qb), 0)
    cur = (t0 + lax.broadcasted_iota(I32, (nr, qb), 1)) // C_SEL_BLOCK
    causal = (blk <= cur) & (blk < nsel)
    forced = (blk == 0) | (blk == cur) | (blk == cur - 1)
    imp = jnp.where(causal, jnp.where(forced, C_FORCE_SCORE, imp), NEG)
    rank = jnp.zeros((nr, qb), I32)
    for jp in range(nsel):
        c = jnp.broadcast_to(imp[jp:jp + 1, :], (nr, qb))
        beats = (c > imp) | ((c == imp) & (blk > jp))
        rank = rank + beats.astype(I32)
    sel_t = jnp.where((rank < C_SEL_TOPN) & causal, 1.0, 0.0)
    sel_t = jnp.concatenate([sel_t, jnp.zeros((LANES - nr, qb), F32)], axis=0)
    sel = sel_t.T.astype(BF16)

    n_tiles = (t0 + qb + tk - 1) // tk
    mx_scr[...] = jnp.full(mx_scr.shape, NEG, F32)

    def score_tile(kt, carry):
        k0 = pl.multiple_of(kt * tk, tk)
        sel_keys = _dot(sel, ex_ref[kt])
        ok = (sel_keys > 0.5) & (k0 + lax.broadcasted_iota(I32, (qb, tk), 1) <= t_rows(tk))
        s = _dot_nt(qr4, ks_ref[0, pl.ds(k0, tk), :].astype(BF16)) + stacked_bias(ok)
        s_scr[kt] = s
        mx = mx_scr[...]
        for c in range(tk // LANES):
            mx = jnp.maximum(mx, s[:, c * LANES:(c + 1) * LANES])
        mx_scr[...] = mx
        return carry

    lax.fori_loop(0, n_tiles, score_tile, 0)
    m_slc = jnp.max(mx_scr[...], axis=1, keepdims=True)

    ext_scr[...] = jnp.zeros(ext_scr.shape, F32)

    def value_tile(kt, carry):
        k0 = pl.multiple_of(kt * tk, tk)
        p = jnp.exp2(s_scr[kt] - m_slc).astype(BF16)
        ext_scr[...] += _dot(p, _with_ones(vs_ref[0, pl.ds(k0, tk), :]))
        return carry

    lax.fori_loop(0, n_tiles, value_tile, 0)
    o_slc4 = ext_scr[:, :HEAD_DIM] / ext_scr[:, HEAD_DIM:]

    for g in range(C_GROUP):
        rows = slice(g * qb, (g + 1) * qb)
        o = (gate_ref[:, 3 * g:3 * g + 1] * o_cmp4[rows] + gate_ref[:, 3 * g + 1:3 * g + 2] * o_slc4[rows]
             + gate_ref[:, 3 * g + 2:3 * g + 3] * o_win4[rows])
        o_ref[:, g * HEAD_DIM:(g + 1) * HEAD_DIM] = o.astype(o_ref.dtype)


def nsa_prompt(z, gates, cos, sin, kcv, overlap_t, expand, batch, seq):
    m, nc = z.shape
    d = C_KV_HEADS * C_GROUP * HEAD_DIM
    qb = C_QBLOCK
    nq = seq // qb
    nsel = -(-seq // C_SEL_BLOCK)
    tk = C_KEY_TILE
    assert nsel <= LANES and seq % qb == 0 and seq >= C_WINDOW + qb and seq % tk == 0
    assert expand.shape == (seq // tk, LANES, tk)
    nch = kcv.shape[2]
    assert nch == HEAD_DIM and qb == LANES and overlap_t.shape == (LANES, nch)
    zv = z.reshape(batch, seq, nc)
    cb = d // HEAD_DIM
    kvw = C_KV_HEADS

    def kv_spec(branch, kv):
        off = cb + (branch * 2 + kv) * kvw
        return pl.BlockSpec((1, seq, HEAD_DIM), lambda b, k, q: (b, 0, off + k))

    return pl.pallas_call(
        functools.partial(_nsa_prompt_kernel, nsel=nsel),
        out_shape=jax.ShapeDtypeStruct((m, d), BF16),
        scratch_shapes=[pltpu.VMEM((seq // tk, C_GROUP * qb, tk), F32),
                        pltpu.VMEM((C_GROUP * qb, LANES), F32),
                        pltpu.VMEM((C_GROUP * qb, 2 * HEAD_DIM), F32)],
        grid=(batch, C_KV_HEADS, nq),
        in_specs=[
            pl.BlockSpec((qb, C_GROUP * HEAD_DIM), lambda b, k, q: (b * nq + q, k)),
            pl.BlockSpec((qb, LANES), lambda b, k, q: (b * nq + q, k)),
            pl.BlockSpec((qb, HEAD_DIM), lambda b, k, q: (q, 0)),
            pl.BlockSpec((qb, HEAD_DIM), lambda b, k, q: (q, 0)),
            pl.BlockSpec((1, 1, nch, HEAD_DIM), lambda b, k, q: (b, k, 0, 0)),
            pl.BlockSpec((1, 1, nch, HEAD_DIM), lambda b, k, q: (b, C_KV_HEADS + k, 0, 0)),
            kv_spec(1, 0), kv_spec(1, 1), kv_spec(2, 0), kv_spec(2, 1),
            pl.BlockSpec(overlap_t.shape, lambda b, k, q: (0, 0)),
            pl.BlockSpec(expand.shape, lambda b, k, q: (0, 0, 0)),
        ],
        out_specs=pl.BlockSpec((qb, C_GROUP * HEAD_DIM), lambda b, k, q: (b * nq + q, k)),
        compiler_params=_cparams(3),
        name="c_nsa_prompt",
    )(z, gates, cos, sin, kcv, kcv, zv, zv, zv, zv, overlap_t, expand)


def _nsa_sample_cmp_kernel(q_ref, kc_ref, vc_ref, ov_ref, ocmp_ref, idx_ref, *, t, nsel):
    nch = kc_ref.shape[2]
    scale = HEAD_DIM ** -0.5
    q = (q_ref[0, 0] * scale).astype(BF16)
    cmp_ok = lax.broadcasted_iota(I32, (8, nch), 1) * C_CMP_STRIDE + (C_CMP_LEN - 1) <= t
    e, den = _masked_softmax_rows(_dot_nt(q, kc_ref[0, 0].astype(BF16)), cmp_ok)
    p = e / den
    ocmp_ref[0, 0] = _dot(p.astype(BF16), vc_ref[0, 0].astype(BF16))
    real = lax.broadcasted_iota(I32, (8, nch), 0) < C_GROUP
    p_sum = jnp.broadcast_to(jnp.sum(jnp.where(real, p, 0.0), axis=0, keepdims=True), (8, nch))
    p_hi = p_sum.astype(BF16)
    p_lo = (p_sum - p_hi.astype(F32)).astype(BF16)
    imp = _dot(p_hi, ov_ref[...]) + _dot(p_lo, ov_ref[...])
    width = imp.shape[1]
    lane = lax.broadcasted_iota(I32, (8, width), 1)
    cur = t // C_SEL_BLOCK
    causal = (lane <= cur) & (lane < nsel)
    forced = (lane == 0) | (lane == cur) | (lane == cur - 1)
    work = jnp.where(causal, jnp.where(forced, C_FORCE_SCORE, imp), NEG)
    lane_f = lane.astype(F32)
    out_lane = lax.broadcasted_iota(I32, (8, LANES), 1)
    idx = jnp.full((8, LANES), -1, I32)
    for r in range(C_SEL_TOPN):
        best = jnp.max(work, axis=1, keepdims=True)
        pick = jnp.min(jnp.where(work == best, lane_f, float(width)), axis=1, keepdims=True)
        found = jnp.where(best > 0.5 * NEG, pick, -1.0).astype(I32)
        idx = jnp.where(out_lane == r, found, idx)
        work = jnp.where(lane_f == pick, NEG, work)
    idx_ref[0, 0] = idx


def nsa_sample_cmp(q8, kcv, overlap, t, nsel):
    n_batch = q8.shape[0]
    nch = kcv.shape[2]
    blk = pl.BlockSpec((1, 1, 8, HEAD_DIM), lambda b, k: (b, k, 0, 0))
    return pl.pallas_call(
        functools.partial(_nsa_sample_cmp_kernel, t=t, nsel=nsel),
        out_shape=[jax.ShapeDtypeStruct((n_batch, C_KV_HEADS, 8, HEAD_DIM), F32),
                   jax.ShapeDtypeStruct((n_batch, C_KV_HEADS, 8, LANES), I32)],
        grid=(n_batch, C_KV_HEADS),
        in_specs=[
            blk,
            pl.BlockSpec((1, 1, nch, HEAD_DIM), lambda b, k: (b, k, 0, 0)),
            pl.BlockSpec((1, 1, nch, HEAD_DIM), lambda b, k: (b, C_KV_HEADS + k, 0, 0)),
            pl.BlockSpec(overlap.shape, lambda b, k: (0, 0)),
        ],
        out_specs=[blk, blk],
        compiler_params=_cparams(2),
        name="c_nsa_sample_cmp",
    )(q8, kcv, kcv, overlap)


def _nsa_sample_kernel(idx_ref, pt_ref, q_ref, cos_ref, sin_ref, *refs, t, n_past_blocks):
    del pt_ref
    slc_refs, (new_ref, win_ref, ocmp_ref, gate_ref, o_ref) = refs[:C_SEL_TOPN], refs[C_SEL_TOPN:]
    b, k = pl.program_id(0), pl.program_id(1)

    def head_rows(ref, kv):
        return ref[:, kv, pl.ds(k, 1), :][:, 0, :].astype(BF16)

    base = (b * C_KV_HEADS + k) * C_SEL_TOPN
    scale = HEAD_DIM ** -0.5
    qr_f = _rope(q_ref[0, 0], cos_ref[...], sin_ref[...]) * scale
    qr = qr_f.astype(BF16)

    nk = C_SEL_TOPN * C_SEL_BLOCK
    s = _dot_nt(qr, jnp.concatenate([head_rows(r, 0) for r in slc_refs], axis=0))
    lane = lax.broadcasted_iota(I32, (8, nk), 1)
    blk = jnp.full((8, nk), -1, I32)
    has_new = False
    for n in range(C_SEL_TOPN):
        blk_n = idx_ref[base + n]
        blk = jnp.where((lane >= n * C_SEL_BLOCK) & (lane < (n + 1) * C_SEL_BLOCK), blk_n, blk)
        has_new = has_new | (blk_n == n_past_blocks)
    ok = (blk >= 0) & (blk < n_past_blocks) & (blk * C_SEL_BLOCK + (lane & (C_SEL_BLOCK - 1)) <= t)
    k_new, v_new = new_ref[0, 0, 0:1, :], new_ref[0, 0, 1:2, :]
    s = jnp.where(ok, s, NEG)
    s_new = jnp.where(has_new, jnp.sum(qr_f * k_new, axis=1, keepdims=True), NEG)
    m = jnp.maximum(jnp.max(s, axis=1, keepdims=True), s_new)
    p = jnp.where(ok, jnp.exp(s - m), 0.0)
    p_new = jnp.where(has_new, jnp.exp(s_new - m), 0.0)
    den = jnp.maximum(jnp.sum(p, axis=1, keepdims=True) + p_new, 1e-30)
    vs = jnp.concatenate([head_rows(r, 1) for r in slc_refs], axis=0)
    o_slc = (_dot(p.astype(BF16), vs) + p_new * v_new) / den

    lw = win_ref.shape[0]
    kw_new, vw_new = new_ref[0, 0, 2:3, :], new_ref[0, 0, 3:4, :]
    dist = lw - lax.broadcasted_iota(I32, (8, lw), 1)
    w_ok = (dist >= 0) & (dist < C_WINDOW)
    s = jnp.where(w_ok, _dot_nt(qr, head_rows(win_ref, 0)), NEG)
    s_new = jnp.sum(qr_f * kw_new, axis=1, keepdims=True)
    m = jnp.maximum(jnp.max(s, axis=1, keepdims=True), s_new)
    p = jnp.where(w_ok, jnp.exp(s - m), 0.0)
    p_new = jnp.exp(s_new - m)
    den = jnp.sum(p, axis=1, keepdims=True) + p_new
    o_win = (_dot(p.astype(BF16), head_rows(win_ref, 1)) + p_new * vw_new) / den

    gate = gate_ref[0, 0]
    o_ref[0, 0] = gate[:, 0:1] * ocmp_ref[0, 0] + gate[:, 1:2] * o_slc + gate[:, 2:3] * o_win


def nsa_sample(top_idx, page_table, q8, cos, sin, pool_slc, win_buf, li, new_kv, ocmp, gate8, t, n_past_blocks):
    n_batch = q8.shape[0]
    n_pages = page_table.shape[1]
    n_layers, n_pool, page_rows = pool_slc.shape[:3]
    blocks_per_page = page_rows // C_SEL_BLOCK
    kv_dims = (2, C_KV_HEADS, HEAD_DIM)
    pool = pool_slc.reshape((n_layers, n_pool, blocks_per_page, C_SEL_BLOCK) + kv_dims)
    lw = win_buf.shape[2]

    def slc_spec(n):
        def index(b, k, idx, pt):
            blk = jnp.clip(idx[(b * C_KV_HEADS + k) * C_SEL_TOPN + n], 0, n_past_blocks - 1)
            return (li, pt[b * n_pages + blk // blocks_per_page], blk % blocks_per_page, 0, 0, 0, 0)
        return pl.BlockSpec((None, None, None, C_SEL_BLOCK) + kv_dims, index)

    win_spec = pl.BlockSpec((None, None, lw) + kv_dims, lambda b, k, idx, pt: (li, b, 0, 0, 0, 0))
    blk8 = pl.BlockSpec((1, 1, 8, HEAD_DIM), lambda b, k, idx, pt: (b, k, 0, 0))
    tab = pl.BlockSpec((8, HEAD_DIM), lambda b, k, idx, pt: (0, 0))
    return pl.pallas_call(
        functools.partial(_nsa_sample_kernel, t=t, n_past_blocks=n_past_blocks),
        out_shape=jax.ShapeDtypeStruct((n_batch, C_KV_HEADS, 8, HEAD_DIM), F32),
        grid_spec=pltpu.PrefetchScalarGridSpec(
            num_scalar_prefetch=2,
            grid=(n_batch, C_KV_HEADS),
            in_specs=[blk8, tab, tab] + [slc_spec(n) for n in range(C_SEL_TOPN)] + [blk8, win_spec, blk8, blk8],
            out_specs=blk8,
        ),
        compiler_params=_cparams(2),
        name="c_nsa_sample",
    )(top_idx.reshape(-1), page_table.reshape(-1), q8, cos, sin, *([pool] * C_SEL_TOPN), new_kv, win_buf, ocmp, gate8)


def _shift_append_kernel(cur_ref, nxt_ref, new_ref, out_ref):
    r = cur_ref.shape[0]
    out_ref[0:r - 1] = cur_ref[1:r]
    last = pl.program_id(1) == pl.num_programs(1) - 1

    @pl.when(last)
    def _():
        out_ref[r - 1:r] = new_ref[...]

    @pl.when(jnp.logical_not(last))
    def _():
        out_ref[r - 1:r] = nxt_ref[...]


def shift_append(cache, li, new, name):
    n_batch, length = cache.shape[1:3]
    tail = cache.shape[3:]
    assert new.shape == (n_batch, 1) + tail
    r = min(length, SHIFT_ROWS)
    assert length % r == 0
    zeros = (0,) * len(tail)
    return pl.pallas_call(
        _shift_append_kernel,
        out_shape=jax.ShapeDtypeStruct(cache.shape[1:], cache.dtype),
        grid=(n_batch, length // r),
        in_specs=[
            pl.BlockSpec((None, None, r) + tail, lambda b, i: (li, b, i) + zeros),
            pl.BlockSpec((None, None, 1) + tail, lambda b, i: (li, b, jnp.minimum((i + 1) * r, length - 1)) + zeros),
            pl.BlockSpec((None, 1) + tail, lambda b, i: (b, 0) + zeros),
        ],
        out_specs=pl.BlockSpec((None, r) + tail, lambda b, i: (b, i) + zeros),
        compiler_params=_cparams(2),
        name=name,
    )(cache, cache, new)


def _rope_tables(pos):
    half = HEAD_DIM // 2
    inv = ROPE_THETA ** (-(jnp.arange(half, dtype=F32) * 2.0 / HEAD_DIM))
    ang = pos.astype(F32)[:, None] * inv[None, :]
    cos, sin = jnp.cos(ang), jnp.sin(ang)
    return jnp.concatenate([cos, cos], axis=1), jnp.concatenate([-sin, sin], axis=1)


def _overlap_matrix(nch, nsel, width):
    cmp_start = jnp.arange(nch) * C_CMP_STRIDE
    sel_start = jnp.arange(width) * C_SEL_BLOCK
    ov = ((cmp_start[:, None] < sel_start[None, :] + C_SEL_BLOCK)
          & (cmp_start[:, None] + C_CMP_LEN > sel_start[None, :])
          & (jnp.arange(width)[None, :] < nsel))
    return ov.astype(BF16)


def _c_weights(prm, li, n_heads):
    d = n_heads * HEAD_DIM
    nkv = 3 * 2 * C_KV_HEADS * HEAD_DIM
    k_gain = prm['c_k_norm'][li]
    ones = jnp.ones((C_KV_HEADS * HEAD_DIM,), F32)
    head_gain = jnp.concatenate([
        jnp.tile(prm['c_q_norm'][li], n_heads), ones, ones,
        jnp.tile(k_gain[1], C_KV_HEADS), ones, jnp.tile(k_gain[2], C_KV_HEADS), ones])[None, :]
    wg = prm['w_c_in'][li][:, d + nkv:].reshape(-1, C_KV_HEADS, C_GROUP * 3)
    wg = jnp.pad(wg, ((0, 0), (0, 0), (0, LANES - C_GROUP * 3))).reshape(1, -1, C_KV_HEADS * LANES)
    w1 = prm['c_cmp_w1'][li]
    w_cat = jnp.concatenate([w1[:, :C_CMP_STRIDE], w1[:, C_CMP_STRIDE:]], axis=-1).astype(BF16)
    w_cat = w_cat.reshape(2, C_CMP_STRIDE // 2, 2 * HEAD_DIM, w_cat.shape[-1])
    pe8 = jnp.pad(prm['c_cmp_pe'][li].reshape(2, 1, -1), ((0, 0), (0, 7), (0, 0)))
    w1_flat = w1.reshape(2, C_CMP_LEN * HEAD_DIM, -1)
    return head_gain, wg, w_cat, pe8, w1_flat, prm['c_cmp_w2'][li], k_gain[0][None, :]


TN = 512
TN_WIDE_K = 256
C_MODES = (NORM,) * 4 + (RAW, RAW, NORM_ROPE, RAW, NORM_ROPE, RAW)
A_MODES = ((NORM_ROPE,) * 8 + (RAW,) * 4) * len(A_PATTERNS)


def _pad_rows(a, rows):
    return jnp.pad(a, ((0, rows - a.shape[0]),) + ((0, 0),) * (a.ndim - 1))


def _a_head_gain(prm, li, n_heads):
    ones = jnp.ones((n_heads * HEAD_DIM,), F32)
    parts = []
    for g in range(len(A_PATTERNS)):
        parts += [jnp.tile(prm['a_q_norm'][li, g], n_heads), jnp.tile(prm['a_k_norm'][li, g], n_heads), ones]
    return jnp.concatenate(parts)[None, :]


def _channel_mixer(x, p, prm, i, tag):
    act = swiglu_in(x, prm['ffn_norm'], prm['w_ffn_in'], i, tn=TN, name=f"ffn_in_{tag}")
    x = linear_residual(act, prm['w_ffn_out'], i, x, tn=TN_WIDE_K, name=f"ffn_out_{tag}")
    return ple(x, prm['ple_norm'], p, prm['w_ple_proj'], prm['w_ple_gate'], i, tn=TN, name=f"ple_{tag}")


def _run_prompt(x3, p, prm):
    batch, seq, d = x3.shape
    n_heads = d // HEAD_DIM
    x = x3.reshape(batch * seq, d)
    cos, sin = _rope_tables(jnp.arange(seq))
    new = {}
    depth = prm['attn_norm'].shape[0]
    for i in range(depth):
        kind, li = i % 4, i // 4
        if kind == 0:
            z = norm_linear(x, prm['attn_norm'], i, prm['w_a_qkv'], li, col0=0, modes=A_MODES,
                            head_gain=_a_head_gain(prm, li, n_heads), cos=cos, sin=sin, tn=TN, name="a_qkv_prompt")
            x = linear_residual(a_attention_prompt(z, batch, seq), prm['w_a_out'], li, x, tn=TN, name="a_out_prompt")
            z3 = z.reshape(batch, seq, -1)
            for g, (win, dil) in enumerate(A_PATTERNS):
                kv = z3[:, seq - min(win, seq):, (3 * g + 1) * d:(3 * g + 3) * d]
                new.setdefault(f'a_w{g + 1}', []).append(kv.reshape(batch, -1, 2, n_heads, HEAD_DIM))
        elif kind == 1:
            bg, u = b_in(x, prm['attn_norm'], i, prm['w_b_in'], li, tn=TN_WIDE_K, name="b_in_prompt")
            x = b_out_prompt(u, bg, prm['b_conv'], prm['w_b_out'], li, x, seq, tn=TN, name="b_out_prompt")
            new.setdefault('b_conv', []).append(u.reshape(batch, seq, d)[:, seq - (CONV_W - 1):])
        elif kind == 2:
            head_gain, wg, w_cat, pe8, w1_flat, w2, k_gain0 = _c_weights(prm, li, n_heads)
            z = norm_linear(x, prm['attn_norm'], i, prm['w_c_in'], li, col0=0, modes=C_MODES,
                            head_gain=head_gain, cos=cos, sin=sin, tn=TN, name="c_in_prompt")
            gates = norm_linear(x, prm['attn_norm'], i, wg, 0, col0=0, modes=(SIGMOID,),
                                head_gain=head_gain[:, :TN], cos=cos, sin=sin, tn=TN, name="c_gate_prompt")
            n_pages = seq // LANES
            pool = z.reshape(batch * n_pages, LANES, z.shape[1])
            kv_width = 2 * C_KV_HEADS * HEAD_DIM
            fs = cmp_first_second(pool, jnp.arange(batch * n_pages, dtype=I32), d // kv_width, w_cat, batch, n_pages)
            kcv = cmp_mlp(fs, pe8, w1_flat, w2, k_gain0)
            nsel = -(-seq // C_SEL_BLOCK)
            overlap = _overlap_matrix(kcv.shape[2], nsel, LANES)
            key_blk = (jnp.arange(seq) // C_SEL_BLOCK).reshape(seq // C_KEY_TILE, 1, C_KEY_TILE)
            expand = (key_blk == jnp.arange(LANES)[None, :, None]).astype(BF16)
            o = nsa_prompt(z, gates, cos, sin, kcv, overlap.T, expand, batch, seq)
            x = linear_residual(o, prm['w_c_out'], li, x, tn=TN, name="c_out_prompt")
            z3 = z.reshape(batch, seq, -1)
            for br, nm in enumerate(('c_cmp', 'c_slc', 'c_win')):
                rows = min(C_WINDOW, seq) if nm == 'c_win' else seq
                kv = z3[:, seq - rows:, d + br * kv_width:d + (br + 1) * kv_width]
                new.setdefault(nm, []).append(kv.reshape(batch, rows, 2, C_KV_HEADS, HEAD_DIM))
        else:
            x, tail = d_mixer_prompt(x, prm['attn_norm'], i, prm['w_d_group'], prm['d_scale'], li, seq,
                                     name="d_mixer_prompt")
            new.setdefault('d_pool', []).append(tail[:, 1:])
        x = _channel_mixer(x, p, prm, i, "prompt")
    return x.reshape(batch, seq, d), {nm: jnp.stack(v, axis=0) for nm, v in new.items()}


def _run_sample(x3, p, prm, past, page_table):
    n_batch, n_new, d = x3.shape
    assert n_new == 1
    n_heads = d // HEAD_DIM
    rows = SUBLANES_BF16
    x = _pad_rows(x3.reshape(n_batch, d), rows)
    past_len = page_table.shape[1] * past['c_cmp'].shape[2]
    cos, sin = _rope_tables(jnp.full((rows,), past_len))
    new = {}
    depth = prm['attn_norm'].shape[0]
    for i in range(depth):
        kind, li = i % 4, i // 4
        if kind == 0:
            z = norm_linear(x, prm['attn_norm'], i, prm['w_a_qkv'], li, col0=0, modes=A_MODES,
                            head_gain=_a_head_gain(prm, li, n_heads), cos=cos, sin=sin, tn=TN, name="a_qkv_sample")
            caches = [past[f'a_w{g + 1}'] for g in range(len(A_PATTERNS))]
            o = a_attention_sample(z, caches, li, n_batch)
            x = linear_residual(_pad_rows(o, rows), prm['w_a_out'], li, x, tn=TN, name="a_out_sample")
            for g, cache in enumerate(caches):
                kv = z[:n_batch, (3 * g + 1) * d:(3 * g + 3) * d].reshape(n_batch, 1, 2, n_heads, HEAD_DIM)
                new.setdefault(f'a_w{g + 1}', []).append(shift_append(cache, li, kv, f"a_w{g + 1}_shift"))
        elif kind == 1:
            bg, u = b_in(x, prm['attn_norm'], i, prm['w_b_in'], li, tn=TN_WIDE_K, name="b_in_sample")
            hist = past['b_conv'][li]
            x = b_out_sample(u, _pad_rows(hist[:, 1], rows), _pad_rows(hist[:, 0], rows), bg, prm['b_conv'],
                             prm['w_b_out'], li, x, tn=TN, name="b_out_sample")
            new.setdefault('b_conv', []).append(jnp.concatenate([hist[:, 1:], u[:n_batch, None]], axis=1))
        elif kind == 2:
            head_gain, wg, w_cat, pe8, w1_flat, w2, k_gain0 = _c_weights(prm, li, n_heads)
            z = norm_linear(x, prm['attn_norm'], i, prm['w_c_in'], li, col0=0, modes=C_MODES,
                            head_gain=head_gain, cos=cos, sin=sin, tn=TN, name="c_in_sample")
            gates = norm_linear(x, prm['attn_norm'], i, wg, 0, col0=0, modes=(SIGMOID,),
                                head_gain=head_gain[:, :TN], cos=cos, sin=sin, tn=TN, name="c_gate_sample")
            kv_width = 2 * C_KV_HEADS * HEAD_DIM
            kcv = cmp_mlp(cmp_first_second_paged(past['c_cmp'], li, page_table, w_cat), pe8, w1_flat, w2, k_gain0)
            nsel = -(-(past_len + 1) // C_SEL_BLOCK)
            width = -(-nsel // LANES) * LANES
            overlap = _overlap_matrix(kcv.shape[2], nsel, width)
            zb = z[:n_batch]
            q8 = jnp.pad(zb[:, :d].reshape(n_batch, C_KV_HEADS, C_GROUP, HEAD_DIM),
                         ((0, 0), (0, 0), (0, 8 - C_GROUP), (0, 0)))
            ocmp, idx = nsa_sample_cmp(q8, kcv, overlap, past_len, nsel)
            top_idx = idx[:, :, 0, :C_SEL_TOPN]
            kvh_w = C_KV_HEADS * HEAD_DIM
            new_rows = [zb[:, d + kv_width + j * kvh_w:d + kv_width + (j + 1) * kvh_w]
                        .reshape(n_batch, C_KV_HEADS, 1, HEAD_DIM) for j in range(4)]
            new_kv = jnp.pad(jnp.concatenate(new_rows, axis=2), ((0, 0), (0, 0), (0, 4), (0, 0)))
            gate8 = gates[:n_batch].reshape(n_batch, C_KV_HEADS, LANES)[:, :, :C_GROUP * 3]
            gate8 = jnp.pad(gate8.reshape(n_batch, C_KV_HEADS, C_GROUP, 3),
                            ((0, 0), (0, 0), (0, 8 - C_GROUP), (0, HEAD_DIM - 3)))
            o = nsa_sample(top_idx, page_table, q8, cos[:8], sin[:8], past['c_slc'], past['c_win'], li, new_kv,
                           ocmp, gate8, past_len, past_len // C_SEL_BLOCK)
            o = o[:, :, :C_GROUP].reshape(n_batch, d)
            x = linear_residual(_pad_rows(o, rows), prm['w_c_out'], li, x, tn=TN, name="c_out_sample")
            for br, nm in enumerate(('c_cmp', 'c_slc', 'c_win')):
                kv = zb[:, d + br * kv_width:d + (br + 1) * kv_width].reshape(n_batch, 1, 2, C_KV_HEADS, HEAD_DIM)
                if nm == 'c_win':
                    kv = shift_append(past['c_win'], li, kv, "c_win_shift")
                new.setdefault(nm, []).append(kv)
        else:
            hist = past['d_pool'][li]
            hist_t = jnp.pad(jnp.swapaxes(hist, 0, 1), ((0, 0), (0, rows - n_batch), (0, 0)))
            x, h = d_mixer_sample(x, hist_t, prm['attn_norm'], i, prm['w_d_group'], prm['d_scale'], li,
                                  name="d_mixer_sample")
            new.setdefault('d_pool', []).append(jnp.concatenate([hist[:, 1:], h[:n_batch, None]], axis=1))
        x = _channel_mixer(x, p, prm, i, "sample")
    return x[:n_batch].reshape(n_batch, 1, d), {nm: jnp.stack(v, axis=0) for nm, v in new.items()}


def kernel(x_prompt, x_sample, cache_a_w1, cache_a_w2, cache_a_w3, state_b_conv, cache_c_cmp, cache_c_slc, cache_c_win, state_d_pool, page_table, p_prompt, p_sample, attn_norm, ffn_norm, ple_norm, w_a_qkv, a_q_norm, a_k_norm, w_a_out, w_b_in, b_conv, w_b_out, w_c_in, c_q_norm, c_k_norm, c_cmp_pe, c_cmp_w1, c_cmp_w2, w_c_out, w_d_group, d_scale, w_ffn_in, w_ffn_out, w_ple_proj, w_ple_gate):
    prm = dict(attn_norm=attn_norm, ffn_norm=ffn_norm, ple_norm=ple_norm, w_a_qkv=w_a_qkv, a_q_norm=a_q_norm,
               a_k_norm=a_k_norm, w_a_out=w_a_out, w_b_in=w_b_in, b_conv=b_conv, w_b_out=w_b_out, w_c_in=w_c_in,
               c_q_norm=c_q_norm, c_k_norm=c_k_norm, c_cmp_pe=c_cmp_pe, c_cmp_w1=c_cmp_w1, c_cmp_w2=c_cmp_w2,
               w_c_out=w_c_out, w_d_group=w_d_group, d_scale=d_scale, w_ffn_in=w_ffn_in, w_ffn_out=w_ffn_out,
               w_ple_proj=w_ple_proj, w_ple_gate=w_ple_gate)
    past = dict(a_w1=cache_a_w1, a_w2=cache_a_w2, a_w3=cache_a_w3, b_conv=state_b_conv, c_cmp=cache_c_cmp,
                c_slc=cache_c_slc, c_win=cache_c_win, d_pool=state_d_pool)
    depth = attn_norm.shape[0]
    batch, seq, _ = x_prompt.shape
    n_dec = x_sample.shape[0]
    p_p = p_prompt.reshape(depth, batch * seq, -1)
    p_s = jnp.pad(p_sample.reshape(depth, n_dec, -1), ((0, 0), (0, SUBLANES_BF16 - n_dec), (0, 0)))
    y_prompt, sp = _run_prompt(x_prompt, p_p, prm)
    y_sample, ss = _run_sample(x_sample, p_s, prm, past, page_table)
    return (y_prompt, y_sample,
            sp['a_w1'], ss['a_w1'], sp['a_w2'], ss['a_w2'], sp['a_w3'], ss['a_w3'],
            sp['b_conv'], ss['b_conv'],
            sp['c_cmp'], ss['c_cmp'], sp['c_slc'], ss['c_slc'], sp['c_win'], ss['c_win'],
            sp['d_pool'], ss['d_pool'])
```

```python
import functools

import jax
import jax.numpy as jnp
from jax import lax
from jax.experimental import pallas as pl
from jax.experimental.pallas import tpu as pltpu

F32 = jnp.float32
BF16 = jnp.bfloat16
I32 = jnp.int32

HEAD_DIM = 128
LANES = 128
SUBLANES_BF16 = 16
RMS_EPS = 1e-6
ROPE_THETA = 10000.0
NEG = -1e30
V7X_VMEM_BYTES = 64 * 1024 * 1024
VMEM_LIMIT = V7X_VMEM_BYTES - 8 * 1024 * 1024

A_PATTERNS = ((128, 1), (512, 4), (2048, 16))
A_BLOCK = 128
A_UNROLL = 4
CONV_W = 3
C_KV_HEADS = 4
C_GROUP = 4
C_CMP_LEN = 32
C_CMP_STRIDE = 16
C_SEL_BLOCK = 64
C_SEL_TOPN = 16
C_WINDOW = 512
C_FORCE_SCORE = 1e4
C_QBLOCK = 128
C_KEY_TILE = 512
LOG2_E = 1.4426950408889634
D_POOLS = (2, 4, 8, 16)
D_HIST = max(D_POOLS) - 1
PAGES_PER_GROUP = 16
SHIFT_ROWS = 256

RAW, NORM, NORM_ROPE, SIGMOID = range(4)


def _cparams(n_axes):
    return pltpu.CompilerParams(dimension_semantics=("arbitrary",) * n_axes,
                                vmem_limit_bytes=VMEM_LIMIT)


def _row_tile(m):
    return 1024 if m % 1024 == 0 else m


def _dot(a, b):
    return jnp.dot(a, b, preferred_element_type=F32)


def _dot_nt(a, b):
    return lax.dot_general(a, b, (((1,), (1,)), ((), ())), preferred_element_type=F32)


def _with_ones(v):
    return jnp.concatenate([v.astype(BF16), jnp.ones(v.shape, BF16)], axis=1)


def _rms(x, g):
    return x * lax.rsqrt(jnp.mean(x * x, axis=-1, keepdims=True) + RMS_EPS) * g


def _rope(y, cos, sin):
    return y * cos + pltpu.roll(y, HEAD_DIM // 2, 1) * sin


def _tile_pred(j, tiles):
    pred = None
    lo = prev = tiles[0]
    runs = []
    for t in tiles[1:]:
        if t != prev + 1:
            runs.append((lo, prev))
            lo = t
        prev = t
    runs.append((lo, prev))
    for lo, hi in runs:
        p = (j == lo) if lo == hi else ((j >= lo) & (j <= hi))
        pred = p if pred is None else (pred | p)
    return pred


def _norm_linear_kernel(x_ref, g_ref, w_ref, hg_ref, cos_ref, sin_ref, o_ref, h_scr, *, modes):
    j = pl.program_id(1)

    @pl.when(j == 0)
    def _():
        h_scr[...] = _rms(x_ref[...], g_ref[...]).astype(BF16)

    acc = _dot(h_scr[...], w_ref[...].astype(BF16))
    tn = acc.shape[1]

    def emit(mode):
        if mode == RAW:
            o_ref[...] = acc.astype(o_ref.dtype)
        elif mode == SIGMOID:
            o_ref[...] = jax.nn.sigmoid(acc).astype(o_ref.dtype)
        else:
            for c in range(tn // HEAD_DIM):
                sl = slice(c * HEAD_DIM, (c + 1) * HEAD_DIM)
                y = _rms(acc[:, sl], hg_ref[:, sl])
                if mode == NORM_ROPE:
                    y = _rope(y, cos_ref[...], sin_ref[...])
                o_ref[:, sl] = y.astype(o_ref.dtype)

    kinds = sorted(set(modes))
    if len(kinds) == 1:
        emit(kinds[0])
    else:
        for m in kinds:
            tiles = [t for t, mm in enumerate(modes) if mm == m]
            pl.when(_tile_pred(j, tiles))(functools.partial(emit, m))


def norm_linear(x, gain, gi, w, wi, *, col0, modes, head_gain, cos, sin, tn, name):
    m, k = x.shape
    tm = _row_tile(m)
    nt = len(modes)
    assert col0 % tn == 0 and cos.shape[0] % tm == 0
    coff = col0 // tn
    n_pos_tiles = cos.shape[0] // tm
    return pl.pallas_call(
        functools.partial(_norm_linear_kernel, modes=tuple(modes)),
        out_shape=jax.ShapeDtypeStruct((m, nt * tn), F32),
        grid=(m // tm, nt),
        in_specs=[
            pl.BlockSpec((tm, k), lambda i, j: (i, 0)),
            pl.BlockSpec((None, 1, k), lambda i, j: (gi, 0, 0)),
            pl.BlockSpec((None, k, tn), lambda i, j: (wi, 0, j + coff)),
            pl.BlockSpec((1, tn), lambda i, j: (0, j)),
            pl.BlockSpec((tm, HEAD_DIM), lambda i, j: (i % n_pos_tiles, 0)),
            pl.BlockSpec((tm, HEAD_DIM), lambda i, j: (i % n_pos_tiles, 0)),
        ],
        out_specs=pl.BlockSpec((tm, tn), lambda i, j: (i, j)),
        scratch_shapes=[pltpu.VMEM((tm, k), BF16)],
        compiler_params=_cparams(2),
        name=name,
    )(x, gain.reshape(gain.shape[0], 1, k), w, head_gain, cos, sin)


def _linear_res_kernel(a_ref, w_ref, r_ref, o_ref):
    o_ref[...] = r_ref[...] + _dot(a_ref[...].astype(BF16), w_ref[...].astype(BF16))


def linear_residual(a, w, li, res, *, tn, name, tm=None):
    m, k = a.shape
    n = w.shape[-1]
    tm = _row_tile(m) if tm is None or m % tm else tm
    return pl.pallas_call(
        _linear_res_kernel,
        out_shape=jax.ShapeDtypeStruct((m, n), F32),
        grid=(m // tm, n // tn),
        in_specs=[
            pl.BlockSpec((tm, k), lambda i, j: (i, 0)),
            pl.BlockSpec((None, k, tn), lambda i, j: (li, 0, j)),
            pl.BlockSpec((tm, tn), lambda i, j: (i, j)),
        ],
        out_specs=pl.BlockSpec((tm, tn), lambda i, j: (i, j)),
        compiler_params=_cparams(2),
        name=name,
    )(a, w, res)


def _swiglu_in_kernel(x_ref, g_ref, wg_ref, wu_ref, o_ref, h_scr):
    @pl.when(pl.program_id(1) == 0)
    def _():
        h_scr[...] = _rms(x_ref[...], g_ref[...]).astype(BF16)

    h = h_scr[...]
    a = _dot(h, wg_ref[...].astype(BF16))
    b = _dot(h, wu_ref[...].astype(BF16))
    o_ref[...] = (a * jax.nn.sigmoid(a) * b).astype(o_ref.dtype)


def swiglu_in(x, gain, w, li, *, tn, name):
    m, k = x.shape
    f = w.shape[-1] // 2
    tm = _row_tile(m)
    nt = f // tn
    return pl.pallas_call(
        _swiglu_in_kernel,
        out_shape=jax.ShapeDtypeStruct((m, f), BF16),
        grid=(m // tm, nt),
        in_specs=[
            pl.BlockSpec((tm, k), lambda i, j: (i, 0)),
            pl.BlockSpec((None, 1, k), lambda i, j: (li, 0, 0)),
            pl.BlockSpec((None, k, tn), lambda i, j: (li, 0, j)),
            pl.BlockSpec((None, k, tn), lambda i, j: (li, 0, j + nt)),
        ],
        out_specs=pl.BlockSpec((tm, tn), lambda i, j: (i, j)),
        scratch_shapes=[pltpu.VMEM((tm, k), BF16)],
        compiler_params=_cparams(2),
        name=name,
    )(x, gain.reshape(gain.shape[0], 1, k), w, w)


def _ple_kernel(x_ref, g_ref, p_ref, wp_ref, wg_ref, o_ref, h_scr):
    j = pl.program_id(1)
    tn = o_ref.shape[1]

    @pl.when(j == 0)
    def _():
        h_scr[...] = _rms(x_ref[...], g_ref[...]).astype(BF16)

    gate = jax.nn.sigmoid(_dot(h_scr[...], wg_ref[...].astype(BF16)))
    proj = _dot(p_ref[...].astype(BF16), wp_ref[...].astype(BF16))
    o_ref[...] = x_ref[:, pl.ds(pl.multiple_of(j * tn, tn), tn)] + proj * gate


def ple(x, gain, p, wp, wg, li, *, tn, name):
    m, k = x.shape
    pd = p.shape[-1]
    tm = _row_tile(m)
    assert p.shape[1] == m
    return pl.pallas_call(
        _ple_kernel,
        out_shape=jax.ShapeDtypeStruct((m, k), F32),
        grid=(m // tm, k // tn),
        in_specs=[
            pl.BlockSpec((tm, k), lambda i, j: (i, 0)),
            pl.BlockSpec((None, 1, k), lambda i, j: (li, 0, 0)),
            pl.BlockSpec((None, tm, pd), lambda i, j: (li, i, 0)),
            pl.BlockSpec((None, pd, tn), lambda i, j: (li, 0, j)),
            pl.BlockSpec((None, k, tn), lambda i, j: (li, 0, j)),
        ],
        out_specs=pl.BlockSpec((tm, tn), lambda i, j: (i, j)),
        scratch_shapes=[pltpu.VMEM((tm, k), BF16)],
        compiler_params=_cparams(2),
        name=name,
    )(x, gain.reshape(gain.shape[0], 1, k), p, wp, wg)


def _a_prompt_kernel(*refs, seq):
    qkv_refs, o_ref = refs[:9], refs[9]
    acc_scr, m_scr, l_scr = refs[10:]
    scale = HEAD_DIM ** -0.5
    row = lax.broadcasted_iota(I32, (A_BLOCK, A_BLOCK), 0)
    col = lax.broadcasted_iota(I32, (A_BLOCK, A_BLOCK), 1)
    cur_ok = col <= row
    prev_ok = col >= row
    wide = (A_BLOCK, HEAD_DIM)
    first = len(A_PATTERNS) - 1
    for g, (win, dil) in reversed(list(enumerate(A_PATTERNS))):
        q_ref, k_ref, v_ref = qkv_refs[3 * g:3 * g + 3]
        nb = seq // dil // A_BLOCK

        def rows_of(r, n, dil=dil):
            start = r + dil * A_BLOCK * n
            if dil == 1:
                return pl.ds(pl.multiple_of(start, A_BLOCK), A_BLOCK)
            return pl.ds(start, A_BLOCK, stride=dil)

        def attend(idx, nb=nb, q_ref=q_ref, k_ref=k_ref, v_ref=v_ref, rows_of=rows_of):
            r, n = idx // nb, idx % nb
            rows = rows_of(r, n)
            q = (q_ref[0, rows, :] * scale).astype(BF16)
            s_c = jnp.where(cur_ok, _dot_nt(q, k_ref[0, rows, :].astype(BF16)), NEG)
            if nb > 1:
                prows = rows_of(r, jnp.maximum(n - 1, 0))
                s_p = jnp.where(prev_ok & (n > 0), _dot_nt(q, k_ref[0, prows, :].astype(BF16)), NEG)
                m = jnp.max(jnp.maximum(s_c, s_p), axis=1, keepdims=True)
            else:
                m = jnp.max(s_c, axis=1, keepdims=True)
            ext = _dot(jnp.exp(s_c - m).astype(BF16), _with_ones(v_ref[0, rows, :]))
            if nb > 1:
                ext = ext + _dot(jnp.exp(s_p - m).astype(BF16), _with_ones(v_ref[0, prows, :]))
            return rows, m, ext[:, :HEAD_DIM], ext[:, HEAD_DIM:]

        def block(idx, carry, g=g, attend=attend):
            rows, m, acc, den = attend(idx)
            if g == first:
                acc_scr[rows, :] = acc
                m_scr[rows, :] = jnp.broadcast_to(m, wide)
                l_scr[rows, :] = den
            else:
                m_old = m_scr[rows, :]
                m_new = jnp.maximum(m_old, m)
                a_old, a_blk = jnp.exp(m_old - m_new), jnp.exp(m - m_new)
                acc_scr[rows, :] = a_old * acc_scr[rows, :] + a_blk * acc
                l_scr[rows, :] = a_old * l_scr[rows, :] + a_blk * den
                m_scr[rows, :] = m_new
            return carry

        lax.fori_loop(0, dil * nb, block, 0, unroll=A_UNROLL)
    o_ref[0] = (acc_scr[...] / l_scr[...]).astype(o_ref.dtype)


def a_attention_prompt(z, batch, seq):
    nc = z.shape[1]
    d = nc // (3 * len(A_PATTERNS))
    n_heads = d // HEAD_DIM
    assert all(win // dil == A_BLOCK and seq % (dil * A_BLOCK) == 0 for win, dil in A_PATTERNS)
    zv = z.reshape(batch, seq, nc)
    in_specs = [pl.BlockSpec((1, seq, HEAD_DIM), functools.partial(lambda b, h, c: (b, 0, c * n_heads + h), c=c))
                for c in range(3 * len(A_PATTERNS))]
    out = pl.pallas_call(
        functools.partial(_a_prompt_kernel, seq=seq),
        out_shape=jax.ShapeDtypeStruct((batch, seq, d), BF16),
        grid=(batch, n_heads),
        in_specs=in_specs,
        out_specs=pl.BlockSpec((1, seq, HEAD_DIM), lambda b, h: (b, 0, h)),
        scratch_shapes=[pltpu.VMEM((seq, HEAD_DIM), F32)] * 3,
        compiler_params=_cparams(2),
        name="a_attn_prompt",
    )(*([zv] * (3 * len(A_PATTERNS))))
    return out.reshape(batch * seq, d)


def _a_sample_kernel(z_ref, *refs, n_heads, d):
    cache_refs, o_ref = refs[:-1], refs[-1]
    scale = HEAD_DIM ** -0.5
    for h in range(n_heads):
        ms, dens, accs = [], [], []
        for g in range(len(A_PATTERNS)):
            kv_ref = cache_refs[g]
            c0 = g * 3 * d + h * HEAD_DIM
            q = z_ref[0, :, c0:c0 + HEAD_DIM] * scale
            k_new = z_ref[0, :, c0 + d:c0 + d + HEAD_DIM]
            v_new = z_ref[0, :, c0 + 2 * d:c0 + 2 * d + HEAD_DIM]
            q8 = jnp.broadcast_to(q, (8, HEAD_DIM))
            s = _dot_nt(q8.astype(BF16), kv_ref[:, 0, h, :].astype(BF16))
            s_new = jnp.sum(q8 * k_new, axis=1, keepdims=True)
            m = jnp.maximum(jnp.max(s, axis=1, keepdims=True), s_new)
            p = jnp.exp(s - m)
            p_new = jnp.exp(s_new - m)
            dens.append(jnp.sum(p, axis=1, keepdims=True) + p_new)
            accs.append(_dot(p.astype(BF16), kv_ref[:, 1, h, :].astype(BF16)) + p_new * v_new)
            ms.append(m)
        mt = jnp.maximum(jnp.maximum(ms[0], ms[1]), ms[2])
        es = [jnp.exp(mm - mt) for mm in ms]
        tot = es[0] * dens[0] + es[1] * dens[1] + es[2] * dens[2]
        num = es[0] * accs[0] + es[1] * accs[1] + es[2] * accs[2]
        o_ref[0, :, h * HEAD_DIM:(h + 1) * HEAD_DIM] = num / tot


def a_attention_sample(z, caches, li, n_batch):
    d = z.shape[1] // 9
    n_heads = d // HEAD_DIM
    in_specs = [pl.BlockSpec((1, 1, z.shape[1]), lambda b: (b, 0, 0))]
    args = [z[:, None, :]]
    for (win, dil), cache in zip(A_PATTERNS, caches):
        lb = cache.shape[2]
        assert lb == win and (win // dil) == A_BLOCK
        in_specs.append(pl.BlockSpec((None, None, A_BLOCK, None, 2, n_heads, HEAD_DIM),
                                     lambda b: (li, b, 0, 0, 0, 0, 0)))
        args.append(cache.reshape(cache.shape[0], n_batch, lb // dil, dil, 2, n_heads, HEAD_DIM))
    out = pl.pallas_call(
        functools.partial(_a_sample_kernel, n_heads=d // HEAD_DIM, d=d),
        out_shape=jax.ShapeDtypeStruct((n_batch, 8, d), F32),
        grid=(n_batch,),
        in_specs=in_specs,
        out_specs=pl.BlockSpec((1, 8, d), lambda b: (b, 0, 0)),
        compiler_params=_cparams(1),
        name="a_attn_sample",
    )(*args)
    return out[:, 0]


def _b_in_kernel(x_ref, g_ref, wb_ref, wc_ref, wx_ref, bg_ref, u_ref, h_scr):
    @pl.when(pl.program_id(1) == 0)
    def _():
        h_scr[...] = _rms(x_ref[...], g_ref[...]).astype(BF16)

    h = h_scr[...]
    bg_ref[...] = _dot(h, wb_ref[...].astype(BF16))
    u_ref[...] = _dot(h, wc_ref[...].astype(BF16)) * _dot(h, wx_ref[...].astype(BF16))


def b_in(x, gain, gi, w, li, *, tn, name):
    m, k = x.shape
    d = w.shape[-1] // 3
    tm = _row_tile(m)
    nt = d // tn
    out_spec = pl.BlockSpec((tm, tn), lambda i, j: (i, j))
    return pl.pallas_call(
        _b_in_kernel,
        out_shape=[jax.ShapeDtypeStruct((m, d), F32)] * 2,
        grid=(m // tm, nt),
        in_specs=[
            pl.BlockSpec((tm, k), lambda i, j: (i, 0)),
            pl.BlockSpec((None, 1, k), lambda i, j: (gi, 0, 0)),
            pl.BlockSpec((None, k, tn), lambda i, j: (li, 0, j)),
            pl.BlockSpec((None, k, tn), lambda i, j: (li, 0, j + nt)),
            pl.BlockSpec((None, k, tn), lambda i, j: (li, 0, j + 2 * nt)),
        ],
        out_specs=[out_spec, out_spec],
        scratch_shapes=[pltpu.VMEM((tm, k), BF16)],
        compiler_params=_cparams(2),
        name=name,
    )(x, gain.reshape(gain.shape[0], 1, k), w, w, w)


def _b_out_prompt_kernel(u_ref, up_ref, bg_ref, cw_ref, w_ref, r_ref, o_ref, ext_scr, a_scr, *, tiles_per_seq):
    i = pl.program_id(0)
    tm = u_ref.shape[0]

    @pl.when(pl.program_id(1) == 0)
    def _():
        ext_scr[0:8, :] = jnp.where(i % tiles_per_seq == 0, 0.0, up_ref[...])
        ext_scr[8:, :] = u_ref[...]
        y = (cw_ref[0:1, :] * ext_scr[pl.ds(6, tm), :] + cw_ref[1:2, :] * ext_scr[pl.ds(7, tm), :]
             + cw_ref[2:3, :] * ext_scr[pl.ds(8, tm), :])
        a_scr[...] = (bg_ref[...] * y).astype(BF16)

    o_ref[...] = r_ref[...] + _dot(a_scr[...], w_ref[...].astype(BF16))


def b_out_prompt(u, bg, conv_w, w, li, res, seq, *, tn, name):
    m, d = u.shape
    tm = 512
    assert seq % tm == 0
    return pl.pallas_call(
        functools.partial(_b_out_prompt_kernel, tiles_per_seq=seq // tm),
        out_shape=jax.ShapeDtypeStruct((m, d), F32),
        grid=(m // tm, d // tn),
        in_specs=[
            pl.BlockSpec((tm, d), lambda i, j: (i, 0)),
            pl.BlockSpec((8, d), lambda i, j: (jnp.maximum(i * (tm // 8) - 1, 0), 0)),
            pl.BlockSpec((tm, d), lambda i, j: (i, 0)),
            pl.BlockSpec((None, CONV_W, d), lambda i, j: (li, 0, 0)),
            pl.BlockSpec((None, d, tn), lambda i, j: (li, 0, j)),
            pl.BlockSpec((tm, tn), lambda i, j: (i, j)),
        ],
        out_specs=pl.BlockSpec((tm, tn), lambda i, j: (i, j)),
        scratch_shapes=[pltpu.VMEM((tm + 8, d), F32), pltpu.VMEM((tm, d), BF16)],
        compiler_params=_cparams(2),
        name=name,
    )(u, u, bg, conv_w, w, res)


def _b_out_sample_kernel(u_ref, um1_ref, um2_ref, bg_ref, cw_ref, w_ref, r_ref, o_ref):
    y = cw_ref[0:1, :] * um2_ref[...] + cw_ref[1:2, :] * um1_ref[...] + cw_ref[2:3, :] * u_ref[...]
    o_ref[...] = r_ref[...] + _dot((bg_ref[...] * y).astype(BF16), w_ref[...].astype(BF16))


def b_out_sample(u, um1, um2, bg, conv_w, w, li, res, *, tn, name):
    m, d = u.shape
    full = pl.BlockSpec((m, d), lambda j: (0, 0))
    return pl.pallas_call(
        _b_out_sample_kernel,
        out_shape=jax.ShapeDtypeStruct((m, d), F32),
        grid=(d // tn,),
        in_specs=[full, full, full, full,
                  pl.BlockSpec((None, CONV_W, d), lambda j: (li, 0, 0)),
                  pl.BlockSpec((None, d, tn), lambda j: (li, 0, j)),
                  pl.BlockSpec((m, tn), lambda j: (0, j))],
        out_specs=pl.BlockSpec((m, tn), lambda j: (0, j)),
        compiler_params=_cparams(1),
        name=name,
    )(u, um1, um2, bg, conv_w, w, res)


def _d_prompt_kernel(x_ref, xp_ref, g_ref, w_ref, sc_ref, o_ref, ht_ref, ext_scr, *, tiles_per_seq):
    i = pl.program_id(0)
    j = pl.program_id(1)
    tm = x_ref.shape[0]
    halo = D_HIST + 1
    gw = w_ref.shape[0]

    @pl.when(j == 0)
    def _():
        ext_scr[0:halo, :] = jnp.where(i % tiles_per_seq == 0, 0.0, _rms(xp_ref[...], g_ref[...]))
        ext_scr[halo:, :] = _rms(x_ref[...], g_ref[...])
        ht_ref[0] = ext_scr[pl.ds(tm, halo), :]

    pos = (i % tiles_per_seq) * tm + lax.broadcasted_iota(I32, (tm, 1), 0)
    for g, win in enumerate(D_POOLS):
        @pl.when(j == g)
        def _(g=g, win=win):
            cols = slice(g * gw, (g + 1) * gw)
            tot = ext_scr[pl.ds(halo, tm), cols]
            h = tot
            for back in range(1, win):
                tot = tot + ext_scr[pl.ds(halo - back, tm), cols]
            count = jnp.minimum(pos + 1, win).astype(F32)
            pooled = tot / count - h
            o_ref[...] = x_ref[:, cols] + _dot(pooled.astype(BF16), w_ref[...].astype(BF16)) * sc_ref[...]


def d_mixer_prompt(x, gain, li_norm, w_group, scale, li, seq, *, name):
    m, d = x.shape
    n_groups, gw = w_group.shape[1], w_group.shape[2]
    tm = 512
    halo = D_HIST + 1
    assert seq % tm == 0 and n_groups == len(D_POOLS)
    tps = seq // tm
    return pl.pallas_call(
        functools.partial(_d_prompt_kernel, tiles_per_seq=tps),
        out_shape=[jax.ShapeDtypeStruct((m, d), F32), jax.ShapeDtypeStruct((m // seq, halo, d), F32)],
        grid=(m // tm, n_groups),
        in_specs=[
            pl.BlockSpec((tm, d), lambda i, j: (i, 0)),
            pl.BlockSpec((halo, d), lambda i, j: (jnp.maximum(i * (tm // halo) - 1, 0), 0)),
            pl.BlockSpec((None, 1, d), lambda i, j: (li_norm, 0, 0)),
            pl.BlockSpec((None, None, gw, gw), lambda i, j: (li, j, 0, 0)),
            pl.BlockSpec((None, 1, gw), lambda i, j: (li, 0, j)),
        ],
        out_specs=[pl.BlockSpec((tm, gw), lambda i, j: (i, j)),
                   pl.BlockSpec((1, halo, d), lambda i, j: (i // tps, 0, 0))],
        scratch_shapes=[pltpu.VMEM((tm + halo, d), F32)],
        compiler_params=_cparams(2),
        name=name,
    )(x, x, gain.reshape(gain.shape[0], 1, d), w_group, scale.reshape(scale.shape[0], 1, d))


def _d_sample_kernel(x_ref, hist_ref, g_ref, w_ref, sc_ref, o_ref, h_ref):
    h = _rms(x_ref[...], g_ref[...])
    h_ref[...] = h
    gw = w_ref.shape[1]
    for g, win in enumerate(D_POOLS):
        cols = slice(g * gw, (g + 1) * gw)
        tot = h[:, cols]
        for back in range(1, win):
            tot = tot + hist_ref[D_HIST - back, :, cols]
        pooled = tot / float(win) - h[:, cols]
        o_ref[:, cols] = x_ref[:, cols] + _dot(pooled.astype(BF16), w_ref[g].astype(BF16)) * sc_ref[:, cols]


def d_mixer_sample(x, hist_t, gain, li_norm, w_group, scale, li, *, name):
    m, d = x.shape
    n_groups, gw = w_group.shape[1], w_group.shape[2]
    assert hist_t.shape[0] == D_HIST
    return pl.pallas_call(
        _d_sample_kernel,
        out_shape=[jax.ShapeDtypeStruct((m, d), F32)] * 2,
        grid=(1,),
        in_specs=[
            pl.BlockSpec((m, d), lambda i: (0, 0)),
            pl.BlockSpec(hist_t.shape, lambda i: (0, 0, 0)),
            pl.BlockSpec((None, 1, d), lambda i: (li_norm, 0, 0)),
            pl.BlockSpec((None, n_groups, gw, gw), lambda i: (li, 0, 0, 0)),
            pl.BlockSpec((None, 1, d), lambda i: (li, 0, 0)),
        ],
        out_specs=[pl.BlockSpec((m, d), lambda i: (0, 0))] * 2,
        compiler_params=_cparams(1),
        name=name,
    )(x, hist_t, gain.reshape(gain.shape[0], 1, d), w_group, scale.reshape(scale.shape[0], 1, d))


def _cmp_fs_kernel(pt_ref, page_ref, w_ref, o_ref, ring_scr):
    del pt_ref
    slot = pl.program_id(1) % PAGES_PER_GROUP
    rows = page_ref.shape[1]
    for ck in range(2 * C_KV_HEADS):
        ring_scr[ck, pl.ds(pl.multiple_of(slot * rows, rows), rows), :] = page_ref[0, :, ck * HEAD_DIM:(ck + 1) * HEAD_DIM]

    @pl.when(slot == PAGES_PER_GROUP - 1)
    def _():
        n_chunks = PAGES_PER_GROUP * rows // C_CMP_STRIDE
        for ck in range(2 * C_KV_HEADS):
            c = ck // C_KV_HEADS
            acc = jnp.zeros((n_chunks, w_ref.shape[-1]), F32)
            for pp in range(C_CMP_STRIDE // 2):
                lhs = jnp.concatenate([ring_scr[ck, pl.ds(2 * pp + i, n_chunks, stride=C_CMP_STRIDE), :]
                                       for i in range(2)], axis=1)
                acc = acc + _dot(lhs.astype(BF16), w_ref[c, pp])
            o_ref[0, ck] = acc


def cmp_first_second(pool, page_ids, col_block, w_cat, n_batch, n_pages):
    rows = pool.shape[1]
    width = 2 * C_KV_HEADS * HEAD_DIM
    assert n_pages % PAGES_PER_GROUP == 0 and rows % C_CMP_STRIDE == 0
    chunks_per_group = PAGES_PER_GROUP * rows // C_CMP_STRIDE
    n_out = w_cat.shape[-1]
    return pl.pallas_call(
        _cmp_fs_kernel,
        out_shape=jax.ShapeDtypeStruct((n_batch, 2 * C_KV_HEADS, n_pages * rows // C_CMP_STRIDE, n_out), F32),
        grid_spec=pltpu.PrefetchScalarGridSpec(
            num_scalar_prefetch=1,
            grid=(n_batch, n_pages),
            in_specs=[
                pl.BlockSpec((1, rows, width), lambda b, pg, pt: (pt[b * n_pages + pg], 0, col_block)),
                pl.BlockSpec(w_cat.shape, lambda b, pg, pt: (0, 0, 0, 0)),
            ],
            out_specs=pl.BlockSpec((1, 2 * C_KV_HEADS, chunks_per_group, n_out),
                                   lambda b, pg, pt: (b, 0, pg // PAGES_PER_GROUP, 0)),
            scratch_shapes=[pltpu.VMEM((width // HEAD_DIM, PAGES_PER_GROUP * rows, HEAD_DIM), F32)],
        ),
        compiler_params=_cparams(2),
        name="c_cmp_first_second",
    )(page_ids, pool, w_cat)


def _cmp_fs_paged_kernel(pt_ref, *refs):
    del pt_ref
    page_refs, (w_ref, o_ref, acc_scr) = refs[:PAGES_PER_GROUP], refs[PAGES_PER_GROUP:]
    chunks_per_page = page_refs[0].shape[0] // C_CMP_STRIDE
    n_chunks = PAGES_PER_GROUP * chunks_per_page
    n_out = w_ref.shape[-1]

    def chunk_rows(p, c):
        return jnp.concatenate([pr[pl.ds(p, chunks_per_page, stride=C_CMP_STRIDE), c, :, :]
                                .reshape(chunks_per_page * C_KV_HEADS, HEAD_DIM) for pr in page_refs], axis=0)

    for c in range(2):
        acc = jnp.zeros((n_chunks * C_KV_HEADS, n_out), F32)
        for pp in range(C_CMP_STRIDE // 2):
            lhs = jnp.concatenate([chunk_rows(2 * pp, c), chunk_rows(2 * pp + 1, c)], axis=1)
            acc = acc + _dot(lhs.astype(BF16), w_ref[c, pp])
        for j in range(n_out // LANES):
            acc_scr[j] = acc[:, j * LANES:(j + 1) * LANES]
        for k in range(C_KV_HEADS):
            for j in range(n_out // LANES):
                o_ref[0, c * C_KV_HEADS + k, :, j * LANES:(j + 1) * LANES] = (
                    acc_scr[j, pl.ds(k, n_chunks, stride=C_KV_HEADS), :])


def cmp_first_second_paged(pool, li, page_table, w_cat):
    n_batch, n_pages = page_table.shape
    rows = pool.shape[2]
    assert n_pages % PAGES_PER_GROUP == 0 and rows % C_CMP_STRIDE == 0
    chunks_per_group = PAGES_PER_GROUP * rows // C_CMP_STRIDE
    n_out = w_cat.shape[-1]

    def page_spec(s):
        return pl.BlockSpec((None, None, rows, 2, C_KV_HEADS, HEAD_DIM),
                            lambda b, grp, pt: (li, pt[b * n_pages + grp * PAGES_PER_GROUP + s], 0, 0, 0, 0))

    return pl.pallas_call(
        _cmp_fs_paged_kernel,
        out_shape=jax.ShapeDtypeStruct((n_batch, 2 * C_KV_HEADS, n_pages * rows // C_CMP_STRIDE, n_out), F32),
        grid_spec=pltpu.PrefetchScalarGridSpec(
            num_scalar_prefetch=1,
            grid=(n_batch, n_pages // PAGES_PER_GROUP),
            in_specs=[page_spec(s) for s in range(PAGES_PER_GROUP)]
            + [pl.BlockSpec(w_cat.shape, lambda b, grp, pt: (0, 0, 0, 0))],
            out_specs=pl.BlockSpec((1, 2 * C_KV_HEADS, chunks_per_group, n_out), lambda b, grp, pt: (b, 0, grp, 0)),
            scratch_shapes=[pltpu.VMEM((n_out // LANES, chunks_per_group * C_KV_HEADS, LANES), F32)],
        ),
        compiler_params=_cparams(2),
        name="c_cmp_first_second_paged",
    )(page_table.reshape(-1), *([pool] * PAGES_PER_GROUP), w_cat)


def _cmp_pe_kernel(pe_ref, w1_ref, o_ref):
    o_ref[0] = _dot(pe_ref[0].astype(BF16), w1_ref[0].astype(BF16))


def cmp_pe_term(pe8, w1_flat):
    return pl.pallas_call(
        _cmp_pe_kernel,
        out_shape=jax.ShapeDtypeStruct((pe8.shape[0], pe8.shape[1], w1_flat.shape[-1]), F32),
        grid=(pe8.shape[0],),
        in_specs=[pl.BlockSpec((1,) + pe8.shape[1:], lambda c: (c, 0, 0)),
                  pl.BlockSpec((1,) + w1_flat.shape[1:], lambda c: (c, 0, 0))],
        out_specs=pl.BlockSpec((1, pe8.shape[1], w1_flat.shape[-1]), lambda c: (c, 0, 0)),
        compiler_params=_cparams(1),
        name="c_cmp_pe_term",
    )(pe8, w1_flat)


def _cmp_mlp_kernel(fs_ref, pe_ref, w2_ref, kg_ref, o_ref):
    ck = pl.program_id(1)
    nch = fs_ref.shape[2]
    hid_w = w2_ref.shape[1]
    pe_term = pe_ref[0, 0:1, :]
    first = fs_ref[0, 0, :, 0:hid_w]
    second = fs_ref[0, 0, :, hid_w:2 * hid_w]
    nxt = pltpu.roll(second, nch - 1, 0)
    last = lax.broadcasted_iota(I32, (nch, 1), 0) == nch - 1
    pre = first + jnp.where(last, 0.0, nxt) + pe_term
    hid = pre * jax.nn.sigmoid(pre)
    out = _dot(hid.astype(BF16), w2_ref[0].astype(BF16))

    @pl.when(ck < C_KV_HEADS)
    def _():
        o_ref[0, 0] = _rms(out, kg_ref[...])

    @pl.when(ck >= C_KV_HEADS)
    def _():
        o_ref[0, 0] = out


def cmp_mlp(fs, pe_term, w2, k_gain0):
    n_batch, n_ck, nch, _ = fs.shape
    return pl.pallas_call(
        _cmp_mlp_kernel,
        out_shape=jax.ShapeDtypeStruct((n_batch, n_ck, nch, HEAD_DIM), F32),
        grid=(n_batch, n_ck),
        in_specs=[
            pl.BlockSpec((1, 1, nch, fs.shape[-1]), lambda b, ck: (b, ck, 0, 0)),
            pl.BlockSpec((1,) + pe_term.shape[1:], lambda b, ck: (ck // C_KV_HEADS, 0, 0)),
            pl.BlockSpec((1,) + w2.shape[1:], lambda b, ck: (ck // C_KV_HEADS, 0, 0)),
            pl.BlockSpec((1, HEAD_DIM), lambda b, ck: (0, 0)),
        ],
        out_specs=pl.BlockSpec((1, 1, nch, HEAD_DIM), lambda b, ck: (b, ck, 0, 0)),
        compiler_params=_cparams(2),
        name="c_cmp_mlp",
    )(fs, pe_term, w2, k_gain0)


def _masked_softmax_rows(s, ok):
    s = jnp.where(ok, s, NEG)
    m = jnp.max(s, axis=1, keepdims=True)
    e = jnp.where(ok, jnp.exp(s - m), 0.0)
    return e, jnp.maximum(jnp.sum(e, axis=1, keepdims=True), 1e-30)


def _nsa_prompt_kernel(q_ref, gate_ref, cos_ref, sin_ref, kc_ref, vc_ref, ks_ref, vs_ref, kw_ref, vw_ref,
                       ov_ref, ex_ref, o_ref, s_scr, mx_scr, ext_scr, *, nsel):
    qb = q_ref.shape[0]
    nch = kc_ref.shape[2]
    tk = s_scr.shape[2]
    t0 = pl.program_id(2) * qb

    def stacked_bias(ok):
        return jnp.concatenate([jnp.where(ok, 0.0, NEG)] * C_GROUP, axis=0)

    def t_rows(n):
        return t0 + lax.broadcasted_iota(I32, (qb, n), 0)

    qscale = HEAD_DIM ** -0.5 * LOG2_E
    qs = [q_ref[:, g * HEAD_DIM:(g + 1) * HEAD_DIM] for g in range(C_GROUP)]
    qn4 = jnp.concatenate([(q * qscale).astype(BF16) for q in qs], axis=0)
    qr4 = jnp.concatenate([(_rope(q, cos_ref[...], sin_ref[...]) * qscale).astype(BF16) for q in qs], axis=0)

    wk = C_WINDOW + qb
    start = pl.multiple_of(jnp.maximum(t0 - C_WINDOW, 0), qb)
    dist = t_rows(wk) - (start + lax.broadcasted_iota(I32, (qb, wk), 1))
    s = _dot_nt(qr4, kw_ref[0, pl.ds(start, wk), :].astype(BF16)) + stacked_bias((dist >= 0) & (dist < C_WINDOW))
    p = jnp.exp2(s - jnp.max(s, axis=1, keepdims=True)).astype(BF16)
    ext = _dot(p, _with_ones(vw_ref[0, pl.ds(start, wk), :]))
    o_win4 = ext[:, :HEAD_DIM] / ext[:, HEAD_DIM:]

    cmp_bias = stacked_bias(lax.broadcasted_iota(I32, (qb, nch), 1) * C_CMP_STRIDE + (C_CMP_LEN - 1) <= t_rows(nch))
    s = _dot_nt(qn4, kc_ref[0, 0].astype(BF16)) + cmp_bias
    e = jnp.where(cmp_bias < 0.0, 0.0, jnp.exp2(s - jnp.max(s, axis=1, keepdims=True)))
    ext = _dot(e.astype(BF16), _with_ones(vc_ref[0, 0]))
    den = jnp.maximum(ext[:, HEAD_DIM:], 1e-30)
    o_cmp4 = ext[:, :HEAD_DIM] / den
    p = e / den
    p_sum = p[0:qb]
    for g in range(1, C_GROUP):
        p_sum = p_sum + p[g * qb:(g + 1) * qb]

    p_hi = p_sum.astype(BF16)
    p_lo = (p_sum - p_hi.astype(F32)).astype(BF16)
    nr = -(-nsel // 8) * 8
    imp = (_dot_nt(ov_ref[...], p_hi) + _dot_nt(ov_ref[...], p_lo))[:nr]
    blk = lax.broadcasted_iota(I32, (nr, qb), 0)
    cur = (t0 + lax.broadcasted_iota(I32, (nr, qb), 1)) // C_SEL_BLOCK
    causal = (blk <= cur) & (blk < nsel)
    forced = (blk == 0) | (blk == cur) | (blk == cur - 1)
    imp = jnp.where(causal, jnp.where(forced, C_FORCE_SCORE, imp), NEG)
    rank = jnp.zeros((nr, qb), I32)
    for jp in range(nsel):
        c = jnp.broadcast_to(imp[jp:jp + 1, :], (nr, qb))
        beats = (c > imp) | ((c == imp) & (blk > jp))
        rank = rank + beats.astype(I32)
    sel_t = jnp.where((rank < C_SEL_TOPN) & causal, 1.0, 0.0)
    sel_t = jnp.concatenate([sel_t, jnp.zeros((LANES - nr, qb), F32)], axis=0)
    sel = sel_t.T.astype(BF16)

    n_tiles = (t0 + qb + tk - 1) // tk
    mx_scr[...] = jnp.full(mx_scr.shape, NEG, F32)

    def score_tile(kt, carry):
        k0 = pl.multiple_of(kt * tk, tk)
        sel_keys = _dot(sel, ex_ref[kt])
        ok = (sel_keys > 0.5) & (k0 + lax.broadcasted_iota(I32, (qb, tk), 1) <= t_rows(tk))
        s = _dot_nt(qr4, ks_ref[0, pl.ds(k0, tk), :].astype(BF16)) + stacked_bias(ok)
        s_scr[kt] = s
        mx = mx_scr[...]
        for c in range(tk // LANES):
            mx = jnp.maximum(mx, s[:, c * LANES:(c + 1) * LANES])
        mx_scr[...] = mx
        return carry

    lax.fori_loop(0, n_tiles, score_tile, 0)
    m_slc = jnp.max(mx_scr[...], axis=1, keepdims=True)

    ext_scr[...] = jnp.zeros(ext_scr.shape, F32)

    def value_tile(kt, carry):
        k0 = pl.multiple_of(kt * tk, tk)
        p = jnp.exp2(s_scr[kt] - m_slc).astype(BF16)
        ext_scr[...] += _dot(p, _with_ones(vs_ref[0, pl.ds(k0, tk), :]))
        return carry

    lax.fori_loop(0, n_tiles, value_tile, 0)
    o_slc4 = ext_scr[:, :HEAD_DIM] / ext_scr[:, HEAD_DIM:]

    for g in range(C_GROUP):
        rows = slice(g * qb, (g + 1) * qb)
        o = (gate_ref[:, 3 * g:3 * g + 1] * o_cmp4[rows] + gate_ref[:, 3 * g + 1:3 * g + 2] * o_slc4[rows]
             + gate_ref[:, 3 * g + 2:3 * g + 3] * o_win4[rows])
        o_ref[:, g * HEAD_DIM:(g + 1) * HEAD_DIM] = o.astype(o_ref.dtype)


def nsa_prompt(z, gates, cos, sin, kcv, overlap_t, expand, batch, seq):
    m, nc = z.shape
    d = C_KV_HEADS * C_GROUP * HEAD_DIM
    qb = C_QBLOCK
    nq = seq // qb
    nsel = -(-seq // C_SEL_BLOCK)
    tk = C_KEY_TILE
    assert nsel <= LANES and seq % qb == 0 and seq >= C_WINDOW + qb and seq % tk == 0
    assert expand.shape == (seq // tk, LANES, tk)
    nch = kcv.shape[2]
    assert nch == HEAD_DIM and qb == LANES and overlap_t.shape == (LANES, nch)
    zv = z.reshape(batch, seq, nc)
    cb = d // HEAD_DIM
    kvw = C_KV_HEADS

    def kv_spec(branch, kv):
        off = cb + (branch * 2 + kv) * kvw
        return pl.BlockSpec((1, seq, HEAD_DIM), lambda b, k, q: (b, 0, off + k))

    return pl.pallas_call(
        functools.partial(_nsa_prompt_kernel, nsel=nsel),
        out_shape=jax.ShapeDtypeStruct((m, d), BF16),
        scratch_shapes=[pltpu.VMEM((seq // tk, C_GROUP * qb, tk), F32),
                        pltpu.VMEM((C_GROUP * qb, LANES), F32),
                        pltpu.VMEM((C_GROUP * qb, 2 * HEAD_DIM), F32)],
        grid=(batch, C_KV_HEADS, nq),
        in_specs=[
            pl.BlockSpec((qb, C_GROUP * HEAD_DIM), lambda b, k, q: (b * nq + q, k)),
            pl.BlockSpec((qb, LANES), lambda b, k, q: (b * nq + q, k)),
            pl.BlockSpec((qb, HEAD_DIM), lambda b, k, q: (q, 0)),
            pl.BlockSpec((qb, HEAD_DIM), lambda b, k, q: (q, 0)),
            pl.BlockSpec((1, 1, nch, HEAD_DIM), lambda b, k, q: (b, k, 0, 0)),
            pl.BlockSpec((1, 1, nch, HEAD_DIM), lambda b, k, q: (b, C_KV_HEADS + k, 0, 0)),
            kv_spec(1, 0), kv_spec(1, 1), kv_spec(2, 0), kv_spec(2, 1),
            pl.BlockSpec(overlap_t.shape, lambda b, k, q: (0, 0)),
            pl.BlockSpec(expand.shape, lambda b, k, q: (0, 0, 0)),
        ],
        out_specs=pl.BlockSpec((qb, C_GROUP * HEAD_DIM), lambda b, k, q: (b * nq + q, k)),
        compiler_params=_cparams(3),
        name="c_nsa_prompt",
    )(z, gates, cos, sin, kcv, kcv, zv, zv, zv, zv, overlap_t, expand)


def _nsa_sample_cmp_kernel(q_ref, kc_ref, vc_ref, ov_ref, ocmp_ref, idx_ref, *, t, nsel):
    nch = kc_ref.shape[2]
    scale = HEAD_DIM ** -0.5
    cmp_ok = lax.broadcasted_iota(I32, (8, nch), 1) * C_CMP_STRIDE + (C_CMP_LEN - 1) <= t
    row = lax.broadcasted_iota(I32, (8, nch), 0)
    p_sum = jnp.zeros((8, nch), F32)
    for k in range(C_KV_HEADS):
        q = (q_ref[0, k] * scale).astype(BF16)
        e, den = _masked_softmax_rows(_dot_nt(q, kc_ref[0, k].astype(BF16)), cmp_ok)
        p = e / den
        ocmp_ref[0, k] = _dot(p.astype(BF16), vc_ref[0, k].astype(BF16))
        p_k = jnp.sum(jnp.where(row < C_GROUP, p, 0.0), axis=0, keepdims=True)
        p_sum = jnp.where(row == k, jnp.broadcast_to(p_k, (8, nch)), p_sum)
    p_hi = p_sum.astype(BF16)
    p_lo = (p_sum - p_hi.astype(F32)).astype(BF16)
    imp = _dot(p_hi, ov_ref[...]) + _dot(p_lo, ov_ref[...])
    width = imp.shape[1]
    lane = lax.broadcasted_iota(I32, (8, width), 1)
    cur = t // C_SEL_BLOCK
    causal = (lane <= cur) & (lane < nsel)
    forced = (lane == 0) | (lane == cur) | (lane == cur - 1)
    work = jnp.where(causal, jnp.where(forced, C_FORCE_SCORE, imp), NEG)
    lane_f = lane.astype(F32)
    out_lane = lax.broadcasted_iota(I32, (8, LANES), 1)
    idx = jnp.full((8, LANES), -1, I32)
    for r in range(C_SEL_TOPN):
        best = jnp.max(work, axis=1, keepdims=True)
        pick = jnp.min(jnp.where(work == best, lane_f, float(width)), axis=1, keepdims=True)
        found = jnp.where(best > 0.5 * NEG, pick, -1.0).astype(I32)
        idx = jnp.where(out_lane == r, found, idx)
        work = jnp.where(lane_f == pick, NEG, work)
    idx_ref[0] = idx


def nsa_sample_cmp(q8, kcv, overlap, t, nsel):
    n_batch = q8.shape[0]
    nch = kcv.shape[2]
    assert C_KV_HEADS <= 8
    blk = pl.BlockSpec((1, C_KV_HEADS, 8, HEAD_DIM), lambda b: (b, 0, 0, 0))
    return pl.pallas_call(
        functools.partial(_nsa_sample_cmp_kernel, t=t, nsel=nsel),
        out_shape=[jax.ShapeDtypeStruct((n_batch, C_KV_HEADS, 8, HEAD_DIM), F32),
                   jax.ShapeDtypeStruct((n_batch, 8, LANES), I32)],
        grid=(n_batch,),
        in_specs=[
            blk,
            pl.BlockSpec((1, C_KV_HEADS, nch, HEAD_DIM), lambda b: (b, 0, 0, 0)),
            pl.BlockSpec((1, C_KV_HEADS, nch, HEAD_DIM), lambda b: (b, 1, 0, 0)),
            pl.BlockSpec(overlap.shape, lambda b: (0, 0)),
        ],
        out_specs=[blk, pl.BlockSpec((1, 8, LANES), lambda b: (b, 0, 0))],
        compiler_params=_cparams(1),
        name="c_nsa_sample_cmp",
    )(q8, kcv, kcv, overlap)


def _nsa_sample_kernel(idx_ref, pt_ref, q_ref, cos_ref, sin_ref, *refs, t, n_past_blocks):
    del pt_ref
    slc_refs, (new_ref, win_ref, ocmp_ref, gate_ref, o_ref) = refs[:C_SEL_TOPN], refs[C_SEL_TOPN:]
    b, k = pl.program_id(0), pl.program_id(1)

    def head_rows(ref, kv):
        return ref[:, kv, pl.ds(k, 1), :][:, 0, :].astype(BF16)

    base = (b * C_KV_HEADS + k) * C_SEL_TOPN
    scale = HEAD_DIM ** -0.5
    qr_f = _rope(q_ref[0, 0], cos_ref[...], sin_ref[...]) * scale
    qr = qr_f.astype(BF16)

    nk = C_SEL_TOPN * C_SEL_BLOCK
    s = _dot_nt(qr, jnp.concatenate([head_rows(r, 0) for r in slc_refs], axis=0))
    lane = lax.broadcasted_iota(I32, (8, nk), 1)
    blk = jnp.full((8, nk), -1, I32)
    has_new = False
    for n in range(C_SEL_TOPN):
        blk_n = idx_ref[base + n]
        blk = jnp.where((lane >= n * C_SEL_BLOCK) & (lane < (n + 1) * C_SEL_BLOCK), blk_n, blk)
        has_new = has_new | (blk_n == n_past_blocks)
    ok = (blk >= 0) & (blk < n_past_blocks) & (blk * C_SEL_BLOCK + (lane & (C_SEL_BLOCK - 1)) <= t)
    k_new, v_new = new_ref[0, 0, 0:1, :], new_ref[0, 0, 1:2, :]
    s = jnp.where(ok, s, NEG)
    s_new = jnp.where(has_new, jnp.sum(qr_f * k_new, axis=1, keepdims=True), NEG)
    m = jnp.maximum(jnp.max(s, axis=1, keepdims=True), s_new)
    p = jnp.where(ok, jnp.exp(s - m), 0.0)
    p_new = jnp.where(has_new, jnp.exp(s_new - m), 0.0)
    den = jnp.maximum(jnp.sum(p, axis=1, keepdims=True) + p_new, 1e-30)
    vs = jnp.concatenate([head_rows(r, 1) for r in slc_refs], axis=0)
    o_slc = (_dot(p.astype(BF16), vs) + p_new * v_new) / den

    lw = win_ref.shape[0]
    kw_new, vw_new = new_ref[0, 0, 2:3, :], new_ref[0, 0, 3:4, :]
    dist = lw - lax.broadcasted_iota(I32, (8, lw), 1)
    w_ok = (dist >= 0) & (dist < C_WINDOW)
    s = jnp.where(w_ok, _dot_nt(qr, head_rows(win_ref, 0)), NEG)
    s_new = jnp.sum(qr_f * kw_new, axis=1, keepdims=True)
    m = jnp.maximum(jnp.max(s, axis=1, keepdims=True), s_new)
    p = jnp.where(w_ok, jnp.exp(s - m), 0.0)
    p_new = jnp.exp(s_new - m)
    den = jnp.sum(p, axis=1, keepdims=True) + p_new
    o_win = (_dot(p.astype(BF16), head_rows(win_ref, 1)) + p_new * vw_new) / den

    gate = gate_ref[0, 0]
    o_ref[0, 0] = gate[:, 0:1] * ocmp_ref[0, 0] + gate[:, 1:2] * o_slc + gate[:, 2:3] * o_win


def nsa_sample(top_idx, page_table, q8, cos, sin, pool_slc, win_buf, li, new_kv, ocmp, gate8, t, n_past_blocks):
    n_batch = q8.shape[0]
    n_pages = page_table.shape[1]
    n_layers, n_pool, page_rows = pool_slc.shape[:3]
    blocks_per_page = page_rows // C_SEL_BLOCK
    kv_dims = (2, C_KV_HEADS, HEAD_DIM)
    pool = pool_slc.reshape((n_layers, n_pool, blocks_per_page, C_SEL_BLOCK) + kv_dims)
    lw = win_buf.shape[2]

    def slc_spec(n):
        def index(b, k, idx, pt):
            blk = jnp.clip(idx[(b * C_KV_HEADS + k) * C_SEL_TOPN + n], 0, n_past_blocks - 1)
            return (li, pt[b * n_pages + blk // blocks_per_page], blk % blocks_per_page, 0, 0, 0, 0)
        return pl.BlockSpec((None, None, None, C_SEL_BLOCK) + kv_dims, index)

    win_spec = pl.BlockSpec((None, None, lw) + kv_dims, lambda b, k, idx, pt: (li, b, 0, 0, 0, 0))
    blk8 = pl.BlockSpec((1, 1, 8, HEAD_DIM), lambda b, k, idx, pt: (b, k, 0, 0))
    tab = pl.BlockSpec((8, HEAD_DIM), lambda b, k, idx, pt: (0, 0))
    return pl.pallas_call(
        functools.partial(_nsa_sample_kernel, t=t, n_past_blocks=n_past_blocks),
        out_shape=jax.ShapeDtypeStruct((n_batch, C_KV_HEADS, 8, HEAD_DIM), F32),
        grid_spec=pltpu.PrefetchScalarGridSpec(
            num_scalar_prefetch=2,
            grid=(n_batch, C_KV_HEADS),
            in_specs=[blk8, tab, tab] + [slc_spec(n) for n in range(C_SEL_TOPN)] + [blk8, win_spec, blk8, blk8],
            out_specs=blk8,
        ),
        compiler_params=_cparams(2),
        name="c_nsa_sample",
    )(top_idx.reshape(-1), page_table.reshape(-1), q8, cos, sin, *([pool] * C_SEL_TOPN), new_kv, win_buf, ocmp, gate8)


def _shift_append_kernel(cur_ref, nxt_ref, new_ref, out_ref):
    r = cur_ref.shape[0]
    out_ref[0:r - 1] = cur_ref[1:r]
    last = pl.program_id(1) == pl.num_programs(1) - 1

    @pl.when(last)
    def _():
        out_ref[r - 1:r] = new_ref[...]

    @pl.when(jnp.logical_not(last))
    def _():
        out_ref[r - 1:r] = nxt_ref[...]


def shift_append(cache, li, new, name):
    n_batch, length = cache.shape[1:3]
    tail = cache.shape[3:]
    assert new.shape == (n_batch, 1) + tail
    r = min(length, SHIFT_ROWS)
    assert length % r == 0
    zeros = (0,) * len(tail)
    return pl.pallas_call(
        _shift_append_kernel,
        out_shape=jax.ShapeDtypeStruct(cache.shape[1:], cache.dtype),
        grid=(n_batch, length // r),
        in_specs=[
            pl.BlockSpec((None, None, r) + tail, lambda b, i: (li, b, i) + zeros),
            pl.BlockSpec((None, None, 1) + tail, lambda b, i: (li, b, jnp.minimum((i + 1) * r, length - 1)) + zeros),
            pl.BlockSpec((None, 1) + tail, lambda b, i: (b, 0) + zeros),
        ],
        out_specs=pl.BlockSpec((None, r) + tail, lambda b, i: (b, i) + zeros),
        compiler_params=_cparams(2),
        name=name,
    )(cache, cache, new)


def _rope_tables(pos):
    half = HEAD_DIM // 2
    inv = ROPE_THETA ** (-(jnp.arange(half, dtype=F32) * 2.0 / HEAD_DIM))
    ang = pos.astype(F32)[:, None] * inv[None, :]
    cos, sin = jnp.cos(ang), jnp.sin(ang)
    return jnp.concatenate([cos, cos], axis=1), jnp.concatenate([-sin, sin], axis=1)


def _overlap_matrix(nch, nsel, width):
    cmp_start = jnp.arange(nch) * C_CMP_STRIDE
    sel_start = jnp.arange(width) * C_SEL_BLOCK
    ov = ((cmp_start[:, None] < sel_start[None, :] + C_SEL_BLOCK)
          & (cmp_start[:, None] + C_CMP_LEN > sel_start[None, :])
          & (jnp.arange(width)[None, :] < nsel))
    return ov.astype(BF16)


def _c_weights(prm, li, n_heads):
    d = n_heads * HEAD_DIM
    nkv = 3 * 2 * C_KV_HEADS * HEAD_DIM
    k_gain = prm['c_k_norm'][li]
    ones = jnp.ones((C_KV_HEADS * HEAD_DIM,), F32)
    head_gain = jnp.concatenate([
        jnp.tile(prm['c_q_norm'][li], n_heads), ones, ones,
        jnp.tile(k_gain[1], C_KV_HEADS), ones, jnp.tile(k_gain[2], C_KV_HEADS), ones])[None, :]
    wg = prm['w_c_in'][li][:, d + nkv:].reshape(-1, C_KV_HEADS, C_GROUP * 3)
    wg = jnp.pad(wg, ((0, 0), (0, 0), (0, LANES - C_GROUP * 3))).reshape(1, -1, C_KV_HEADS * LANES)
    w1 = prm['c_cmp_w1'][li]
    w_cat = jnp.concatenate([w1[:, :C_CMP_STRIDE], w1[:, C_CMP_STRIDE:]], axis=-1).astype(BF16)
    w_cat = w_cat.reshape(2, C_CMP_STRIDE // 2, 2 * HEAD_DIM, w_cat.shape[-1])
    pe8 = jnp.pad(prm['c_cmp_pe'][li].reshape(2, 1, -1), ((0, 0), (0, 7), (0, 0)))
    w1_flat = w1.reshape(2, C_CMP_LEN * HEAD_DIM, -1)
    return head_gain, wg, w_cat, pe8, w1_flat, prm['c_cmp_w2'][li], k_gain[0][None, :]


TN = 512
TN_WIDE_K = 256
TM_OUT_PROJ = 2048
C_MODES = (NORM,) * 4 + (RAW, RAW, NORM_ROPE, RAW, NORM_ROPE, RAW)
A_MODES = ((NORM_ROPE,) * 8 + (RAW,) * 4) * len(A_PATTERNS)


def _pad_rows(a, rows):
    return jnp.pad(a, ((0, rows - a.shape[0]),) + ((0, 0),) * (a.ndim - 1))


def _a_head_gain(prm, li, n_heads):
    ones = jnp.ones((n_heads * HEAD_DIM,), F32)
    parts = []
    for g in range(len(A_PATTERNS)):
        parts += [jnp.tile(prm['a_q_norm'][li, g], n_heads), jnp.tile(prm['a_k_norm'][li, g], n_heads), ones]
    return jnp.concatenate(parts)[None, :]


def _channel_mixer(x, p, prm, i, tag):
    act = swiglu_in(x, prm['ffn_norm'], prm['w_ffn_in'], i, tn=TN, name=f"ffn_in_{tag}")
    x = linear_residual(act, prm['w_ffn_out'], i, x, tn=TN_WIDE_K, name=f"ffn_out_{tag}")
    return ple(x, prm['ple_norm'], p, prm['w_ple_proj'], prm['w_ple_gate'], i, tn=TN, name=f"ple_{tag}")


def _run_prompt(x3, p, prm):
    batch, seq, d = x3.shape
    n_heads = d // HEAD_DIM
    x = x3.reshape(batch * seq, d)
    cos, sin = _rope_tables(jnp.arange(seq))
    new = {}
    depth = prm['attn_norm'].shape[0]
    for i in range(depth):
        kind, li = i % 4, i // 4
        if kind == 0:
            z = norm_linear(x, prm['attn_norm'], i, prm['w_a_qkv'], li, col0=0, modes=A_MODES,
                            head_gain=_a_head_gain(prm, li, n_heads), cos=cos, sin=sin, tn=TN, name="a_qkv_prompt")
            x = linear_residual(a_attention_prompt(z, batch, seq), prm['w_a_out'], li, x, tn=TN, tm=TM_OUT_PROJ,
                                name="a_out_prompt")
            z3 = z.reshape(batch, seq, -1)
            for g, (win, dil) in enumerate(A_PATTERNS):
                kv = z3[:, seq - min(win, seq):, (3 * g + 1) * d:(3 * g + 3) * d]
                new.setdefault(f'a_w{g + 1}', []).append(kv.reshape(batch, -1, 2, n_heads, HEAD_DIM))
        elif kind == 1:
            bg, u = b_in(x, prm['attn_norm'], i, prm['w_b_in'], li, tn=TN_WIDE_K, name="b_in_prompt")
            x = b_out_prompt(u, bg, prm['b_conv'], prm['w_b_out'], li, x, seq, tn=TN, name="b_out_prompt")
            new.setdefault('b_conv', []).append(u.reshape(batch, seq, d)[:, seq - (CONV_W - 1):])
        elif kind == 2:
            head_gain, wg, w_cat, pe8, w1_flat, w2, k_gain0 = _c_weights(prm, li, n_heads)
            z = norm_linear(x, prm['attn_norm'], i, prm['w_c_in'], li, col0=0, modes=C_MODES,
                            head_gain=head_gain, cos=cos, sin=sin, tn=TN, name="c_in_prompt")
            gates = norm_linear(x, prm['attn_norm'], i, wg, 0, col0=0, modes=(SIGMOID,),
                                head_gain=head_gain[:, :TN], cos=cos, sin=sin, tn=TN, name="c_gate_prompt")
            n_pages = seq // LANES
            pool = z.reshape(batch * n_pages, LANES, z.shape[1])
            kv_width = 2 * C_KV_HEADS * HEAD_DIM
            fs = cmp_first_second(pool, jnp.arange(batch * n_pages, dtype=I32), d // kv_width, w_cat, batch, n_pages)
            kcv = cmp_mlp(fs, cmp_pe_term(pe8, w1_flat), w2, k_gain0)
            nsel = -(-seq // C_SEL_BLOCK)
            overlap = _overlap_matrix(kcv.shape[2], nsel, LANES)
            key_blk = (jnp.arange(seq) // C_SEL_BLOCK).reshape(seq // C_KEY_TILE, 1, C_KEY_TILE)
            expand = (key_blk == jnp.arange(LANES)[None, :, None]).astype(BF16)
            o = nsa_prompt(z, gates, cos, sin, kcv, overlap.T, expand, batch, seq)
            x = linear_residual(o, prm['w_c_out'], li, x, tn=TN, tm=TM_OUT_PROJ, name="c_out_prompt")
            z3 = z.reshape(batch, seq, -1)
            for br, nm in enumerate(('c_cmp', 'c_slc', 'c_win')):
                rows = min(C_WINDOW, seq) if nm == 'c_win' else seq
                kv = z3[:, seq - rows:, d + br * kv_width:d + (br + 1) * kv_width]
                new.setdefault(nm, []).append(kv.reshape(batch, rows, 2, C_KV_HEADS, HEAD_DIM))
        else:
            x, tail = d_mixer_prompt(x, prm['attn_norm'], i, prm['w_d_group'], prm['d_scale'], li, seq,
                                     name="d_mixer_prompt")
            new.setdefault('d_pool', []).append(tail[:, 1:])
        x = _channel_mixer(x, p, prm, i, "prompt")
    return x.reshape(batch, seq, d), {nm: jnp.stack(v, axis=0) for nm, v in new.items()}


def _run_sample(x3, p, prm, past, page_table):
    n_batch, n_new, d = x3.shape
    assert n_new == 1
    n_heads = d // HEAD_DIM
    rows = SUBLANES_BF16
    x = _pad_rows(x3.reshape(n_batch, d), rows)
    past_len = page_table.shape[1] * past['c_cmp'].shape[2]
    cos, sin = _rope_tables(jnp.full((rows,), past_len))
    new = {}
    depth = prm['attn_norm'].shape[0]
    for i in range(depth):
        kind, li = i % 4, i // 4
        if kind == 0:
            z = norm_linear(x, prm['attn_norm'], i, prm['w_a_qkv'], li, col0=0, modes=A_MODES,
                            head_gain=_a_head_gain(prm, li, n_heads), cos=cos, sin=sin, tn=TN, name="a_qkv_sample")
            caches = [past[f'a_w{g + 1}'] for g in range(len(A_PATTERNS))]
            o = a_attention_sample(z, caches, li, n_batch)
            x = linear_residual(_pad_rows(o, rows), prm['w_a_out'], li, x, tn=TN, name="a_out_sample")
            for g, cache in enumerate(caches):
                kv = z[:n_batch, (3 * g + 1) * d:(3 * g + 3) * d].reshape(n_batch, 1, 2, n_heads, HEAD_DIM)
                new.setdefault(f'a_w{g + 1}', []).append(shift_append(cache, li, kv, f"a_w{g + 1}_shift"))
        elif kind == 1:
            bg, u = b_in(x, prm['attn_norm'], i, prm['w_b_in'], li, tn=TN_WIDE_K, name="b_in_sample")
            hist = past['b_conv'][li]
            x = b_out_sample(u, _pad_rows(hist[:, 1], rows), _pad_rows(hist[:, 0], rows), bg, prm['b_conv'],
                             prm['w_b_out'], li, x, tn=TN, name="b_out_sample")
            new.setdefault('b_conv', []).append(jnp.concatenate([hist[:, 1:], u[:n_batch, None]], axis=1))
        elif kind == 2:
            head_gain, wg, w_cat, pe8, w1_flat, w2, k_gain0 = _c_weights(prm, li, n_heads)
            z = norm_linear(x, prm['attn_norm'], i, prm['w_c_in'], li, col0=0, modes=C_MODES,
                            head_gain=head_gain, cos=cos, sin=sin, tn=TN, name="c_in_sample")
            gates = norm_linear(x, prm['attn_norm'], i, wg, 0, col0=0, modes=(SIGMOID,),
                                head_gain=head_gain[:, :TN], cos=cos, sin=sin, tn=TN, name="c_gate_sample")
            kv_width = 2 * C_KV_HEADS * HEAD_DIM
            kcv = cmp_mlp(cmp_first_second_paged(past['c_cmp'], li, page_table, w_cat),
                          cmp_pe_term(pe8, w1_flat), w2, k_gain0)
            nsel = -(-(past_len + 1) // C_SEL_BLOCK)
            width = -(-nsel // LANES) * LANES
            overlap = _overlap_matrix(kcv.shape[2], nsel, width)
            zb = z[:n_batch]
            q8 = jnp.pad(zb[:, :d].reshape(n_batch, C_KV_HEADS, C_GROUP, HEAD_DIM),
                         ((0, 0), (0, 0), (0, 8 - C_GROUP), (0, 0)))
            ocmp, idx = nsa_sample_cmp(q8, kcv, overlap, past_len, nsel)
            top_idx = idx[:, :C_KV_HEADS, :C_SEL_TOPN]
            kvh_w = C_KV_HEADS * HEAD_DIM
            new_rows = [zb[:, d + kv_width + j * kvh_w:d + kv_width + (j + 1) * kvh_w]
                        .reshape(n_batch, C_KV_HEADS, 1, HEAD_DIM) for j in range(4)]
            new_kv = jnp.pad(jnp.concatenate(new_rows, axis=2), ((0, 0), (0, 0), (0, 4), (0, 0)))
            gate8 = gates[:n_batch].reshape(n_batch, C_KV_HEADS, LANES)[:, :, :C_GROUP * 3]
            gate8 = jnp.pad(gate8.reshape(n_batch, C_KV_HEADS, C_GROUP, 3),
                            ((0, 0), (0, 0), (0, 8 - C_GROUP), (0, HEAD_DIM - 3)))
            o = nsa_sample(top_idx, page_table, q8, cos[:8], sin[:8], past['c_slc'], past['c_win'], li, new_kv,
                           ocmp, gate8, past_len, past_len // C_SEL_BLOCK)
            o = o[:, :, :C_GROUP].reshape(n_batch, d)
            x = linear_residual(_pad_rows(o, rows), prm['w_c_out'], li, x, tn=TN, name="c_out_sample")
            for br, nm in enumerate(('c_cmp', 'c_slc', 'c_win')):
                kv = zb[:, d + br * kv_width:d + (br + 1) * kv_width].reshape(n_batch, 1, 2, C_KV_HEADS, HEAD_DIM)
                if nm == 'c_win':
                    kv = shift_append(past['c_win'], li, kv, "c_win_shift")
                new.setdefault(nm, []).append(kv)
        else:
            hist = past['d_pool'][li]
            hist_t = jnp.pad(jnp.swapaxes(hist, 0, 1), ((0, 0), (0, rows - n_batch), (0, 0)))
            x, h = d_mixer_sample(x, hist_t, prm['attn_norm'], i, prm['w_d_group'], prm['d_scale'], li,
                                  name="d_mixer_sample")
            new.setdefault('d_pool', []).append(jnp.concatenate([hist[:, 1:], h[:n_batch, None]], axis=1))
        x = _channel_mixer(x, p, prm, i, "sample")
    return x[:n_batch].reshape(n_batch, 1, d), {nm: jnp.stack(v, axis=0) for nm, v in new.items()}


def kernel(x_prompt, x_sample, cache_a_w1, cache_a_w2, cache_a_w3, state_b_conv, cache_c_cmp, cache_c_slc, cache_c_win, state_d_pool, page_table, p_prompt, p_sample, attn_norm, ffn_norm, ple_norm, w_a_qkv, a_q_norm, a_k_norm, w_a_out, w_b_in, b_conv, w_b_out, w_c_in, c_q_norm, c_k_norm, c_cmp_pe, c_cmp_w1, c_cmp_w2, w_c_out, w_d_group, d_scale, w_ffn_in, w_ffn_out, w_ple_proj, w_ple_gate):
    prm = dict(attn_norm=attn_norm, ffn_norm=ffn_norm, ple_norm=ple_norm, w_a_qkv=w_a_qkv, a_q_norm=a_q_norm,
               a_k_norm=a_k_norm, w_a_out=w_a_out, w_b_in=w_b_in, b_conv=b_conv, w_b_out=w_b_out, w_c_in=w_c_in,
               c_q_norm=c_q_norm, c_k_norm=c_k_norm, c_cmp_pe=c_cmp_pe, c_cmp_w1=c_cmp_w1, c_cmp_w2=c_cmp_w2,
               w_c_out=w_c_out, w_d_group=w_d_group, d_scale=d_scale, w_ffn_in=w_ffn_in, w_ffn_out=w_ffn_out,
               w_ple_proj=w_ple_proj, w_ple_gate=w_ple_gate)
    past = dict(a_w1=cache_a_w1, a_w2=cache_a_w2, a_w3=cache_a_w3, b_conv=state_b_conv, c_cmp=cache_c_cmp,
                c_slc=cache_c_slc, c_win=cache_c_win, d_pool=state_d_pool)
    depth = attn_norm.shape[0]
    batch, seq, _ = x_prompt.shape
    n_dec = x_sample.shape[0]
    p_p = p_prompt.reshape(depth, batch * seq, -1)
    p_s = jnp.pad(p_sample.reshape(depth, n_dec, -1), ((0, 0), (0, SUBLANES_BF16 - n_dec), (0, 0)))
    y_prompt, sp = _run_prompt(x_prompt, p_p, prm)
    y_sample, ss = _run_sample(x_sample, p_s, prm, past, page_table)
    return (y_prompt, y_sample,
            sp['a_w1'], ss['a_w1'], sp['a_w2'], ss['a_w2'], sp['a_w3'], ss['a_w3'],
            sp['b_conv'], ss['b_conv'],
            sp['c_cmp'], ss['c_cmp'], sp['c_slc'], ss['c_slc'], sp['c_win'], ss['c_win'],
            sp['d_pool'], ss['d_pool'])
```

```python
import functools

import jax
import jax.numpy as jnp
from jax import lax
from jax.experimental import pallas as pl
from jax.experimental.pallas import tpu as pltpu

F32 = jnp.float32
BF16 = jnp.bfloat16
I32 = jnp.int32

HEAD_DIM = 128
LANES = 128
SUBLANES_BF16 = 16
RMS_EPS = 1e-6
ROPE_THETA = 10000.0
NEG = -1e30
V7X_VMEM_BYTES = 64 * 1024 * 1024
VMEM_LIMIT = V7X_VMEM_BYTES - 8 * 1024 * 1024

A_PATTERNS = ((128, 1), (512, 4), (2048, 16))
A_BLOCK = 128
A_UNROLL = 4
CONV_W = 3
C_KV_HEADS = 4
C_GROUP = 4
C_CMP_LEN = 32
C_CMP_STRIDE = 16
C_SEL_BLOCK = 64
C_SEL_TOPN = 16
C_WINDOW = 512
C_FORCE_SCORE = 1e4
C_QBLOCK = 128
C_KEY_TILE = 512
LOG2_E = 1.4426950408889634
D_POOLS = (2, 4, 8, 16)
D_HIST = max(D_POOLS) - 1
PAGES_PER_GROUP = 16
SHIFT_ROWS = 256

RAW, NORM, NORM_ROPE, SIGMOID = range(4)


def _cparams(n_axes):
    return pltpu.CompilerParams(dimension_semantics=("arbitrary",) * n_axes,
                                vmem_limit_bytes=VMEM_LIMIT)


def _row_tile(m):
    return 1024 if m % 1024 == 0 else m


def _dot(a, b):
    return jnp.dot(a, b, preferred_element_type=F32)


def _dot_nt(a, b):
    return lax.dot_general(a, b, (((1,), (1,)), ((), ())), preferred_element_type=F32)


def _with_ones(v):
    return jnp.concatenate([v.astype(BF16), jnp.ones(v.shape, BF16)], axis=1)


def _rms(x, g):
    return x * lax.rsqrt(jnp.mean(x * x, axis=-1, keepdims=True) + RMS_EPS) * g


def _rope(y, cos, sin):
    return y * cos + pltpu.roll(y, HEAD_DIM // 2, 1) * sin


def _tile_pred(j, tiles):
    pred = None
    lo = prev = tiles[0]
    runs = []
    for t in tiles[1:]:
        if t != prev + 1:
            runs.append((lo, prev))
            lo = t
        prev = t
    runs.append((lo, prev))
    for lo, hi in runs:
        p = (j == lo) if lo == hi else ((j >= lo) & (j <= hi))
        pred = p if pred is None else (pred | p)
    return pred


def _norm_linear_kernel(x_ref, xs_ref, g_ref, w_ref, hg_ref, cos_ref, sin_ref, coss_ref, sins_ref, o_ref, os_ref,
                        h_scr, *, modes):
    j = pl.program_id(1)
    tm = x_ref.shape[0]

    @pl.when(j == 0)
    def _():
        h_scr[0:tm, :] = _rms(x_ref[...], g_ref[...]).astype(BF16)
        h_scr[tm:, :] = _rms(xs_ref[...], g_ref[...]).astype(BF16)

    acc = _dot(h_scr[...], w_ref[...].astype(BF16))
    tn = acc.shape[1]

    def emit(mode):
        for rows, out_ref, c_ref, s_ref in ((slice(0, tm), o_ref, cos_ref, sin_ref),
                                            (slice(tm, None), os_ref, coss_ref, sins_ref)):
            if mode == RAW:
                out_ref[...] = acc[rows].astype(out_ref.dtype)
            elif mode == SIGMOID:
                out_ref[...] = jax.nn.sigmoid(acc[rows]).astype(out_ref.dtype)
            else:
                for c in range(tn // HEAD_DIM):
                    sl = slice(c * HEAD_DIM, (c + 1) * HEAD_DIM)
                    y = _rms(acc[rows, sl], hg_ref[:, sl])
                    if mode == NORM_ROPE:
                        y = _rope(y, c_ref[...], s_ref[...])
                    out_ref[:, sl] = y.astype(out_ref.dtype)

    kinds = sorted(set(modes))
    if len(kinds) == 1:
        emit(kinds[0])
    else:
        for m in kinds:
            tiles = [t for t, mm in enumerate(modes) if mm == m]
            pl.when(_tile_pred(j, tiles))(functools.partial(emit, m))


def _dec_spec(rows, cols):
    return pl.BlockSpec((None, rows, cols), lambda i, j: (i, 0, j))


def _dec_shape(n_row_tiles, rows, cols, dtype):
    return jax.ShapeDtypeStruct((n_row_tiles, rows, cols), dtype)


def norm_linear(x, xs, gain, gi, w, wi, *, col0, modes, head_gain, rope, rope_s, tn, name):
    m, k = x.shape
    ms = xs.shape[0]
    tm = _row_tile(m)
    nt = len(modes)
    (cos, sin), (cos_s, sin_s) = rope, rope_s
    assert col0 % tn == 0 and cos.shape[0] % tm == 0 and cos_s.shape[0] == ms and ms % SUBLANES_BF16 == 0
    coff = col0 // tn
    n_pos_tiles = cos.shape[0] // tm
    full_s = pl.BlockSpec((ms, HEAD_DIM), lambda i, j: (0, 0))
    out, out_s = pl.pallas_call(
        functools.partial(_norm_linear_kernel, modes=tuple(modes)),
        out_shape=[jax.ShapeDtypeStruct((m, nt * tn), F32), _dec_shape(m // tm, ms, nt * tn, F32)],
        grid=(m // tm, nt),
        in_specs=[
            pl.BlockSpec((tm, k), lambda i, j: (i, 0)),
            pl.BlockSpec((ms, k), lambda i, j: (0, 0)),
            pl.BlockSpec((None, 1, k), lambda i, j: (gi, 0, 0)),
            pl.BlockSpec((None, k, tn), lambda i, j: (wi, 0, j + coff)),
            pl.BlockSpec((1, tn), lambda i, j: (0, j)),
            pl.BlockSpec((tm, HEAD_DIM), lambda i, j: (i % n_pos_tiles, 0)),
            pl.BlockSpec((tm, HEAD_DIM), lambda i, j: (i % n_pos_tiles, 0)),
            full_s, full_s,
        ],
        out_specs=[pl.BlockSpec((tm, tn), lambda i, j: (i, j)), _dec_spec(ms, tn)],
        scratch_shapes=[pltpu.VMEM((tm + ms, k), BF16)],
        compiler_params=_cparams(2),
        name=name,
    )(x, xs, gain.reshape(gain.shape[0], 1, k), w, head_gain, cos, sin, cos_s, sin_s)
    return out, out_s[0]


def _linear_res_kernel(a_ref, w_ref, r_ref, o_ref):
    o_ref[...] = r_ref[...] + _dot(a_ref[...].astype(BF16), w_ref[...].astype(BF16))


def linear_residual(a, w, li, res, *, tn, name, tm=None):
    m, k = a.shape
    n = w.shape[-1]
    tm = _row_tile(m) if tm is None or m % tm else tm
    return pl.pallas_call(
        _linear_res_kernel,
        out_shape=jax.ShapeDtypeStruct((m, n), F32),
        grid=(m // tm, n // tn),
        in_specs=[
            pl.BlockSpec((tm, k), lambda i, j: (i, 0)),
            pl.BlockSpec((None, k, tn), lambda i, j: (li, 0, j)),
            pl.BlockSpec((tm, tn), lambda i, j: (i, j)),
        ],
        out_specs=pl.BlockSpec((tm, tn), lambda i, j: (i, j)),
        compiler_params=_cparams(2),
        name=name,
    )(a, w, res)


def _swiglu_in_kernel(x_ref, xs_ref, g_ref, wg_ref, wu_ref, o_ref, os_ref, h_scr):
    tm = x_ref.shape[0]

    @pl.when(pl.program_id(1) == 0)
    def _():
        h_scr[0:tm, :] = _rms(x_ref[...], g_ref[...]).astype(BF16)
        h_scr[tm:, :] = _rms(xs_ref[...], g_ref[...]).astype(BF16)

    h = h_scr[...]
    a = _dot(h, wg_ref[...].astype(BF16))
    b = _dot(h, wu_ref[...].astype(BF16))
    act = (a * jax.nn.sigmoid(a) * b).astype(o_ref.dtype)
    o_ref[...] = act[0:tm]
    os_ref[...] = act[tm:]


def swiglu_in(x, xs, gain, w, li, *, tn, name):
    m, k = x.shape
    ms = xs.shape[0]
    f = w.shape[-1] // 2
    tm = _row_tile(m)
    nt = f // tn
    act, act_s = pl.pallas_call(
        _swiglu_in_kernel,
        out_shape=[jax.ShapeDtypeStruct((m, f), BF16), _dec_shape(m // tm, ms, f, BF16)],
        grid=(m // tm, nt),
        in_specs=[
            pl.BlockSpec((tm, k), lambda i, j: (i, 0)),
            pl.BlockSpec((ms, k), lambda i, j: (0, 0)),
            pl.BlockSpec((None, 1, k), lambda i, j: (li, 0, 0)),
            pl.BlockSpec((None, k, tn), lambda i, j: (li, 0, j)),
            pl.BlockSpec((None, k, tn), lambda i, j: (li, 0, j + nt)),
        ],
        out_specs=[pl.BlockSpec((tm, tn), lambda i, j: (i, j)), _dec_spec(ms, tn)],
        scratch_shapes=[pltpu.VMEM((tm + ms, k), BF16)],
        compiler_params=_cparams(2),
        name=name,
    )(x, xs, gain.reshape(gain.shape[0], 1, k), w, w)
    return act, act_s[0]


def _ple_kernel(x_ref, xs_ref, g_ref, p_ref, ps_ref, wp_ref, wg_ref, o_ref, os_ref, h_scr):
    j = pl.program_id(1)
    tm, tn = o_ref.shape

    @pl.when(j == 0)
    def _():
        h_scr[0:tm, :] = _rms(x_ref[...], g_ref[...]).astype(BF16)
        h_scr[tm:, :] = _rms(xs_ref[...], g_ref[...]).astype(BF16)

    gate = jax.nn.sigmoid(_dot(h_scr[...], wg_ref[...].astype(BF16)))
    p_rows = jnp.concatenate([p_ref[...].astype(BF16), ps_ref[...].astype(BF16)], axis=0)
    upd = _dot(p_rows, wp_ref[...].astype(BF16)) * gate
    cols = pl.ds(pl.multiple_of(j * tn, tn), tn)
    o_ref[...] = x_ref[:, cols] + upd[0:tm]
    os_ref[...] = xs_ref[:, cols] + upd[tm:]


def ple(x, xs, gain, p, ps, wp, wg, li, *, tn, name):
    m, k = x.shape
    ms = xs.shape[0]
    pd = p.shape[-1]
    tm = _row_tile(m)
    assert p.shape[1] == m and ps.shape[1] == ms
    out, out_s = pl.pallas_call(
        _ple_kernel,
        out_shape=[jax.ShapeDtypeStruct((m, k), F32), _dec_shape(m // tm, ms, k, F32)],
        grid=(m // tm, k // tn),
        in_specs=[
            pl.BlockSpec((tm, k), lambda i, j: (i, 0)),
            pl.BlockSpec((ms, k), lambda i, j: (0, 0)),
            pl.BlockSpec((None, 1, k), lambda i, j: (li, 0, 0)),
            pl.BlockSpec((None, tm, pd), lambda i, j: (li, i, 0)),
            pl.BlockSpec((None, ms, pd), lambda i, j: (li, 0, 0)),
            pl.BlockSpec((None, pd, tn), lambda i, j: (li, 0, j)),
            pl.BlockSpec((None, k, tn), lambda i, j: (li, 0, j)),
        ],
        out_specs=[pl.BlockSpec((tm, tn), lambda i, j: (i, j)), _dec_spec(ms, tn)],
        scratch_shapes=[pltpu.VMEM((tm + ms, k), BF16)],
        compiler_params=_cparams(2),
        name=name,
    )(x, xs, gain.reshape(gain.shape[0], 1, k), p, ps, wp, wg)
    return out, out_s[0]


def _a_prompt_kernel(*refs, seq):
    qkv_refs, o_ref = refs[:9], refs[9]
    acc_scr, m_scr, l_scr = refs[10:]
    scale = HEAD_DIM ** -0.5
    row = lax.broadcasted_iota(I32, (A_BLOCK, A_BLOCK), 0)
    col = lax.broadcasted_iota(I32, (A_BLOCK, A_BLOCK), 1)
    cur_ok = col <= row
    prev_ok = col >= row
    wide = (A_BLOCK, HEAD_DIM)
    first = len(A_PATTERNS) - 1
    for g, (win, dil) in reversed(list(enumerate(A_PATTERNS))):
        q_ref, k_ref, v_ref = qkv_refs[3 * g:3 * g + 3]
        nb = seq // dil // A_BLOCK

        def rows_of(r, n, dil=dil):
            start = r + dil * A_BLOCK * n
            if dil == 1:
                return pl.ds(pl.multiple_of(start, A_BLOCK), A_BLOCK)
            return pl.ds(start, A_BLOCK, stride=dil)

        def attend(idx, nb=nb, q_ref=q_ref, k_ref=k_ref, v_ref=v_ref, rows_of=rows_of):
            r, n = idx // nb, idx % nb
            rows = rows_of(r, n)
            q = (q_ref[0, rows, :] * scale).astype(BF16)
            s_c = jnp.where(cur_ok, _dot_nt(q, k_ref[0, rows, :].astype(BF16)), NEG)
            if nb > 1:
                prows = rows_of(r, jnp.maximum(n - 1, 0))
                s_p = jnp.where(prev_ok & (n > 0), _dot_nt(q, k_ref[0, prows, :].astype(BF16)), NEG)
                m = jnp.max(jnp.maximum(s_c, s_p), axis=1, keepdims=True)
            else:
                m = jnp.max(s_c, axis=1, keepdims=True)
            ext = _dot(jnp.exp(s_c - m).astype(BF16), _with_ones(v_ref[0, rows, :]))
            if nb > 1:
                ext = ext + _dot(jnp.exp(s_p - m).astype(BF16), _with_ones(v_ref[0, prows, :]))
            return rows, m, ext[:, :HEAD_DIM], ext[:, HEAD_DIM:]

        def block(idx, carry, g=g, attend=attend):
            rows, m, acc, den = attend(idx)
            if g == first:
                acc_scr[rows, :] = acc
                m_scr[rows, :] = jnp.broadcast_to(m, wide)
                l_scr[rows, :] = den
            else:
                m_old = m_scr[rows, :]
                m_new = jnp.maximum(m_old, m)
                a_old, a_blk = jnp.exp(m_old - m_new), jnp.exp(m - m_new)
                acc_scr[rows, :] = a_old * acc_scr[rows, :] + a_blk * acc
                l_scr[rows, :] = a_old * l_scr[rows, :] + a_blk * den
                m_scr[rows, :] = m_new
            return carry

        lax.fori_loop(0, dil * nb, block, 0, unroll=A_UNROLL)
    o_ref[0] = (acc_scr[...] / l_scr[...]).astype(o_ref.dtype)


def a_attention_prompt(z, batch, seq):
    nc = z.shape[1]
    d = nc // (3 * len(A_PATTERNS))
    n_heads = d // HEAD_DIM
    assert all(win // dil == A_BLOCK and seq % (dil * A_BLOCK) == 0 for win, dil in A_PATTERNS)
    zv = z.reshape(batch, seq, nc)
    in_specs = [pl.BlockSpec((1, seq, HEAD_DIM), functools.partial(lambda b, h, c: (b, 0, c * n_heads + h), c=c))
                for c in range(3 * len(A_PATTERNS))]
    out = pl.pallas_call(
        functools.partial(_a_prompt_kernel, seq=seq),
        out_shape=jax.ShapeDtypeStruct((batch, seq, d), BF16),
        grid=(batch, n_heads),
        in_specs=in_specs,
        out_specs=pl.BlockSpec((1, seq, HEAD_DIM), lambda b, h: (b, 0, h)),
        scratch_shapes=[pltpu.VMEM((seq, HEAD_DIM), F32)] * 3,
        compiler_params=_cparams(2),
        name="a_attn_prompt",
    )(*([zv] * (3 * len(A_PATTERNS))))
    return out.reshape(batch * seq, d)


def _a_sample_kernel(z_ref, *refs, n_heads, d):
    cache_refs, o_ref = refs[:-1], refs[-1]
    scale = HEAD_DIM ** -0.5
    for h in range(n_heads):
        ms, dens, accs = [], [], []
        for g in range(len(A_PATTERNS)):
            kv_ref = cache_refs[g]
            c0 = g * 3 * d + h * HEAD_DIM
            q = z_ref[0, :, c0:c0 + HEAD_DIM] * scale
            k_new = z_ref[0, :, c0 + d:c0 + d + HEAD_DIM]
            v_new = z_ref[0, :, c0 + 2 * d:c0 + 2 * d + HEAD_DIM]
            q8 = jnp.broadcast_to(q, (8, HEAD_DIM))
            s = _dot_nt(q8.astype(BF16), kv_ref[:, 0, h, :].astype(BF16))
            s_new = jnp.sum(q8 * k_new, axis=1, keepdims=True)
            m = jnp.maximum(jnp.max(s, axis=1, keepdims=True), s_new)
            p = jnp.exp(s - m)
            p_new = jnp.exp(s_new - m)
            dens.append(jnp.sum(p, axis=1, keepdims=True) + p_new)
            accs.append(_dot(p.astype(BF16), kv_ref[:, 1, h, :].astype(BF16)) + p_new * v_new)
            ms.append(m)
        mt = jnp.maximum(jnp.maximum(ms[0], ms[1]), ms[2])
        es = [jnp.exp(mm - mt) for mm in ms]
        tot = es[0] * dens[0] + es[1] * dens[1] + es[2] * dens[2]
        num = es[0] * accs[0] + es[1] * accs[1] + es[2] * accs[2]
        o_ref[0, :, h * HEAD_DIM:(h + 1) * HEAD_DIM] = num / tot


def a_attention_sample(z, caches, li, n_batch):
    d = z.shape[1] // 9
    n_heads = d // HEAD_DIM
    in_specs = [pl.BlockSpec((1, 1, z.shape[1]), lambda b: (b, 0, 0))]
    args = [z[:, None, :]]
    for (win, dil), cache in zip(A_PATTERNS, caches):
        lb = cache.shape[2]
        assert lb == win and (win // dil) == A_BLOCK
        in_specs.append(pl.BlockSpec((None, None, A_BLOCK, None, 2, n_heads, HEAD_DIM),
                                     lambda b: (li, b, 0, 0, 0, 0, 0)))
        args.append(cache.reshape(cache.shape[0], n_batch, lb // dil, dil, 2, n_heads, HEAD_DIM))
    out = pl.pallas_call(
        functools.partial(_a_sample_kernel, n_heads=d // HEAD_DIM, d=d),
        out_shape=jax.ShapeDtypeStruct((n_batch, 8, d), F32),
        grid=(n_batch,),
        in_specs=in_specs,
        out_specs=pl.BlockSpec((1, 8, d), lambda b: (b, 0, 0)),
        compiler_params=_cparams(1),
        name="a_attn_sample",
    )(*args)
    return out[:, 0]


def _b_in_kernel(x_ref, xs_ref, g_ref, wb_ref, wc_ref, wx_ref, bg_ref, u_ref, bgs_ref, us_ref, h_scr):
    tm = x_ref.shape[0]

    @pl.when(pl.program_id(1) == 0)
    def _():
        h_scr[0:tm, :] = _rms(x_ref[...], g_ref[...]).astype(BF16)
        h_scr[tm:, :] = _rms(xs_ref[...], g_ref[...]).astype(BF16)

    h = h_scr[...]
    bg = _dot(h, wb_ref[...].astype(BF16))
    u = _dot(h, wc_ref[...].astype(BF16)) * _dot(h, wx_ref[...].astype(BF16))
    bg_ref[...] = bg[0:tm]
    u_ref[...] = u[0:tm]
    bgs_ref[...] = bg[tm:]
    us_ref[...] = u[tm:]


def b_in(x, xs, gain, gi, w, li, *, tn, name):
    m, k = x.shape
    ms = xs.shape[0]
    d = w.shape[-1] // 3
    tm = _row_tile(m)
    nt = d // tn
    out_spec = pl.BlockSpec((tm, tn), lambda i, j: (i, j))
    bg, u, bg_s, u_s = pl.pallas_call(
        _b_in_kernel,
        out_shape=[jax.ShapeDtypeStruct((m, d), F32)] * 2 + [_dec_shape(m // tm, ms, d, F32)] * 2,
        grid=(m // tm, nt),
        in_specs=[
            pl.BlockSpec((tm, k), lambda i, j: (i, 0)),
            pl.BlockSpec((ms, k), lambda i, j: (0, 0)),
            pl.BlockSpec((None, 1, k), lambda i, j: (gi, 0, 0)),
            pl.BlockSpec((None, k, tn), lambda i, j: (li, 0, j)),
            pl.BlockSpec((None, k, tn), lambda i, j: (li, 0, j + nt)),
            pl.BlockSpec((None, k, tn), lambda i, j: (li, 0, j + 2 * nt)),
        ],
        out_specs=[out_spec, out_spec, _dec_spec(ms, tn), _dec_spec(ms, tn)],
        scratch_shapes=[pltpu.VMEM((tm + ms, k), BF16)],
        compiler_params=_cparams(2),
        name=name,
    )(x, xs, gain.reshape(gain.shape[0], 1, k), w, w, w)
    return bg, u, bg_s[0], u_s[0]


def _b_out_prompt_kernel(u_ref, up_ref, bg_ref, cw_ref, w_ref, r_ref, o_ref, ext_scr, a_scr, *, tiles_per_seq):
    i = pl.program_id(0)
    tm = u_ref.shape[0]

    @pl.when(pl.program_id(1) == 0)
    def _():
        ext_scr[0:8, :] = jnp.where(i % tiles_per_seq == 0, 0.0, up_ref[...])
        ext_scr[8:, :] = u_ref[...]
        y = (cw_ref[0:1, :] * ext_scr[pl.ds(6, tm), :] + cw_ref[1:2, :] * ext_scr[pl.ds(7, tm), :]
             + cw_ref[2:3, :] * ext_scr[pl.ds(8, tm), :])
        a_scr[...] = (bg_ref[...] * y).astype(BF16)

    o_ref[...] = r_ref[...] + _dot(a_scr[...], w_ref[...].astype(BF16))


def b_out_prompt(u, bg, conv_w, w, li, res, seq, *, tn, name):
    m, d = u.shape
    tm = 512
    assert seq % tm == 0
    return pl.pallas_call(
        functools.partial(_b_out_prompt_kernel, tiles_per_seq=seq // tm),
        out_shape=jax.ShapeDtypeStruct((m, d), F32),
        grid=(m // tm, d // tn),
        in_specs=[
            pl.BlockSpec((tm, d), lambda i, j: (i, 0)),
            pl.BlockSpec((8, d), lambda i, j: (jnp.maximum(i * (tm // 8) - 1, 0), 0)),
            pl.BlockSpec((tm, d), lambda i, j: (i, 0)),
            pl.BlockSpec((None, CONV_W, d), lambda i, j: (li, 0, 0)),
            pl.BlockSpec((None, d, tn), lambda i, j: (li, 0, j)),
            pl.BlockSpec((tm, tn), lambda i, j: (i, j)),
        ],
        out_specs=pl.BlockSpec((tm, tn), lambda i, j: (i, j)),
        scratch_shapes=[pltpu.VMEM((tm + 8, d), F32), pltpu.VMEM((tm, d), BF16)],
        compiler_params=_cparams(2),
        name=name,
    )(u, u, bg, conv_w, w, res)


def _b_out_sample_kernel(u_ref, um1_ref, um2_ref, bg_ref, cw_ref, w_ref, r_ref, o_ref):
    y = cw_ref[0:1, :] * um2_ref[...] + cw_ref[1:2, :] * um1_ref[...] + cw_ref[2:3, :] * u_ref[...]
    o_ref[...] = r_ref[...] + _dot((bg_ref[...] * y).astype(BF16), w_ref[...].astype(BF16))


def b_out_sample(u, um1, um2, bg, conv_w, w, li, res, *, tn, name):
    m, d = u.shape
    full = pl.BlockSpec((m, d), lambda j: (0, 0))
    return pl.pallas_call(
        _b_out_sample_kernel,
        out_shape=jax.ShapeDtypeStruct((m, d), F32),
        grid=(d // tn,),
        in_specs=[full, full, full, full,
                  pl.BlockSpec((None, CONV_W, d), lambda j: (li, 0, 0)),
                  pl.BlockSpec((None, d, tn), lambda j: (li, 0, j)),
                  pl.BlockSpec((m, tn), lambda j: (0, j))],
        out_specs=pl.BlockSpec((m, tn), lambda j: (0, j)),
        compiler_params=_cparams(1),
        name=name,
    )(u, um1, um2, bg, conv_w, w, res)


def _d_prompt_kernel(x_ref, xp_ref, g_ref, w_ref, sc_ref, o_ref, ht_ref, ext_scr, *, tiles_per_seq):
    i = pl.program_id(0)
    j = pl.program_id(1)
    tm = x_ref.shape[0]
    halo = D_HIST + 1
    gw = w_ref.shape[0]

    @pl.when(j == 0)
    def _():
        ext_scr[0:halo, :] = jnp.where(i % tiles_per_seq == 0, 0.0, _rms(xp_ref[...], g_ref[...]))
        ext_scr[halo:, :] = _rms(x_ref[...], g_ref[...])
        ht_ref[0] = ext_scr[pl.ds(tm, halo), :]

    pos = (i % tiles_per_seq) * tm + lax.broadcasted_iota(I32, (tm, 1), 0)
    for g, win in enumerate(D_POOLS):
        @pl.when(j == g)
        def _(g=g, win=win):
            cols = slice(g * gw, (g + 1) * gw)
            tot = ext_scr[pl.ds(halo, tm), cols]
            h = tot
            for back in range(1, win):
                tot = tot + ext_scr[pl.ds(halo - back, tm), cols]
            count = jnp.minimum(pos + 1, win).astype(F32)
            pooled = tot / count - h
            o_ref[...] = x_ref[:, cols] + _dot(pooled.astype(BF16), w_ref[...].astype(BF16)) * sc_ref[...]


def d_mixer_prompt(x, gain, li_norm, w_group, scale, li, seq, *, name):
    m, d = x.shape
    n_groups, gw = w_group.shape[1], w_group.shape[2]
    tm = 512
    halo = D_HIST + 1
    assert seq % tm == 0 and n_groups == len(D_POOLS)
    tps = seq // tm
    return pl.pallas_call(
        functools.partial(_d_prompt_kernel, tiles_per_seq=tps),
        out_shape=[jax.ShapeDtypeStruct((m, d), F32), jax.ShapeDtypeStruct((m // seq, halo, d), F32)],
        grid=(m // tm, n_groups),
        in_specs=[
            pl.BlockSpec((tm, d), lambda i, j: (i, 0)),
            pl.BlockSpec((halo, d), lambda i, j: (jnp.maximum(i * (tm // halo) - 1, 0), 0)),
            pl.BlockSpec((None, 1, d), lambda i, j: (li_norm, 0, 0)),
            pl.BlockSpec((None, None, gw, gw), lambda i, j: (li, j, 0, 0)),
            pl.BlockSpec((None, 1, gw), lambda i, j: (li, 0, j)),
        ],
        out_specs=[pl.BlockSpec((tm, gw), lambda i, j: (i, j)),
                   pl.BlockSpec((1, halo, d), lambda i, j: (i // tps, 0, 0))],
        scratch_shapes=[pltpu.VMEM((tm + halo, d), F32)],
        compiler_params=_cparams(2),
        name=name,
    )(x, x, gain.reshape(gain.shape[0], 1, d), w_group, scale.reshape(scale.shape[0], 1, d))


def _d_sample_kernel(x_ref, hist_ref, g_ref, w_ref, sc_ref, o_ref, h_ref):
    h = _rms(x_ref[...], g_ref[...])
    h_ref[...] = h
    gw = w_ref.shape[1]
    for g, win in enumerate(D_POOLS):
        cols = slice(g * gw, (g + 1) * gw)
        tot = h[:, cols]
        for back in range(1, win):
            tot = tot + hist_ref[D_HIST - back, :, cols]
        pooled = tot / float(win) - h[:, cols]
        o_ref[:, cols] = x_ref[:, cols] + _dot(pooled.astype(BF16), w_ref[g].astype(BF16)) * sc_ref[:, cols]


def d_mixer_sample(x, hist_t, gain, li_norm, w_group, scale, li, *, name):
    m, d = x.shape
    n_groups, gw = w_group.shape[1], w_group.shape[2]
    assert hist_t.shape[0] == D_HIST
    return pl.pallas_call(
        _d_sample_kernel,
        out_shape=[jax.ShapeDtypeStruct((m, d), F32)] * 2,
        grid=(1,),
        in_specs=[
            pl.BlockSpec((m, d), lambda i: (0, 0)),
            pl.BlockSpec(hist_t.shape, lambda i: (0, 0, 0)),
            pl.BlockSpec((None, 1, d), lambda i: (li_norm, 0, 0)),
            pl.BlockSpec((None, n_groups, gw, gw), lambda i: (li, 0, 0, 0)),
            pl.BlockSpec((None, 1, d), lambda i: (li, 0, 0)),
        ],
        out_specs=[pl.BlockSpec((m, d), lambda i: (0, 0))] * 2,
        compiler_params=_cparams(1),
        name=name,
    )(x, hist_t, gain.reshape(gain.shape[0], 1, d), w_group, scale.reshape(scale.shape[0], 1, d))


def _cmp_fs_kernel(pt_ref, page_ref, w_ref, o_ref, ring_scr):
    del pt_ref
    slot = pl.program_id(1) % PAGES_PER_GROUP
    rows = page_ref.shape[1]
    for ck in range(2 * C_KV_HEADS):
        ring_scr[ck, pl.ds(pl.multiple_of(slot * rows, rows), rows), :] = page_ref[0, :, ck * HEAD_DIM:(ck + 1) * HEAD_DIM]

    @pl.when(slot == PAGES_PER_GROUP - 1)
    def _():
        n_chunks = PAGES_PER_GROUP * rows // C_CMP_STRIDE
        for ck in range(2 * C_KV_HEADS):
            c = ck // C_KV_HEADS
            acc = jnp.zeros((n_chunks, w_ref.shape[-1]), F32)
            for pp in range(C_CMP_STRIDE // 2):
                lhs = jnp.concatenate([ring_scr[ck, pl.ds(2 * pp + i, n_chunks, stride=C_CMP_STRIDE), :]
                                       for i in range(2)], axis=1)
                acc = acc + _dot(lhs.astype(BF16), w_ref[c, pp])
            o_ref[0, ck] = acc


def cmp_first_second(pool, page_ids, col_block, w_cat, n_batch, n_pages):
    rows = pool.shape[1]
    width = 2 * C_KV_HEADS * HEAD_DIM
    assert n_pages % PAGES_PER_GROUP == 0 and rows % C_CMP_STRIDE == 0
    chunks_per_group = PAGES_PER_GROUP * rows // C_CMP_STRIDE
    n_out = w_cat.shape[-1]
    return pl.pallas_call(
        _cmp_fs_kernel,
        out_shape=jax.ShapeDtypeStruct((n_batch, 2 * C_KV_HEADS, n_pages * rows // C_CMP_STRIDE, n_out), F32),
        grid_spec=pltpu.PrefetchScalarGridSpec(
            num_scalar_prefetch=1,
            grid=(n_batch, n_pages),
            in_specs=[
                pl.BlockSpec((1, rows, width), lambda b, pg, pt: (pt[b * n_pages + pg], 0, col_block)),
                pl.BlockSpec(w_cat.shape, lambda b, pg, pt: (0, 0, 0, 0)),
            ],
            out_specs=pl.BlockSpec((1, 2 * C_KV_HEADS, chunks_per_group, n_out),
                                   lambda b, pg, pt: (b, 0, pg // PAGES_PER_GROUP, 0)),
            scratch_shapes=[pltpu.VMEM((width // HEAD_DIM, PAGES_PER_GROUP * rows, HEAD_DIM), F32)],
        ),
        compiler_params=_cparams(2),
        name="c_cmp_first_second",
    )(page_ids, pool, w_cat)


def _cmp_fs_paged_kernel(pt_ref, *refs):
    del pt_ref
    page_refs, (w_ref, o_ref, acc_scr) = refs[:PAGES_PER_GROUP], refs[PAGES_PER_GROUP:]
    chunks_per_page = page_refs[0].shape[0] // C_CMP_STRIDE
    n_chunks = PAGES_PER_GROUP * chunks_per_page
    n_out = w_ref.shape[-1]

    def chunk_rows(p, c):
        return jnp.concatenate([pr[pl.ds(p, chunks_per_page, stride=C_CMP_STRIDE), c, :, :]
                                .reshape(chunks_per_page * C_KV_HEADS, HEAD_DIM) for pr in page_refs], axis=0)

    for c in range(2):
        acc = jnp.zeros((n_chunks * C_KV_HEADS, n_out), F32)
        for pp in range(C_CMP_STRIDE // 2):
            lhs = jnp.concatenate([chunk_rows(2 * pp, c), chunk_rows(2 * pp + 1, c)], axis=1)
            acc = acc + _dot(lhs.astype(BF16), w_ref[c, pp])
        for j in range(n_out // LANES):
            acc_scr[j] = acc[:, j * LANES:(j + 1) * LANES]
        for k in range(C_KV_HEADS):
            for j in range(n_out // LANES):
                o_ref[0, c * C_KV_HEADS + k, :, j * LANES:(j + 1) * LANES] = (
                    acc_scr[j, pl.ds(k, n_chunks, stride=C_KV_HEADS), :])


def cmp_first_second_paged(pool, li, page_table, w_cat):
    n_batch, n_pages = page_table.shape
    rows = pool.shape[2]
    assert n_pages % PAGES_PER_GROUP == 0 and rows % C_CMP_STRIDE == 0
    chunks_per_group = PAGES_PER_GROUP * rows // C_CMP_STRIDE
    n_out = w_cat.shape[-1]

    def page_spec(s):
        return pl.BlockSpec((None, None, rows, 2, C_KV_HEADS, HEAD_DIM),
                            lambda b, grp, pt: (li, pt[b * n_pages + grp * PAGES_PER_GROUP + s], 0, 0, 0, 0))

    return pl.pallas_call(
        _cmp_fs_paged_kernel,
        out_shape=jax.ShapeDtypeStruct((n_batch, 2 * C_KV_HEADS, n_pages * rows // C_CMP_STRIDE, n_out), F32),
        grid_spec=pltpu.PrefetchScalarGridSpec(
            num_scalar_prefetch=1,
            grid=(n_batch, n_pages // PAGES_PER_GROUP),
            in_specs=[page_spec(s) for s in range(PAGES_PER_GROUP)]
            + [pl.BlockSpec(w_cat.shape, lambda b, grp, pt: (0, 0, 0, 0))],
            out_specs=pl.BlockSpec((1, 2 * C_KV_HEADS, chunks_per_group, n_out), lambda b, grp, pt: (b, 0, grp, 0)),
            scratch_shapes=[pltpu.VMEM((n_out // LANES, chunks_per_group * C_KV_HEADS, LANES), F32)],
        ),
        compiler_params=_cparams(2),
        name="c_cmp_first_second_paged",
    )(page_table.reshape(-1), *([pool] * PAGES_PER_GROUP), w_cat)


def _cmp_pe_kernel(pe_ref, w1_ref, o_ref):
    o_ref[0] = _dot(pe_ref[0].astype(BF16), w1_ref[0].astype(BF16))


def cmp_pe_term(pe8, w1_flat):
    return pl.pallas_call(
        _cmp_pe_kernel,
        out_shape=jax.ShapeDtypeStruct((pe8.shape[0], pe8.shape[1], w1_flat.shape[-1]), F32),
        grid=(pe8.shape[0],),
        in_specs=[pl.BlockSpec((1,) + pe8.shape[1:], lambda c: (c, 0, 0)),
                  pl.BlockSpec((1,) + w1_flat.shape[1:], lambda c: (c, 0, 0))],
        out_specs=pl.BlockSpec((1, pe8.shape[1], w1_flat.shape[-1]), lambda c: (c, 0, 0)),
        compiler_params=_cparams(1),
        name="c_cmp_pe_term",
    )(pe8, w1_flat)


def _cmp_mlp_kernel(fs_ref, pe_ref, w2_ref, kg_ref, o_ref):
    ck = pl.program_id(1)
    nch = fs_ref.shape[2]
    hid_w = w2_ref.shape[1]
    pe_term = pe_ref[0, 0:1, :]
    first = fs_ref[0, 0, :, 0:hid_w]
    second = fs_ref[0, 0, :, hid_w:2 * hid_w]
    nxt = pltpu.roll(second, nch - 1, 0)
    last = lax.broadcasted_iota(I32, (nch, 1), 0) == nch - 1
    pre = first + jnp.where(last, 0.0, nxt) + pe_term
    hid = pre * jax.nn.sigmoid(pre)
    out = _dot(hid.astype(BF16), w2_ref[0].astype(BF16))

    @pl.when(ck < C_KV_HEADS)
    def _():
        o_ref[0, 0] = _rms(out, kg_ref[...])

    @pl.when(ck >= C_KV_HEADS)
    def _():
        o_ref[0, 0] = out


def cmp_mlp(fs, pe_term, w2, k_gain0):
    n_batch, n_ck, nch, _ = fs.shape
    return pl.pallas_call(
        _cmp_mlp_kernel,
        out_shape=jax.ShapeDtypeStruct((n_batch, n_ck, nch, HEAD_DIM), F32),
        grid=(n_batch, n_ck),
        in_specs=[
            pl.BlockSpec((1, 1, nch, fs.shape[-1]), lambda b, ck: (b, ck, 0, 0)),
            pl.BlockSpec((1,) + pe_term.shape[1:], lambda b, ck: (ck // C_KV_HEADS, 0, 0)),
            pl.BlockSpec((1,) + w2.shape[1:], lambda b, ck: (ck // C_KV_HEADS, 0, 0)),
            pl.BlockSpec((1, HEAD_DIM), lambda b, ck: (0, 0)),
        ],
        out_specs=pl.BlockSpec((1, 1, nch, HEAD_DIM), lambda b, ck: (b, ck, 0, 0)),
        compiler_params=_cparams(2),
        name="c_cmp_mlp",
    )(fs, pe_term, w2, k_gain0)


def _masked_softmax_rows(s, ok):
    s = jnp.where(ok, s, NEG)
    m = jnp.max(s, axis=1, keepdims=True)
    e = jnp.where(ok, jnp.exp(s - m), 0.0)
    return e, jnp.maximum(jnp.sum(e, axis=1, keepdims=True), 1e-30)


def _nsa_prompt_kernel(q_ref, gate_ref, cos_ref, sin_ref, kc_ref, vc_ref, ks_ref, vs_ref, kw_ref, vw_ref,
                       ov_ref, ex_ref, o_ref, s_scr, mx_scr, ext_scr, *, nsel):
    qb = q_ref.shape[0]
    nch = kc_ref.shape[2]
    tk = s_scr.shape[2]
    t0 = pl.program_id(2) * qb

    def stacked_bias(ok):
        return jnp.concatenate([jnp.where(ok, 0.0, NEG)] * C_GROUP, axis=0)

    def t_rows(n):
        return t0 + lax.broadcasted_iota(I32, (qb, n), 0)

    qscale = HEAD_DIM ** -0.5 * LOG2_E
    qs = [q_ref[:, g * HEAD_DIM:(g + 1) * HEAD_DIM] for g in range(C_GROUP)]
    qn4 = jnp.concatenate([(q * qscale).astype(BF16) for q in qs], axis=0)
    qr4 = jnp.concatenate([(_rope(q, cos_ref[...], sin_ref[...]) * qscale).astype(BF16) for q in qs], axis=0)

    wk = C_WINDOW + qb
    start = pl.multiple_of(jnp.maximum(t0 - C_WINDOW, 0), qb)
    dist = t_rows(wk) - (start + lax.broadcasted_iota(I32, (qb, wk), 1))
    s = _dot_nt(qr4, kw_ref[0, pl.ds(start, wk), :].astype(BF16)) + stacked_bias((dist >= 0) & (dist < C_WINDOW))
    p = jnp.exp2(s - jnp.max(s, axis=1, keepdims=True)).astype(BF16)
    ext = _dot(p, _with_ones(vw_ref[0, pl.ds(start, wk), :]))
    o_win4 = ext[:, :HEAD_DIM] / ext[:, HEAD_DIM:]

    cmp_bias = stacked_bias(lax.broadcasted_iota(I32, (qb, nch), 1) * C_CMP_STRIDE + (C_CMP_LEN - 1) <= t_rows(nch))
    s = _dot_nt(qn4, kc_ref[0, 0].astype(BF16)) + cmp_bias
    e = jnp.where(cmp_bias < 0.0, 0.0, jnp.exp2(s - jnp.max(s, axis=1, keepdims=True)))
    ext = _dot(e.astype(BF16), _with_ones(vc_ref[0, 0]))
    den = jnp.maximum(ext[:, HEAD_DIM:], 1e-30)
    o_cmp4 = ext[:, :HEAD_DIM] / den
    p = e / den
    p_sum = p[0:qb]
    for g in range(1, C_GROUP):
        p_sum = p_sum + p[g * qb:(g + 1) * qb]

    p_hi = p_sum.astype(BF16)
    p_lo = (p_sum - p_hi.astype(F32)).astype(BF16)
    nr = -(-nsel // 8) * 8
    imp = (_dot_nt(ov_ref[...], p_hi) + _dot_nt(ov_ref[...], p_lo))[:nr]
    blk = lax.broadcasted_iota(I32, (nr, qb), 0)
    cur = (t0 + lax.broadcasted_iota(I32, (nr, qb), 1)) // C_SEL_BLOCK
    causal = (blk <= cur) & (blk < nsel)
    forced = (blk == 0) | (blk == cur) | (blk == cur - 1)
    imp = jnp.where(causal, jnp.where(forced, C_FORCE_SCORE, imp), NEG)
    rank = jnp.zeros((nr, qb), I32)
    for jp in range(nsel):
        c = jnp.broadcast_to(imp[jp:jp + 1, :], (nr, qb))
        beats = (c > imp) | ((c == imp) & (blk > jp))
        rank = rank + beats.astype(I32)
    sel_t = jnp.where((rank < C_SEL_TOPN) & causal, 1.0, 0.0)
    sel_t = jnp.concatenate([sel_t, jnp.zeros((LANES - nr, qb), F32)], axis=0)
    sel = sel_t.T.astype(BF16)

    n_tiles = (t0 + qb + tk - 1) // tk
    mx_scr[...] = jnp.full(mx_scr.shape, NEG, F32)

    def score_tile(kt, carry):
        k0 = pl.multiple_of(kt * tk, tk)
        sel_keys = _dot(sel, ex_ref[kt])
        ok = (sel_keys > 0.5) & (k0 + lax.broadcasted_iota(I32, (qb, tk), 1) <= t_rows(tk))
        s = _dot_nt(qr4, ks_ref[0, pl.ds(k0, tk), :].astype(BF16)) + stacked_bias(ok)
        s_scr[kt] = s
        mx = mx_scr[...]
        for c in range(tk // LANES):
            mx = jnp.maximum(mx, s[:, c * LANES:(c + 1) * LANES])
        mx_scr[...] = mx
        return carry

    lax.fori_loop(0, n_tiles, score_tile, 0)
    m_slc = jnp.max(mx_scr[...], axis=1, keepdims=True)

    ext_scr[...] = jnp.zeros(ext_scr.shape, F32)

    def value_tile(kt, carry):
        k0 = pl.multiple_of(kt * tk, tk)
        p = jnp.exp2(s_scr[kt] - m_slc).astype(BF16)
        ext_scr[...] += _dot(p, _with_ones(vs_ref[0, pl.ds(k0, tk), :]))
        return carry

    lax.fori_loop(0, n_tiles, value_tile, 0)
    o_slc4 = ext_scr[:, :HEAD_DIM] / ext_scr[:, HEAD_DIM:]

    for g in range(C_GROUP):
        rows = slice(g * qb, (g + 1) * qb)
        o = (gate_ref[:, 3 * g:3 * g + 1] * o_cmp4[rows] + gate_ref[:, 3 * g + 1:3 * g + 2] * o_slc4[rows]
             + gate_ref[:, 3 * g + 2:3 * g + 3] * o_win4[rows])
        o_ref[:, g * HEAD_DIM:(g + 1) * HEAD_DIM] = o.astype(o_ref.dtype)


def nsa_prompt(z, gates, cos, sin, kcv, overlap_t, expand, batch, seq):
    m, nc = z.shape
    d = C_KV_HEADS * C_GROUP * HEAD_DIM
    qb = C_QBLOCK
    nq = seq // qb
    nsel = -(-seq // C_SEL_BLOCK)
    tk = C_KEY_TILE
    assert nsel <= LANES and seq % qb == 0 and seq >= C_WINDOW + qb and seq % tk == 0
    assert expand.shape == (seq // tk, LANES, tk)
    nch = kcv.shape[2]
    assert nch == HEAD_DIM and qb == LANES and overlap_t.shape == (LANES, nch)
    zv = z.reshape(batch, seq, nc)
    cb = d // HEAD_DIM
    kvw = C_KV_HEADS

    def kv_spec(branch, kv):
        off = cb + (branch * 2 + kv) * kvw
        return pl.BlockSpec((1, seq, HEAD_DIM), lambda b, k, q: (b, 0, off + k))

    return pl.pallas_call(
        functools.partial(_nsa_prompt_kernel, nsel=nsel),
        out_shape=jax.ShapeDtypeStruct((m, d), BF16),
        scratch_shapes=[pltpu.VMEM((seq // tk, C_GROUP * qb, tk), F32),
                        pltpu.VMEM((C_GROUP * qb, LANES), F32),
                        pltpu.VMEM((C_GROUP * qb, 2 * HEAD_DIM), F32)],
        grid=(batch, C_KV_HEADS, nq),
        in_specs=[
            pl.BlockSpec((qb, C_GROUP * HEAD_DIM), lambda b, k, q: (b * nq + q, k)),
            pl.BlockSpec((qb, LANES), lambda b, k, q: (b * nq + q, k)),
            pl.BlockSpec((qb, HEAD_DIM), lambda b, k, q: (q, 0)),
            pl.BlockSpec((qb, HEAD_DIM), lambda b, k, q: (q, 0)),
            pl.BlockSpec((1, 1, nch, HEAD_DIM), lambda b, k, q: (b, k, 0, 0)),
            pl.BlockSpec((1, 1, nch, HEAD_DIM), lambda b, k, q: (b, C_KV_HEADS + k, 0, 0)),
            kv_spec(1, 0), kv_spec(1, 1), kv_spec(2, 0), kv_spec(2, 1),
            pl.BlockSpec(overlap_t.shape, lambda b, k, q: (0, 0)),
            pl.BlockSpec(expand.shape, lambda b, k, q: (0, 0, 0)),
        ],
        out_specs=pl.BlockSpec((qb, C_GROUP * HEAD_DIM), lambda b, k, q: (b * nq + q, k)),
        compiler_params=_cparams(3),
        name="c_nsa_prompt",
    )(z, gates, cos, sin, kcv, kcv, zv, zv, zv, zv, overlap_t, expand)


def _nsa_sample_cmp_kernel(q_ref, kc_ref, vc_ref, ov_ref, ocmp_ref, idx_ref, *, t, nsel):
    nch = kc_ref.shape[2]
    scale = HEAD_DIM ** -0.5
    cmp_ok = lax.broadcasted_iota(I32, (8, nch), 1) * C_CMP_STRIDE + (C_CMP_LEN - 1) <= t
    row = lax.broadcasted_iota(I32, (8, nch), 0)
    p_sum = jnp.zeros((8, nch), F32)
    for k in range(C_KV_HEADS):
        q = (q_ref[0, k] * scale).astype(BF16)
        e, den = _masked_softmax_rows(_dot_nt(q, kc_ref[0, k].astype(BF16)), cmp_ok)
        p = e / den
        ocmp_ref[0, k] = _dot(p.astype(BF16), vc_ref[0, k].astype(BF16))
        p_k = jnp.sum(jnp.where(row < C_GROUP, p, 0.0), axis=0, keepdims=True)
        p_sum = jnp.where(row == k, jnp.broadcast_to(p_k, (8, nch)), p_sum)
    p_hi = p_sum.astype(BF16)
    p_lo = (p_sum - p_hi.astype(F32)).astype(BF16)
    imp = _dot(p_hi, ov_ref[...]) + _dot(p_lo, ov_ref[...])
    width = imp.shape[1]
    lane = lax.broadcasted_iota(I32, (8, width), 1)
    cur = t // C_SEL_BLOCK
    causal = (lane <= cur) & (lane < nsel)
    forced = (lane == 0) | (lane == cur) | (lane == cur - 1)
    work = jnp.where(causal, jnp.where(forced, C_FORCE_SCORE, imp), NEG)
    lane_f = lane.astype(F32)
    out_lane = lax.broadcasted_iota(I32, (8, LANES), 1)
    idx = jnp.full((8, LANES), -1, I32)
    for r in range(C_SEL_TOPN):
        best = jnp.max(work, axis=1, keepdims=True)
        pick = jnp.min(jnp.where(work == best, lane_f, float(width)), axis=1, keepdims=True)
        found = jnp.where(best > 0.5 * NEG, pick, -1.0).astype(I32)
        idx = jnp.where(out_lane == r, found, idx)
        work = jnp.where(lane_f == pick, NEG, work)
    idx_ref[0] = idx


def nsa_sample_cmp(q8, kcv, overlap, t, nsel):
    n_batch = q8.shape[0]
    nch = kcv.shape[2]
    assert C_KV_HEADS <= 8
    blk = pl.BlockSpec((1, C_KV_HEADS, 8, HEAD_DIM), lambda b: (b, 0, 0, 0))
    return pl.pallas_call(
        functools.partial(_nsa_sample_cmp_kernel, t=t, nsel=nsel),
        out_shape=[jax.ShapeDtypeStruct((n_batch, C_KV_HEADS, 8, HEAD_DIM), F32),
                   jax.ShapeDtypeStruct((n_batch, 8, LANES), I32)],
        grid=(n_batch,),
        in_specs=[
            blk,
            pl.BlockSpec((1, C_KV_HEADS, nch, HEAD_DIM), lambda b: (b, 0, 0, 0)),
            pl.BlockSpec((1, C_KV_HEADS, nch, HEAD_DIM), lambda b: (b, 1, 0, 0)),
            pl.BlockSpec(overlap.shape, lambda b: (0, 0)),
        ],
        out_specs=[blk, pl.BlockSpec((1, 8, LANES), lambda b: (b, 0, 0))],
        compiler_params=_cparams(1),
        name="c_nsa_sample_cmp",
    )(q8, kcv, kcv, overlap)


def _nsa_sample_kernel(idx_ref, pt_ref, q_ref, cos_ref, sin_ref, *refs, t, n_past_blocks):
    del pt_ref
    slc_refs, (new_ref, win_ref, ocmp_ref, gate_ref, o_ref) = refs[:C_SEL_TOPN], refs[C_SEL_TOPN:]
    b, k = pl.program_id(0), pl.program_id(1)

    def head_rows(ref, kv):
        return ref[:, kv, pl.ds(k, 1), :][:, 0, :].astype(BF16)

    base = (b * C_KV_HEADS + k) * C_SEL_TOPN
    scale = HEAD_DIM ** -0.5
    qr_f = _rope(q_ref[0, 0], cos_ref[...], sin_ref[...]) * scale
    qr = qr_f.astype(BF16)

    nk = C_SEL_TOPN * C_SEL_BLOCK
    s = _dot_nt(qr, jnp.concatenate([head_rows(r, 0) for r in slc_refs], axis=0))
    lane = lax.broadcasted_iota(I32, (8, nk), 1)
    blk = jnp.full((8, nk), -1, I32)
    has_new = False
    for n in range(C_SEL_TOPN):
        blk_n = idx_ref[base + n]
        blk = jnp.where((lane >= n * C_SEL_BLOCK) & (lane < (n + 1) * C_SEL_BLOCK), blk_n, blk)
        has_new = has_new | (blk_n == n_past_blocks)
    ok = (blk >= 0) & (blk < n_past_blocks) & (blk * C_SEL_BLOCK + (lane & (C_SEL_BLOCK - 1)) <= t)
    k_new, v_new = new_ref[0, 0, 0:1, :], new_ref[0, 0, 1:2, :]
    s = jnp.where(ok, s, NEG)
    s_new = jnp.where(has_new, jnp.sum(qr_f * k_new, axis=1, keepdims=True), NEG)
    m = jnp.maximum(jnp.max(s, axis=1, keepdims=True), s_new)
    p = jnp.where(ok, jnp.exp(s - m), 0.0)
    p_new = jnp.where(has_new, jnp.exp(s_new - m), 0.0)
    den = jnp.maximum(jnp.sum(p, axis=1, keepdims=True) + p_new, 1e-30)
    vs = jnp.concatenate([head_rows(r, 1) for r in slc_refs], axis=0)
    o_slc = (_dot(p.astype(BF16), vs) + p_new * v_new) / den

    lw = win_ref.shape[0]
    kw_new, vw_new = new_ref[0, 0, 2:3, :], new_ref[0, 0, 3:4, :]
    dist = lw - lax.broadcasted_iota(I32, (8, lw), 1)
    w_ok = (dist >= 0) & (dist < C_WINDOW)
    s = jnp.where(w_ok, _dot_nt(qr, head_rows(win_ref, 0)), NEG)
    s_new = jnp.sum(qr_f * kw_new, axis=1, keepdims=True)
    m = jnp.maximum(jnp.max(s, axis=1, keepdims=True), s_new)
    p = jnp.where(w_ok, jnp.exp(s - m), 0.0)
    p_new = jnp.exp(s_new - m)
    den = jnp.sum(p, axis=1, keepdims=True) + p_new
    o_win = (_dot(p.astype(BF16), head_rows(win_ref, 1)) + p_new * vw_new) / den

    gate = gate_ref[0, 0]
    o_ref[0, 0] = gate[:, 0:1] * ocmp_ref[0, 0] + gate[:, 1:2] * o_slc + gate[:, 2:3] * o_win


def nsa_sample(top_idx, page_table, q8, cos, sin, pool_slc, win_buf, li, new_kv, ocmp, gate8, t, n_past_blocks):
    n_batch = q8.shape[0]
    n_pages = page_table.shape[1]
    n_layers, n_pool, page_rows = pool_slc.shape[:3]
    blocks_per_page = page_rows // C_SEL_BLOCK
    kv_dims = (2, C_KV_HEADS, HEAD_DIM)
    pool = pool_slc.reshape((n_layers, n_pool, blocks_per_page, C_SEL_BLOCK) + kv_dims)
    lw = win_buf.shape[2]

    def slc_spec(n):
        def index(b, k, idx, pt):
            blk = jnp.clip(idx[(b * C_KV_HEADS + k) * C_SEL_TOPN + n], 0, n_past_blocks - 1)
            return (li, pt[b * n_pages + blk // blocks_per_page], blk % blocks_per_page, 0, 0, 0, 0)
        return pl.BlockSpec((None, None, None, C_SEL_BLOCK) + kv_dims, index)

    win_spec = pl.BlockSpec((None, None, lw) + kv_dims, lambda b, k, idx, pt: (li, b, 0, 0, 0, 0))
    blk8 = pl.BlockSpec((1, 1, 8, HEAD_DIM), lambda b, k, idx, pt: (b, k, 0, 0))
    tab = pl.BlockSpec((8, HEAD_DIM), lambda b, k, idx, pt: (0, 0))
    return pl.pallas_call(
        functools.partial(_nsa_sample_kernel, t=t, n_past_blocks=n_past_blocks),
        out_shape=jax.ShapeDtypeStruct((n_batch, C_KV_HEADS, 8, HEAD_DIM), F32),
        grid_spec=pltpu.PrefetchScalarGridSpec(
            num_scalar_prefetch=2,
            grid=(n_batch, C_KV_HEADS),
            in_specs=[blk8, tab, tab] + [slc_spec(n) for n in range(C_SEL_TOPN)] + [blk8, win_spec, blk8, blk8],
            out_specs=blk8,
        ),
        compiler_params=_cparams(2),
        name="c_nsa_sample",
    )(top_idx.reshape(-1), page_table.reshape(-1), q8, cos, sin, *([pool] * C_SEL_TOPN), new_kv, win_buf, ocmp, gate8)


def _shift_append_kernel(cur_ref, nxt_ref, new_ref, out_ref):
    r = cur_ref.shape[0]
    out_ref[0:r - 1] = cur_ref[1:r]
    last = pl.program_id(1) == pl.num_programs(1) - 1

    @pl.when(last)
    def _():
        out_ref[r - 1:r] = new_ref[...]

    @pl.when(jnp.logical_not(last))
    def _():
        out_ref[r - 1:r] = nxt_ref[...]


def shift_append(cache, li, new, name):
    n_batch, length = cache.shape[1:3]
    tail = cache.shape[3:]
    assert new.shape == (n_batch, 1) + tail
    r = min(length, SHIFT_ROWS)
    assert length % r == 0
    zeros = (0,) * len(tail)
    return pl.pallas_call(
        _shift_append_kernel,
        out_shape=jax.ShapeDtypeStruct(cache.shape[1:], cache.dtype),
        grid=(n_batch, length // r),
        in_specs=[
            pl.BlockSpec((None, None, r) + tail, lambda b, i: (li, b, i) + zeros),
            pl.BlockSpec((None, None, 1) + tail, lambda b, i: (li, b, jnp.minimum((i + 1) * r, length - 1)) + zeros),
            pl.BlockSpec((None, 1) + tail, lambda b, i: (b, 0) + zeros),
        ],
        out_specs=pl.BlockSpec((None, r) + tail, lambda b, i: (b, i) + zeros),
        compiler_params=_cparams(2),
        name=name,
    )(cache, cache, new)


def _rope_tables(pos):
    half = HEAD_DIM // 2
    inv = ROPE_THETA ** (-(jnp.arange(half, dtype=F32) * 2.0 / HEAD_DIM))
    ang = pos.astype(F32)[:, None] * inv[None, :]
    cos, sin = jnp.cos(ang), jnp.sin(ang)
    return jnp.concatenate([cos, cos], axis=1), jnp.concatenate([-sin, sin], axis=1)


def _overlap_matrix(nch, nsel, width):
    cmp_start = jnp.arange(nch) * C_CMP_STRIDE
    sel_start = jnp.arange(width) * C_SEL_BLOCK
    ov = ((cmp_start[:, None] < sel_start[None, :] + C_SEL_BLOCK)
          & (cmp_start[:, None] + C_CMP_LEN > sel_start[None, :])
          & (jnp.arange(width)[None, :] < nsel))
    return ov.astype(BF16)


def _c_weights(prm, li, n_heads):
    d = n_heads * HEAD_DIM
    nkv = 3 * 2 * C_KV_HEADS * HEAD_DIM
    k_gain = prm['c_k_norm'][li]
    ones = jnp.ones((C_KV_HEADS * HEAD_DIM,), F32)
    head_gain = jnp.concatenate([
        jnp.tile(prm['c_q_norm'][li], n_heads), ones, ones,
        jnp.tile(k_gain[1], C_KV_HEADS), ones, jnp.tile(k_gain[2], C_KV_HEADS), ones])[None, :]
    wg = prm['w_c_in'][li][:, d + nkv:].reshape(-1, C_KV_HEADS, C_GROUP * 3)
    wg = jnp.pad(wg, ((0, 0), (0, 0), (0, LANES - C_GROUP * 3))).reshape(1, -1, C_KV_HEADS * LANES)
    w1 = prm['c_cmp_w1'][li]
    w_cat = jnp.concatenate([w1[:, :C_CMP_STRIDE], w1[:, C_CMP_STRIDE:]], axis=-1).astype(BF16)
    w_cat = w_cat.reshape(2, C_CMP_STRIDE // 2, 2 * HEAD_DIM, w_cat.shape[-1])
    pe8 = jnp.pad(prm['c_cmp_pe'][li].reshape(2, 1, -1), ((0, 0), (0, 7), (0, 0)))
    w1_flat = w1.reshape(2, C_CMP_LEN * HEAD_DIM, -1)
    return head_gain, wg, w_cat, pe8, w1_flat, prm['c_cmp_w2'][li], k_gain[0][None, :]


TN = 512
TN_WIDE_K = 256
TM_OUT_PROJ = 2048
C_MODES = (NORM,) * 4 + (RAW, RAW, NORM_ROPE, RAW, NORM_ROPE, RAW)
A_MODES = ((NORM_ROPE,) * 8 + (RAW,) * 4) * len(A_PATTERNS)


def _pad_rows(a, rows):
    return jnp.pad(a, ((0, rows - a.shape[0]),) + ((0, 0),) * (a.ndim - 1))


def _a_head_gain(prm, li, n_heads):
    ones = jnp.ones((n_heads * HEAD_DIM,), F32)
    parts = []
    for g in range(len(A_PATTERNS)):
        parts += [jnp.tile(prm['a_q_norm'][li, g], n_heads), jnp.tile(prm['a_k_norm'][li, g], n_heads), ones]
    return jnp.concatenate(parts)[None, :]


def _channel_mixer(x, p, prm, i, tag):
    act = yield ('swiglu_in', x, dict(gain=prm['ffn_norm'], w=prm['w_ffn_in'], li=i, tn=TN, name="ffn_in"))
    x = linear_residual(act, prm['w_ffn_out'], i, x, tn=TN_WIDE_K, name=f"ffn_out_{tag}")
    x = yield ('ple', x, dict(gain=prm['ple_norm'], wp=prm['w_ple_proj'], wg=prm['w_ple_gate'], li=i, tn=TN,
                              name="ple"), p)
    return x


def _serve(req_p, req_s):
    kind, x, kw = req_p[:3]
    xs = req_s[1]
    assert kind == req_s[0]
    if kind == 'norm_linear':
        return norm_linear(x, xs, rope=req_p[3], rope_s=req_s[3], **kw)
    if kind == 'swiglu_in':
        return swiglu_in(x, xs, **kw)
    if kind == 'ple':
        return ple(x, xs, p=req_p[3], ps=req_s[3], **kw)
    bg, u, bg_s, u_s = b_in(x, xs, **kw)
    return (bg, u), (bg_s, u_s)


def _run_both(walk_p, walk_s):
    req_p, req_s = next(walk_p), next(walk_s)
    while True:
        out_p, out_s = _serve(req_p, req_s)
        try:
            req_p = walk_p.send(out_p)
        except StopIteration as done_p:
            try:
                walk_s.send(out_s)
            except StopIteration as done_s:
                return done_p.value, done_s.value
            raise AssertionError("decode walk outlived the prompt walk")
        req_s = walk_s.send(out_s)


def _run_prompt(x3, p, prm):
    batch, seq, d = x3.shape
    n_heads = d // HEAD_DIM
    x = x3.reshape(batch * seq, d)
    cos, sin = _rope_tables(jnp.arange(seq))
    new = {}
    depth = prm['attn_norm'].shape[0]
    for i in range(depth):
        kind, li = i % 4, i // 4
        if kind == 0:
            z = yield ('norm_linear', x, dict(gain=prm['attn_norm'], gi=i, w=prm['w_a_qkv'], wi=li, col0=0,
                                              modes=A_MODES, head_gain=_a_head_gain(prm, li, n_heads), tn=TN,
                                              name="a_qkv"), (cos, sin))
            x = linear_residual(a_attention_prompt(z, batch, seq), prm['w_a_out'], li, x, tn=TN, tm=TM_OUT_PROJ,
                                name="a_out_prompt")
            z3 = z.reshape(batch, seq, -1)
            for g, (win, dil) in enumerate(A_PATTERNS):
                kv = z3[:, seq - min(win, seq):, (3 * g + 1) * d:(3 * g + 3) * d]
                new.setdefault(f'a_w{g + 1}', []).append(kv.reshape(batch, -1, 2, n_heads, HEAD_DIM))
        elif kind == 1:
            bg, u = yield ('b_in', x, dict(gain=prm['attn_norm'], gi=i, w=prm['w_b_in'], li=li, tn=TN_WIDE_K,
                                           name="b_in"))
            x = b_out_prompt(u, bg, prm['b_conv'], prm['w_b_out'], li, x, seq, tn=TN, name="b_out_prompt")
            new.setdefault('b_conv', []).append(u.reshape(batch, seq, d)[:, seq - (CONV_W - 1):])
        elif kind == 2:
            head_gain, wg, w_cat, pe8, w1_flat, w2, k_gain0 = _c_weights(prm, li, n_heads)
            z = yield ('norm_linear', x, dict(gain=prm['attn_norm'], gi=i, w=prm['w_c_in'], wi=li, col0=0,
                                              modes=C_MODES, head_gain=head_gain, tn=TN, name="c_in"), (cos, sin))
            gates = yield ('norm_linear', x, dict(gain=prm['attn_norm'], gi=i, w=wg, wi=0, col0=0, modes=(SIGMOID,),
                                                  head_gain=head_gain[:, :TN], tn=TN, name="c_gate"), (cos, sin))
            n_pages = seq // LANES
            pool = z.reshape(batch * n_pages, LANES, z.shape[1])
            kv_width = 2 * C_KV_HEADS * HEAD_DIM
            fs = cmp_first_second(pool, jnp.arange(batch * n_pages, dtype=I32), d // kv_width, w_cat, batch, n_pages)
            kcv = cmp_mlp(fs, cmp_pe_term(pe8, w1_flat), w2, k_gain0)
            nsel = -(-seq // C_SEL_BLOCK)
            overlap = _overlap_matrix(kcv.shape[2], nsel, LANES)
            key_blk = (jnp.arange(seq) // C_SEL_BLOCK).reshape(seq // C_KEY_TILE, 1, C_KEY_TILE)
            expand = (key_blk == jnp.arange(LANES)[None, :, None]).astype(BF16)
            o = nsa_prompt(z, gates, cos, sin, kcv, overlap.T, expand, batch, seq)
            x = linear_residual(o, prm['w_c_out'], li, x, tn=TN, tm=TM_OUT_PROJ, name="c_out_prompt")
            z3 = z.reshape(batch, seq, -1)
            for br, nm in enumerate(('c_cmp', 'c_slc', 'c_win')):
                rows = min(C_WINDOW, seq) if nm == 'c_win' else seq
                kv = z3[:, seq - rows:, d + br * kv_width:d + (br + 1) * kv_width]
                new.setdefault(nm, []).append(kv.reshape(batch, rows, 2, C_KV_HEADS, HEAD_DIM))
        else:
            x, tail = d_mixer_prompt(x, prm['attn_norm'], i, prm['w_d_group'], prm['d_scale'], li, seq,
                                     name="d_mixer_prompt")
            new.setdefault('d_pool', []).append(tail[:, 1:])
        x = yield from _channel_mixer(x, p, prm, i, "prompt")
    return x.reshape(batch, seq, d), {nm: jnp.stack(v, axis=0) for nm, v in new.items()}


def _run_sample(x3, p, prm, past, page_table):
    n_batch, n_new, d = x3.shape
    assert n_new == 1
    n_heads = d // HEAD_DIM
    rows = SUBLANES_BF16
    x = _pad_rows(x3.reshape(n_batch, d), rows)
    past_len = page_table.shape[1] * past['c_cmp'].shape[2]
    cos, sin = _rope_tables(jnp.full((rows,), past_len))
    new = {}
    depth = prm['attn_norm'].shape[0]
    for i in range(depth):
        kind, li = i % 4, i // 4
        if kind == 0:
            z = yield ('norm_linear', x, None, (cos, sin))
            caches =[past[f'a_w{g + 1}'] for g in range(len(A_PATTERNS))]
            o = a_attention_sample(z, caches, li, n_batch)
            x = linear_residual(_pad_rows(o, rows), prm['w_a_out'], li, x, tn=TN, name="a_out_sample")
            for g, cache in enumerate(caches):
                kv = z[:n_batch, (3 * g + 1) * d:(3 * g + 3) * d].reshape(n_batch, 1, 2, n_heads, HEAD_DIM)
                new.setdefault(f'a_w{g + 1}', []).append(shift_append(cache, li, kv, f"a_w{g + 1}_shift"))
        elif kind == 1:
            bg, u = yield ('b_in', x, None)
            hist = past['b_conv'][li]
            x = b_out_sample(u, _pad_rows(hist[:, 1], rows), _pad_rows(hist[:, 0], rows), bg, prm['b_conv'],
                             prm['w_b_out'], li, x, tn=TN, name="b_out_sample")
            new.setdefault('b_conv', []).append(jnp.concatenate([hist[:, 1:], u[:n_batch, None]], axis=1))
        elif kind == 2:
            head_gain, wg, w_cat, pe8, w1_flat, w2, k_gain0 = _c_weights(prm, li, n_heads)
            z = yield ('norm_linear', x, None, (cos, sin))
            gates = yield ('norm_linear', x, None, (cos, sin))
            kv_width = 2 * C_KV_HEADS * HEAD_DIM
            kcv = cmp_mlp(cmp_first_second_paged(past['c_cmp'], li, page_table, w_cat),
                          cmp_pe_term(pe8, w1_flat), w2, k_gain0)
            nsel = -(-(past_len + 1) // C_SEL_BLOCK)
            width = -(-nsel // LANES) * LANES
            overlap = _overlap_matrix(kcv.shape[2], nsel, width)
            zb = z[:n_batch]
            q8 = jnp.pad(zb[:, :d].reshape(n_batch, C_KV_HEADS, C_GROUP, HEAD_DIM),
                         ((0, 0), (0, 0), (0, 8 - C_GROUP), (0, 0)))
            ocmp, idx = nsa_sample_cmp(q8, kcv, overlap, past_len, nsel)
            top_idx = idx[:, :C_KV_HEADS, :C_SEL_TOPN]
            kvh_w = C_KV_HEADS * HEAD_DIM
            new_rows = [zb[:, d + kv_width + j * kvh_w:d + kv_width + (j + 1) * kvh_w]
                        .reshape(n_batch, C_KV_HEADS, 1, HEAD_DIM) for j in range(4)]
            new_kv = jnp.pad(jnp.concatenate(new_rows, axis=2), ((0, 0), (0, 0), (0, 4), (0, 0)))
            gate8 = gates[:n_batch].reshape(n_batch, C_KV_HEADS, LANES)[:, :, :C_GROUP * 3]
            gate8 = jnp.pad(gate8.reshape(n_batch, C_KV_HEADS, C_GROUP, 3),
                            ((0, 0), (0, 0), (0, 8 - C_GROUP), (0, HEAD_DIM - 3)))
            o = nsa_sample(top_idx, page_table, q8, cos[:8], sin[:8], past['c_slc'], past['c_win'], li, new_kv,
                           ocmp, gate8, past_len, past_len // C_SEL_BLOCK)
            o = o[:, :, :C_GROUP].reshape(n_batch, d)
            x = linear_residual(_pad_rows(o, rows), prm['w_c_out'], li, x, tn=TN, name="c_out_sample")
            for br, nm in enumerate(('c_cmp', 'c_slc', 'c_win')):
                kv = zb[:, d + br * kv_width:d + (br + 1) * kv_width].reshape(n_batch, 1, 2, C_KV_HEADS, HEAD_DIM)
                if nm == 'c_win':
                    kv = shift_append(past['c_win'], li, kv, "c_win_shift")
                new.setdefault(nm, []).append(kv)
        else:
            hist = past['d_pool'][li]
            hist_t = jnp.pad(jnp.swapaxes(hist, 0, 1), ((0, 0), (0, rows - n_batch), (0, 0)))
            x, h = d_mixer_sample(x, hist_t, prm['attn_norm'], i, prm['w_d_group'], prm['d_scale'], li,
                                  name="d_mixer_sample")
            new.setdefault('d_pool', []).append(jnp.concatenate([hist[:, 1:], h[:n_batch, None]], axis=1))
        x = yield from _channel_mixer(x, p, prm, i, "sample")
    return x[:n_batch].reshape(n_batch, 1, d), {nm: jnp.stack(v, axis=0) for nm, v in new.items()}


def kernel(x_prompt, x_sample, cache_a_w1, cache_a_w2, cache_a_w3, state_b_conv, cache_c_cmp, cache_c_slc, cache_c_win, state_d_pool, page_table, p_prompt, p_sample, attn_norm, ffn_norm, ple_norm, w_a_qkv, a_q_norm, a_k_norm, w_a_out, w_b_in, b_conv, w_b_out, w_c_in, c_q_norm, c_k_norm, c_cmp_pe, c_cmp_w1, c_cmp_w2, w_c_out, w_d_group, d_scale, w_ffn_in, w_ffn_out, w_ple_proj, w_ple_gate):
    prm = dict(attn_norm=attn_norm, ffn_norm=ffn_norm, ple_norm=ple_norm, w_a_qkv=w_a_qkv, a_q_norm=a_q_norm,
               a_k_norm=a_k_norm, w_a_out=w_a_out, w_b_in=w_b_in, b_conv=b_conv, w_b_out=w_b_out, w_c_in=w_c_in,
               c_q_norm=c_q_norm, c_k_norm=c_k_norm, c_cmp_pe=c_cmp_pe, c_cmp_w1=c_cmp_w1, c_cmp_w2=c_cmp_w2,
               w_c_out=w_c_out, w_d_group=w_d_group, d_scale=d_scale, w_ffn_in=w_ffn_in, w_ffn_out=w_ffn_out,
               w_ple_proj=w_ple_proj, w_ple_gate=w_ple_gate)
    past = dict(a_w1=cache_a_w1, a_w2=cache_a_w2, a_w3=cache_a_w3, b_conv=state_b_conv, c_cmp=cache_c_cmp,
                c_slc=cache_c_slc, c_win=cache_c_win, d_pool=state_d_pool)
    depth = attn_norm.shape[0]
    batch, seq, _ = x_prompt.shape
    n_dec = x_sample.shape[0]
    p_p = p_prompt.reshape(depth, batch * seq, -1)
    p_s = jnp.pad(p_sample.reshape(depth, n_dec, -1), ((0, 0), (0, SUBLANES_BF16 - n_dec), (0, 0)))
    (y_prompt, sp), (y_sample, ss) = _run_both(_run_prompt(x_prompt, p_p, prm),
                                               _run_sample(x_sample, p_s, prm, past, page_table))
    return (y_prompt, y_sample,
            sp['a_w1'], ss['a_w1'], sp['a_w2'], ss['a_w2'], sp['a_w3'], ss['a_w3'],
            sp['b_conv'], ss['b_conv'],
            sp['c_cmp'], ss['c_cmp'], sp['c_slc'], ss['c_slc'], sp['c_win'], ss['c_win'],
            sp['d_pool'], ss['d_pool'])
```

```python
import functools

import jax
import jax.numpy as jnp
from jax import lax
from jax.experimental import pallas as pl
from jax.experimental.pallas import tpu as pltpu

F32 = jnp.float32
BF16 = jnp.bfloat16
I32 = jnp.int32

HEAD_DIM = 128
LANES = 128
SUBLANES_BF16 = 16
RMS_EPS = 1e-6
ROPE_THETA = 10000.0
NEG = -1e30
V7X_VMEM_BYTES = 64 * 1024 * 1024
VMEM_LIMIT = V7X_VMEM_BYTES - 8 * 1024 * 1024

A_PATTERNS = ((128, 1), (512, 4), (2048, 16))
A_BLOCK = 128
A_UNROLL = 8
CONV_W = 3
C_KV_HEADS = 4
C_GROUP = 4
C_CMP_LEN = 32
C_CMP_STRIDE = 16
C_SEL_BLOCK = 64
C_SEL_TOPN = 16
C_WINDOW = 512
C_FORCE_SCORE = 1e4
C_QBLOCK = 128
C_KEY_TILE = 512
LOG2_E = 1.4426950408889634
D_POOLS = (2, 4, 8, 16)
D_HIST = max(D_POOLS) - 1
PAGES_PER_GROUP = 16
SHIFT_ROWS = 256

RAW, NORM, NORM_ROPE, SIGMOID = range(4)


def _cparams(n_axes):
    return pltpu.CompilerParams(dimension_semantics=("arbitrary",) * n_axes,
                                vmem_limit_bytes=VMEM_LIMIT)


def _row_tile(m):
    return 1024 if m % 1024 == 0 else m


def _dot(a, b):
    return jnp.dot(a, b, preferred_element_type=F32)


def _dot_nt(a, b):
    return lax.dot_general(a, b, (((1,), (1,)), ((), ())), preferred_element_type=F32)


def _with_ones(v):
    return jnp.concatenate([v.astype(BF16), jnp.ones(v.shape, BF16)], axis=1)


def _rms(x, g):
    return x * lax.rsqrt(jnp.mean(x * x, axis=-1, keepdims=True) + RMS_EPS) * g


def _rope(y, cos, sin):
    return y * cos + pltpu.roll(y, HEAD_DIM // 2, 1) * sin


def _tile_pred(j, tiles):
    pred = None
    lo = prev = tiles[0]
    runs = []
    for t in tiles[1:]:
        if t != prev + 1:
            runs.append((lo, prev))
            lo = t
        prev = t
    runs.append((lo, prev))
    for lo, hi in runs:
        p = (j == lo) if lo == hi else ((j >= lo) & (j <= hi))
        pred = p if pred is None else (pred | p)
    return pred


def _norm_linear_kernel(x_ref, xs_ref, g_ref, w_ref, hg_ref, cos_ref, sin_ref, coss_ref, sins_ref, o_ref, os_ref,
                        h_scr, *, modes):
    j = pl.program_id(1)
    tm = x_ref.shape[0]

    @pl.when(j == 0)
    def _():
        h_scr[0:tm, :] = _rms(x_ref[...], g_ref[...]).astype(BF16)
        h_scr[tm:, :] = _rms(xs_ref[...], g_ref[...]).astype(BF16)

    acc = _dot(h_scr[...], w_ref[...].astype(BF16))
    tn = acc.shape[1]

    def emit(mode):
        for rows, out_ref, c_ref, s_ref in ((slice(0, tm), o_ref, cos_ref, sin_ref),
                                            (slice(tm, None), os_ref, coss_ref, sins_ref)):
            if mode == RAW:
                out_ref[...] = acc[rows].astype(out_ref.dtype)
            elif mode == SIGMOID:
                out_ref[...] = jax.nn.sigmoid(acc[rows]).astype(out_ref.dtype)
            else:
                for c in range(tn // HEAD_DIM):
                    sl = slice(c * HEAD_DIM, (c + 1) * HEAD_DIM)
                    y = _rms(acc[rows, sl], hg_ref[:, sl])
                    if mode == NORM_ROPE:
                        y = _rope(y, c_ref[...], s_ref[...])
                    out_ref[:, sl] = y.astype(out_ref.dtype)

    kinds = sorted(set(modes))
    if len(kinds) == 1:
        emit(kinds[0])
    else:
        for m in kinds:
            tiles = [t for t, mm in enumerate(modes) if mm == m]
            pl.when(_tile_pred(j, tiles))(functools.partial(emit, m))


def _dec_spec(rows, cols):
    return pl.BlockSpec((None, rows, cols), lambda i, j: (i, 0, j))


def _dec_shape(n_row_tiles, rows, cols, dtype):
    return jax.ShapeDtypeStruct((n_row_tiles, rows, cols), dtype)


def norm_linear(x, xs, gain, gi, w, wi, *, col0, modes, head_gain, rope, rope_s, tn, name):
    m, k = x.shape
    ms = xs.shape[0]
    tm = _row_tile(m)
    nt = len(modes)
    (cos, sin), (cos_s, sin_s) = rope, rope_s
    assert col0 % tn == 0 and cos.shape[0] % tm == 0 and cos_s.shape[0] == ms and ms % SUBLANES_BF16 == 0
    coff = col0 // tn
    n_pos_tiles = cos.shape[0] // tm
    full_s = pl.BlockSpec((ms, HEAD_DIM), lambda i, j: (0, 0))
    out, out_s = pl.pallas_call(
        functools.partial(_norm_linear_kernel, modes=tuple(modes)),
        out_shape=[jax.ShapeDtypeStruct((m, nt * tn), F32), _dec_shape(m // tm, ms, nt * tn, F32)],
        grid=(m // tm, nt),
        in_specs=[
            pl.BlockSpec((tm, k), lambda i, j: (i, 0)),
            pl.BlockSpec((ms, k), lambda i, j: (0, 0)),
            pl.BlockSpec((None, 1, k), lambda i, j: (gi, 0, 0)),
            pl.BlockSpec((None, k, tn), lambda i, j: (wi, 0, j + coff)),
            pl.BlockSpec((1, tn), lambda i, j: (0, j)),
            pl.BlockSpec((tm, HEAD_DIM), lambda i, j: (i % n_pos_tiles, 0)),
            pl.BlockSpec((tm, HEAD_DIM), lambda i, j: (i % n_pos_tiles, 0)),
            full_s, full_s,
        ],
        out_specs=[pl.BlockSpec((tm, tn), lambda i, j: (i, j)), _dec_spec(ms, tn)],
        scratch_shapes=[pltpu.VMEM((tm + ms, k), BF16)],
        compiler_params=_cparams(2),
        name=name,
    )(x, xs, gain.reshape(gain.shape[0], 1, k), w, head_gain, cos, sin, cos_s, sin_s)
    return out, out_s[0]


def _linear_res_kernel(a_ref, w_ref, r_ref, o_ref):
    o_ref[...] = r_ref[...] + _dot(a_ref[...].astype(BF16), w_ref[...].astype(BF16))


def linear_residual(a, w, li, res, *, tn, name, tm=None):
    m, k = a.shape
    n = w.shape[-1]
    tm = _row_tile(m) if tm is None or m % tm else tm
    return pl.pallas_call(
        _linear_res_kernel,
        out_shape=jax.ShapeDtypeStruct((m, n), F32),
        grid=(m // tm, n // tn),
        in_specs=[
            pl.BlockSpec((tm, k), lambda i, j: (i, 0)),
            pl.BlockSpec((None, k, tn), lambda i, j: (li, 0, j)),
            pl.BlockSpec((tm, tn), lambda i, j: (i, j)),
        ],
        out_specs=pl.BlockSpec((tm, tn), lambda i, j: (i, j)),
        compiler_params=_cparams(2),
        name=name,
    )(a, w, res)


def _swiglu_in_kernel(x_ref, xs_ref, g_ref, wg_ref, wu_ref, o_ref, os_ref, h_scr):
    tm = x_ref.shape[0]

    @pl.when(pl.program_id(1) == 0)
    def _():
        h_scr[0:tm, :] = _rms(x_ref[...], g_ref[...]).astype(BF16)
        h_scr[tm:, :] = _rms(xs_ref[...], g_ref[...]).astype(BF16)

    h = h_scr[...]
    a = _dot(h, wg_ref[...].astype(BF16))
    b = _dot(h, wu_ref[...].astype(BF16))
    act = (a * jax.nn.sigmoid(a) * b).astype(o_ref.dtype)
    o_ref[...] = act[0:tm]
    os_ref[...] = act[tm:]


def swiglu_in(x, xs, gain, w, li, *, tn, name):
    m, k = x.shape
    ms = xs.shape[0]
    f = w.shape[-1] // 2
    tm = _row_tile(m)
    nt = f // tn
    act, act_s = pl.pallas_call(
        _swiglu_in_kernel,
        out_shape=[jax.ShapeDtypeStruct((m, f), BF16), _dec_shape(m // tm, ms, f, BF16)],
        grid=(m // tm, nt),
        in_specs=[
            pl.BlockSpec((tm, k), lambda i, j: (i, 0)),
            pl.BlockSpec((ms, k), lambda i, j: (0, 0)),
            pl.BlockSpec((None, 1, k), lambda i, j: (li, 0, 0)),
            pl.BlockSpec((None, k, tn), lambda i, j: (li, 0, j)),
            pl.BlockSpec((None, k, tn), lambda i, j: (li, 0, j + nt)),
        ],
        out_specs=[pl.BlockSpec((tm, tn), lambda i, j: (i, j)), _dec_spec(ms, tn)],
        scratch_shapes=[pltpu.VMEM((tm + ms, k), BF16)],
        compiler_params=_cparams(2),
        name=name,
    )(x, xs, gain.reshape(gain.shape[0], 1, k), w, w)
    return act, act_s[0]


def _ple_kernel(x_ref, xs_ref, g_ref, p_ref, ps_ref, wp_ref, wg_ref, o_ref, os_ref, h_scr):
    j = pl.program_id(1)
    tm, tn = o_ref.shape

    @pl.when(j == 0)
    def _():
        h_scr[0:tm, :] = _rms(x_ref[...], g_ref[...]).astype(BF16)
        h_scr[tm:, :] = _rms(xs_ref[...], g_ref[...]).astype(BF16)

    gate = jax.nn.sigmoid(_dot(h_scr[...], wg_ref[...].astype(BF16)))
    p_rows = jnp.concatenate([p_ref[...].astype(BF16), ps_ref[...].astype(BF16)], axis=0)
    upd = _dot(p_rows, wp_ref[...].astype(BF16)) * gate
    cols = pl.ds(pl.multiple_of(j * tn, tn), tn)
    o_ref[...] = x_ref[:, cols] + upd[0:tm]
    os_ref[...] = xs_ref[:, cols] + upd[tm:]


def ple(x, xs, gain, p, ps, wp, wg, li, *, tn, name):
    m, k = x.shape
    ms = xs.shape[0]
    pd = p.shape[-1]
    tm = _row_tile(m)
    assert p.shape[1] == m and ps.shape[1] == ms
    out, out_s = pl.pallas_call(
        _ple_kernel,
        out_shape=[jax.ShapeDtypeStruct((m, k), F32), _dec_shape(m // tm, ms, k, F32)],
        grid=(m // tm, k // tn),
        in_specs=[
            pl.BlockSpec((tm, k), lambda i, j: (i, 0)),
            pl.BlockSpec((ms, k), lambda i, j: (0, 0)),
            pl.BlockSpec((None, 1, k), lambda i, j: (li, 0, 0)),
            pl.BlockSpec((None, tm, pd), lambda i, j: (li, i, 0)),
            pl.BlockSpec((None, ms, pd), lambda i, j: (li, 0, 0)),
            pl.BlockSpec((None, pd, tn), lambda i, j: (li, 0, j)),
            pl.BlockSpec((None, k, tn), lambda i, j: (li, 0, j)),
        ],
        out_specs=[pl.BlockSpec((tm, tn), lambda i, j: (i, j)), _dec_spec(ms, tn)],
        scratch_shapes=[pltpu.VMEM((tm + ms, k), BF16)],
        compiler_params=_cparams(2),
        name=name,
    )(x, xs, gain.reshape(gain.shape[0], 1, k), p, ps, wp, wg)
    return out, out_s[0]


def _a_prompt_kernel(*refs, seq):
    qkv_refs, o_ref = refs[:9], refs[9]
    acc_scr, m_scr, l_scr = refs[10:]
    scale = HEAD_DIM ** -0.5
    row = lax.broadcasted_iota(I32, (A_BLOCK, A_BLOCK), 0)
    col = lax.broadcasted_iota(I32, (A_BLOCK, A_BLOCK), 1)
    cur_ok = col <= row
    prev_ok = col >= row
    wide = (A_BLOCK, HEAD_DIM)
    first = len(A_PATTERNS) - 1
    for g, (win, dil) in reversed(list(enumerate(A_PATTERNS))):
        q_ref, k_ref, v_ref = qkv_refs[3 * g:3 * g + 3]
        nb = seq // dil // A_BLOCK

        def rows_of(r, n, dil=dil):
            start = r + dil * A_BLOCK * n
            if dil == 1:
                return pl.ds(pl.multiple_of(start, A_BLOCK), A_BLOCK)
            return pl.ds(start, A_BLOCK, stride=dil)

        def attend(idx, nb=nb, q_ref=q_ref, k_ref=k_ref, v_ref=v_ref, rows_of=rows_of):
            r, n = idx // nb, idx % nb
            rows = rows_of(r, n)
            q = (q_ref[0, rows, :] * scale).astype(BF16)
            s_c = jnp.where(cur_ok, _dot_nt(q, k_ref[0, rows, :].astype(BF16)), NEG)
            if nb > 1:
                prows = rows_of(r, jnp.maximum(n - 1, 0))
                s_p = jnp.where(prev_ok & (n > 0), _dot_nt(q, k_ref[0, prows, :].astype(BF16)), NEG)
                m = jnp.max(jnp.maximum(s_c, s_p), axis=1, keepdims=True)
            else:
                m = jnp.max(s_c, axis=1, keepdims=True)
            ext = _dot(jnp.exp(s_c - m).astype(BF16), _with_ones(v_ref[0, rows, :]))
            if nb > 1:
                ext = ext + _dot(jnp.exp(s_p - m).astype(BF16), _with_ones(v_ref[0, prows, :]))
            return rows, m, ext[:, :HEAD_DIM], ext[:, HEAD_DIM:]

        def block(idx, carry, g=g, attend=attend):
            rows, m, acc, den = attend(idx)
            if g == first:
                acc_scr[rows, :] = acc
                m_scr[rows, :] = jnp.broadcast_to(m, wide)
                l_scr[rows, :] = den
            else:
                m_old = m_scr[rows, :]
                m_new = jnp.maximum(m_old, m)
                a_old, a_blk = jnp.exp(m_old - m_new), jnp.exp(m - m_new)
                acc_scr[rows, :] = a_old * acc_scr[rows, :] + a_blk * acc
                l_scr[rows, :] = a_old * l_scr[rows, :] + a_blk * den
                m_scr[rows, :] = m_new
            return carry

        lax.fori_loop(0, dil * nb, block, 0, unroll=A_UNROLL)
    o_ref[0] = (acc_scr[...] / l_scr[...]).astype(o_ref.dtype)


def a_attention_prompt(z, batch, seq):
    nc = z.shape[1]
    d = nc // (3 * len(A_PATTERNS))
    n_heads = d // HEAD_DIM
    assert all(win // dil == A_BLOCK and seq % (dil * A_BLOCK) == 0 for win, dil in A_PATTERNS)
    zv = z.reshape(batch, seq, nc)
    in_specs = [pl.BlockSpec((1, seq, HEAD_DIM), functools.partial(lambda b, h, c: (b, 0, c * n_heads + h), c=c))
                for c in range(3 * len(A_PATTERNS))]
    out = pl.pallas_call(
        functools.partial(_a_prompt_kernel, seq=seq),
        out_shape=jax.ShapeDtypeStruct((batch, seq, d), BF16),
        grid=(batch, n_heads),
        in_specs=in_specs,
        out_specs=pl.BlockSpec((1, seq, HEAD_DIM), lambda b, h: (b, 0, h)),
        scratch_shapes=[pltpu.VMEM((seq, HEAD_DIM), F32)] * 3,
        compiler_params=_cparams(2),
        name="a_attn_prompt",
    )(*([zv] * (3 * len(A_PATTERNS))))
    return out.reshape(batch * seq, d)


def _a_sample_kernel(z_ref, *refs, n_heads, d):
    cache_refs, o_ref = refs[:-1], refs[-1]
    scale = HEAD_DIM ** -0.5
    for h in range(n_heads):
        ms, dens, accs = [], [], []
        for g in range(len(A_PATTERNS)):
            kv_ref = cache_refs[g]
            c0 = g * 3 * d + h * HEAD_DIM
            q = z_ref[0, :, c0:c0 + HEAD_DIM] * scale
            k_new = z_ref[0, :, c0 + d:c0 + d + HEAD_DIM]
            v_new = z_ref[0, :, c0 + 2 * d:c0 + 2 * d + HEAD_DIM]
            q8 = jnp.broadcast_to(q, (8, HEAD_DIM))
            s = _dot_nt(q8.astype(BF16), kv_ref[:, 0, h, :].astype(BF16))
            s_new = jnp.sum(q8 * k_new, axis=1, keepdims=True)
            m = jnp.maximum(jnp.max(s, axis=1, keepdims=True), s_new)
            p = jnp.exp(s - m)
            p_new = jnp.exp(s_new - m)
            dens.append(jnp.sum(p, axis=1, keepdims=True) + p_new)
            accs.append(_dot(p.astype(BF16), kv_ref[:, 1, h, :].astype(BF16)) + p_new * v_new)
            ms.append(m)
        mt = jnp.maximum(jnp.maximum(ms[0], ms[1]), ms[2])
        es = [jnp.exp(mm - mt) for mm in ms]
        tot = es[0] * dens[0] + es[1] * dens[1] + es[2] * dens[2]
        num = es[0] * accs[0] + es[1] * accs[1] + es[2] * accs[2]
        o_ref[0, :, h * HEAD_DIM:(h + 1) * HEAD_DIM] = num / tot


def a_attention_sample(z, caches, li, n_batch):
    d = z.shape[1] // 9
    n_heads = d // HEAD_DIM
    in_specs = [pl.BlockSpec((1, 1, z.shape[1]), lambda b: (b, 0, 0))]
    args = [z[:, None, :]]
    for (win, dil), cache in zip(A_PATTERNS, caches):
        lb = cache.shape[2]
        assert lb == win and (win // dil) == A_BLOCK
        in_specs.append(pl.BlockSpec((None, None, A_BLOCK, None, 2, n_heads, HEAD_DIM),
                                     lambda b: (li, b, 0, 0, 0, 0, 0)))
        args.append(cache.reshape(cache.shape[0], n_batch, lb // dil, dil, 2, n_heads, HEAD_DIM))
    out = pl.pallas_call(
        functools.partial(_a_sample_kernel, n_heads=d // HEAD_DIM, d=d),
        out_shape=jax.ShapeDtypeStruct((n_batch, 8, d), F32),
        grid=(n_batch,),
        in_specs=in_specs,
        out_specs=pl.BlockSpec((1, 8, d), lambda b: (b, 0, 0)),
        compiler_params=_cparams(1),
        name="a_attn_sample",
    )(*args)
    return out[:, 0]


def _b_in_kernel(x_ref, xs_ref, g_ref, wb_ref, wc_ref, wx_ref, bg_ref, u_ref, bgs_ref, us_ref, h_scr):
    tm = x_ref.shape[0]

    @pl.when(pl.program_id(1) == 0)
    def _():
        h_scr[0:tm, :] = _rms(x_ref[...], g_ref[...]).astype(BF16)
        h_scr[tm:, :] = _rms(xs_ref[...], g_ref[...]).astype(BF16)

    h = h_scr[...]
    bg = _dot(h, wb_ref[...].astype(BF16))
    u = _dot(h, wc_ref[...].astype(BF16)) * _dot(h, wx_ref[...].astype(BF16))
    bg_ref[...] = bg[0:tm]
    u_ref[...] = u[0:tm]
    bgs_ref[...] = bg[tm:]
    us_ref[...] = u[tm:]


def b_in(x, xs, gain, gi, w, li, *, tn, name):
    m, k = x.shape
    ms = xs.shape[0]
    d = w.shape[-1] // 3
    tm = _row_tile(m)
    nt = d // tn
    out_spec = pl.BlockSpec((tm, tn), lambda i, j: (i, j))
    bg, u, bg_s, u_s = pl.pallas_call(
        _b_in_kernel,
        out_shape=[jax.ShapeDtypeStruct((m, d), F32)] * 2 + [_dec_shape(m // tm, ms, d, F32)] * 2,
        grid=(m // tm, nt),
        in_specs=[
            pl.BlockSpec((tm, k), lambda i, j: (i, 0)),
            pl.BlockSpec((ms, k), lambda i, j: (0, 0)),
            pl.BlockSpec((None, 1, k), lambda i, j: (gi, 0, 0)),
            pl.BlockSpec((None, k, tn), lambda i, j: (li, 0, j)),
            pl.BlockSpec((None, k, tn), lambda i, j: (li, 0, j + nt)),
            pl.BlockSpec((None, k, tn), lambda i, j: (li, 0, j + 2 * nt)),
        ],
        out_specs=[out_spec, out_spec, _dec_spec(ms, tn), _dec_spec(ms, tn)],
        scratch_shapes=[pltpu.VMEM((tm + ms, k), BF16)],
        compiler_params=_cparams(2),
        name=name,
    )(x, xs, gain.reshape(gain.shape[0], 1, k), w, w, w)
    return bg, u, bg_s[0], u_s[0]


def _b_out_prompt_kernel(u_ref, up_ref, bg_ref, cw_ref, w_ref, r_ref, o_ref, ext_scr, a_scr, *, tiles_per_seq):
    i = pl.program_id(0)
    tm = u_ref.shape[0]

    @pl.when(pl.program_id(1) == 0)
    def _():
        ext_scr[0:8, :] = jnp.where(i % tiles_per_seq == 0, 0.0, up_ref[...])
        ext_scr[8:, :] = u_ref[...]
        y = (cw_ref[0:1, :] * ext_scr[pl.ds(6, tm), :] + cw_ref[1:2, :] * ext_scr[pl.ds(7, tm), :]
             + cw_ref[2:3, :] * ext_scr[pl.ds(8, tm), :])
        a_scr[...] = (bg_ref[...] * y).astype(BF16)

    o_ref[...] = r_ref[...] + _dot(a_scr[...], w_ref[...].astype(BF16))


def b_out_prompt(u, bg, conv_w, w, li, res, seq, *, tn, name):
    m, d = u.shape
    tm = 512
    assert seq % tm == 0
    return pl.pallas_call(
        functools.partial(_b_out_prompt_kernel, tiles_per_seq=seq // tm),
        out_shape=jax.ShapeDtypeStruct((m, d), F32),
        grid=(m // tm, d // tn),
        in_specs=[
            pl.BlockSpec((tm, d), lambda i, j: (i, 0)),
            pl.BlockSpec((8, d), lambda i, j: (jnp.maximum(i * (tm // 8) - 1, 0), 0)),
            pl.BlockSpec((tm, d), lambda i, j: (i, 0)),
            pl.BlockSpec((None, CONV_W, d), lambda i, j: (li, 0, 0)),
            pl.BlockSpec((None, d, tn), lambda i, j: (li, 0, j)),
            pl.BlockSpec((tm, tn), lambda i, j: (i, j)),
        ],
        out_specs=pl.BlockSpec((tm, tn), lambda i, j: (i, j)),
        scratch_shapes=[pltpu.VMEM((tm + 8, d), F32), pltpu.VMEM((tm, d), BF16)],
        compiler_params=_cparams(2),
        name=name,
    )(u, u, bg, conv_w, w, res)


def _b_out_sample_kernel(u_ref, um1_ref, um2_ref, bg_ref, cw_ref, w_ref, r_ref, o_ref):
    y = cw_ref[0:1, :] * um2_ref[...] + cw_ref[1:2, :] * um1_ref[...] + cw_ref[2:3, :] * u_ref[...]
    o_ref[...] = r_ref[...] + _dot((bg_ref[...] * y).astype(BF16), w_ref[...].astype(BF16))


def b_out_sample(u, um1, um2, bg, conv_w, w, li, res, *, tn, name):
    m, d = u.shape
    full = pl.BlockSpec((m, d), lambda j: (0, 0))
    return pl.pallas_call(
        _b_out_sample_kernel,
        out_shape=jax.ShapeDtypeStruct((m, d), F32),
        grid=(d // tn,),
        in_specs=[full, full, full, full,
                  pl.BlockSpec((None, CONV_W, d), lambda j: (li, 0, 0)),
                  pl.BlockSpec((None, d, tn), lambda j: (li, 0, j)),
                  pl.BlockSpec((m, tn), lambda j: (0, j))],
        out_specs=pl.BlockSpec((m, tn), lambda j: (0, j)),
        compiler_params=_cparams(1),
        name=name,
    )(u, um1, um2, bg, conv_w, w, res)


def _d_prompt_kernel(x_ref, xp_ref, g_ref, w_ref, sc_ref, o_ref, ht_ref, ext_scr, *, tiles_per_seq):
    i = pl.program_id(0)
    j = pl.program_id(1)
    tm = x_ref.shape[0]
    halo = D_HIST + 1
    gw = w_ref.shape[0]

    @pl.when(j == 0)
    def _():
        ext_scr[0:halo, :] = jnp.where(i % tiles_per_seq == 0, 0.0, _rms(xp_ref[...], g_ref[...]))
        ext_scr[halo:, :] = _rms(x_ref[...], g_ref[...])
        ht_ref[0] = ext_scr[pl.ds(tm, halo), :]

    pos = (i % tiles_per_seq) * tm + lax.broadcasted_iota(I32, (tm, 1), 0)
    for g, win in enumerate(D_POOLS):
        @pl.when(j == g)
        def _(g=g, win=win):
            cols = slice(g * gw, (g + 1) * gw)
            tot = ext_scr[pl.ds(halo, tm), cols]
            h = tot
            for back in range(1, win):
                tot = tot + ext_scr[pl.ds(halo - back, tm), cols]
            count = jnp.minimum(pos + 1, win).astype(F32)
            pooled = tot / count - h
            o_ref[...] = x_ref[:, cols] + _dot(pooled.astype(BF16), w_ref[...].astype(BF16)) * sc_ref[...]


def d_mixer_prompt(x, gain, li_norm, w_group, scale, li, seq, *, name):
    m, d = x.shape
    n_groups, gw = w_group.shape[1], w_group.shape[2]
    tm = 512
    halo = D_HIST + 1
    assert seq % tm == 0 and n_groups == len(D_POOLS)
    tps = seq // tm
    return pl.pallas_call(
        functools.partial(_d_prompt_kernel, tiles_per_seq=tps),
        out_shape=[jax.ShapeDtypeStruct((m, d), F32), jax.ShapeDtypeStruct((m // seq, halo, d), F32)],
        grid=(m // tm, n_groups),
        in_specs=[
            pl.BlockSpec((tm, d), lambda i, j: (i, 0)),
            pl.BlockSpec((halo, d), lambda i, j: (jnp.maximum(i * (tm // halo) - 1, 0), 0)),
            pl.BlockSpec((None, 1, d), lambda i, j: (li_norm, 0, 0)),
            pl.BlockSpec((None, None, gw, gw), lambda i, j: (li, j, 0, 0)),
            pl.BlockSpec((None, 1, gw), lambda i, j: (li, 0, j)),
        ],
        out_specs=[pl.BlockSpec((tm, gw), lambda i, j: (i, j)),
                   pl.BlockSpec((1, halo, d), lambda i, j: (i // tps, 0, 0))],
        scratch_shapes=[pltpu.VMEM((tm + halo, d), F32)],
        compiler_params=_cparams(2),
        name=name,
    )(x, x, gain.reshape(gain.shape[0], 1, d), w_group, scale.reshape(scale.shape[0], 1, d))


def _d_sample_kernel(x_ref, hist_ref, g_ref, w_ref, sc_ref, o_ref, h_ref):
    h = _rms(x_ref[...], g_ref[...])
    h_ref[...] = h
    gw = w_ref.shape[1]
    for g, win in enumerate(D_POOLS):
        cols = slice(g * gw, (g + 1) * gw)
        tot = h[:, cols]
        for back in range(1, win):
            tot = tot + hist_ref[D_HIST - back, :, cols]
        pooled = tot / float(win) - h[:, cols]
        o_ref[:, cols] = x_ref[:, cols] + _dot(pooled.astype(BF16), w_ref[g].astype(BF16)) * sc_ref[:, cols]


def d_mixer_sample(x, hist_t, gain, li_norm, w_group, scale, li, *, name):
    m, d = x.shape
    n_groups, gw = w_group.shape[1], w_group.shape[2]
    assert hist_t.shape[0] == D_HIST
    return pl.pallas_call(
        _d_sample_kernel,
        out_shape=[jax.ShapeDtypeStruct((m, d), F32)] * 2,
        grid=(1,),
        in_specs=[
            pl.BlockSpec((m, d), lambda i: (0, 0)),
            pl.BlockSpec(hist_t.shape, lambda i: (0, 0, 0)),
            pl.BlockSpec((None, 1, d), lambda i: (li_norm, 0, 0)),
            pl.BlockSpec((None, n_groups, gw, gw), lambda i: (li, 0, 0, 0)),
            pl.BlockSpec((None, 1, d), lambda i: (li, 0, 0)),
        ],
        out_specs=[pl.BlockSpec((m, d), lambda i: (0, 0))] * 2,
        compiler_params=_cparams(1),
        name=name,
    )(x, hist_t, gain.reshape(gain.shape[0], 1, d), w_group, scale.reshape(scale.shape[0], 1, d))


def _cmp_fs_kernel(pt_ref, page_ref, w_ref, o_ref, ring_scr):
    del pt_ref
    slot = pl.program_id(1) % PAGES_PER_GROUP
    rows = page_ref.shape[1]
    for ck in range(2 * C_KV_HEADS):
        ring_scr[ck, pl.ds(pl.multiple_of(slot * rows, rows), rows), :] = page_ref[0, :, ck * HEAD_DIM:(ck + 1) * HEAD_DIM]

    @pl.when(slot == PAGES_PER_GROUP - 1)
    def _():
        n_chunks = PAGES_PER_GROUP * rows // C_CMP_STRIDE
        for ck in range(2 * C_KV_HEADS):
            c = ck // C_KV_HEADS
            acc = jnp.zeros((n_chunks, w_ref.shape[-1]), F32)
            for pp in range(C_CMP_STRIDE // 2):
                lhs = jnp.concatenate([ring_scr[ck, pl.ds(2 * pp + i, n_chunks, stride=C_CMP_STRIDE), :]
                                       for i in range(2)], axis=1)
                acc = acc + _dot(lhs.astype(BF16), w_ref[c, pp])
            o_ref[0, ck] = acc


def cmp_first_second(pool, page_ids, col_block, w_cat, n_batch, n_pages):
    rows = pool.shape[1]
    width = 2 * C_KV_HEADS * HEAD_DIM
    assert n_pages % PAGES_PER_GROUP == 0 and rows % C_CMP_STRIDE == 0
    chunks_per_group = PAGES_PER_GROUP * rows // C_CMP_STRIDE
    n_out = w_cat.shape[-1]
    return pl.pallas_call(
        _cmp_fs_kernel,
        out_shape=jax.ShapeDtypeStruct((n_batch, 2 * C_KV_HEADS, n_pages * rows // C_CMP_STRIDE, n_out), F32),
        grid_spec=pltpu.PrefetchScalarGridSpec(
            num_scalar_prefetch=1,
            grid=(n_batch, n_pages),
            in_specs=[
                pl.BlockSpec((1, rows, width), lambda b, pg, pt: (pt[b * n_pages + pg], 0, col_block)),
                pl.BlockSpec(w_cat.shape, lambda b, pg, pt: (0, 0, 0, 0)),
            ],
            out_specs=pl.BlockSpec((1, 2 * C_KV_HEADS, chunks_per_group, n_out),
                                   lambda b, pg, pt: (b, 0, pg // PAGES_PER_GROUP, 0)),
            scratch_shapes=[pltpu.VMEM((width // HEAD_DIM, PAGES_PER_GROUP * rows, HEAD_DIM), F32)],
        ),
        compiler_params=_cparams(2),
        name="c_cmp_first_second",
    )(page_ids, pool, w_cat)


def _cmp_fs_paged_kernel(pt_ref, *refs):
    del pt_ref
    page_refs, (w_ref, o_ref, acc_scr) = refs[:PAGES_PER_GROUP], refs[PAGES_PER_GROUP:]
    chunks_per_page = page_refs[0].shape[0] // C_CMP_STRIDE
    n_chunks = PAGES_PER_GROUP * chunks_per_page
    n_out = w_ref.shape[-1]

    def chunk_rows(p, c):
        return jnp.concatenate([pr[pl.ds(p, chunks_per_page, stride=C_CMP_STRIDE), c, :, :]
                                .reshape(chunks_per_page * C_KV_HEADS, HEAD_DIM) for pr in page_refs], axis=0)

    for c in range(2):
        acc = jnp.zeros((n_chunks * C_KV_HEADS, n_out), F32)
        for pp in range(C_CMP_STRIDE // 2):
            lhs = jnp.concatenate([chunk_rows(2 * pp, c), chunk_rows(2 * pp + 1, c)], axis=1)
            acc = acc + _dot(lhs.astype(BF16), w_ref[c, pp])
        for j in range(n_out // LANES):
            acc_scr[j] = acc[:, j * LANES:(j + 1) * LANES]
        for k in range(C_KV_HEADS):
            for j in range(n_out // LANES):
                o_ref[0, c * C_KV_HEADS + k, :, j * LANES:(j + 1) * LANES] = (
                    acc_scr[j, pl.ds(k, n_chunks, stride=C_KV_HEADS), :])


def cmp_first_second_paged(pool, li, page_table, w_cat):
    n_batch, n_pages = page_table.shape
    rows = pool.shape[2]
    assert n_pages % PAGES_PER_GROUP == 0 and rows % C_CMP_STRIDE == 0
    chunks_per_group = PAGES_PER_GROUP * rows // C_CMP_STRIDE
    n_out = w_cat.shape[-1]

    def page_spec(s):
        return pl.BlockSpec((None, None, rows, 2, C_KV_HEADS, HEAD_DIM),
                            lambda b, grp, pt: (li, pt[b * n_pages + grp * PAGES_PER_GROUP + s], 0, 0, 0, 0))

    return pl.pallas_call(
        _cmp_fs_paged_kernel,
        out_shape=jax.ShapeDtypeStruct((n_batch, 2 * C_KV_HEADS, n_pages * rows // C_CMP_STRIDE, n_out), F32),
        grid_spec=pltpu.PrefetchScalarGridSpec(
            num_scalar_prefetch=1,
            grid=(n_batch, n_pages // PAGES_PER_GROUP),
            in_specs=[page_spec(s) for s in range(PAGES_PER_GROUP)]
            + [pl.BlockSpec(w_cat.shape, lambda b, grp, pt: (0, 0, 0, 0))],
            out_specs=pl.BlockSpec((1, 2 * C_KV_HEADS, chunks_per_group, n_out), lambda b, grp, pt: (b, 0, grp, 0)),
            scratch_shapes=[pltpu.VMEM((n_out // LANES, chunks_per_group * C_KV_HEADS, LANES), F32)],
        ),
        compiler_params=_cparams(2),
        name="c_cmp_first_second_paged",
    )(page_table.reshape(-1), *([pool] * PAGES_PER_GROUP), w_cat)


def _cmp_pe_kernel(pe_ref, w1_ref, o_ref):
    o_ref[0] = _dot(pe_ref[0].astype(BF16), w1_ref[0].astype(BF16))


def cmp_pe_term(pe8, w1_flat):
    return pl.pallas_call(
        _cmp_pe_kernel,
        out_shape=jax.ShapeDtypeStruct((pe8.shape[0], pe8.shape[1], w1_flat.shape[-1]), F32),
        grid=(pe8.shape[0],),
        in_specs=[pl.BlockSpec((1,) + pe8.shape[1:], lambda c: (c, 0, 0)),
                  pl.BlockSpec((1,) + w1_flat.shape[1:], lambda c: (c, 0, 0))],
        out_specs=pl.BlockSpec((1, pe8.shape[1], w1_flat.shape[-1]), lambda c: (c, 0, 0)),
        compiler_params=_cparams(1),
        name="c_cmp_pe_term",
    )(pe8, w1_flat)


def _cmp_mlp_kernel(fs_ref, pe_ref, w2_ref, kg_ref, o_ref):
    ck = pl.program_id(1)
    nch = fs_ref.shape[2]
    hid_w = w2_ref.shape[1]
    pe_term = pe_ref[0, 0:1, :]
    first = fs_ref[0, 0, :, 0:hid_w]
    second = fs_ref[0, 0, :, hid_w:2 * hid_w]
    nxt = pltpu.roll(second, nch - 1, 0)
    last = lax.broadcasted_iota(I32, (nch, 1), 0) == nch - 1
    pre = first + jnp.where(last, 0.0, nxt) + pe_term
    hid = pre * jax.nn.sigmoid(pre)
    out = _dot(hid.astype(BF16), w2_ref[0].astype(BF16))

    @pl.when(ck < C_KV_HEADS)
    def _():
        o_ref[0, 0] = _rms(out, kg_ref[...])

    @pl.when(ck >= C_KV_HEADS)
    def _():
        o_ref[0, 0] = out


def cmp_mlp(fs, pe_term, w2, k_gain0):
    n_batch, n_ck, nch, _ = fs.shape
    return pl.pallas_call(
        _cmp_mlp_kernel,
        out_shape=jax.ShapeDtypeStruct((n_batch, n_ck, nch, HEAD_DIM), F32),
        grid=(n_batch, n_ck),
        in_specs=[
            pl.BlockSpec((1, 1, nch, fs.shape[-1]), lambda b, ck: (b, ck, 0, 0)),
            pl.BlockSpec((1,) + pe_term.shape[1:], lambda b, ck: (ck // C_KV_HEADS, 0, 0)),
            pl.BlockSpec((1,) + w2.shape[1:], lambda b, ck: (ck // C_KV_HEADS, 0, 0)),
            pl.BlockSpec((1, HEAD_DIM), lambda b, ck: (0, 0)),
        ],
        out_specs=pl.BlockSpec((1, 1, nch, HEAD_DIM), lambda b, ck: (b, ck, 0, 0)),
        compiler_params=_cparams(2),
        name="c_cmp_mlp",
    )(fs, pe_term, w2, k_gain0)


def _masked_softmax_rows(s, ok):
    s = jnp.where(ok, s, NEG)
    m = jnp.max(s, axis=1, keepdims=True)
    e = jnp.where(ok, jnp.exp(s - m), 0.0)
    return e, jnp.maximum(jnp.sum(e, axis=1, keepdims=True), 1e-30)


def _nsa_prompt_kernel(q_ref, gate_ref, cos_ref, sin_ref, kc_ref, vc_ref, ks_ref, vs_ref, kw_ref, vw_ref,
                       ov_ref, ex_ref, o_ref, s_scr, mx_scr, ext_scr, *, nsel):
    qb = q_ref.shape[0]
    nch = kc_ref.shape[2]
    tk = s_scr.shape[2]
    t0 = pl.program_id(2) * qb

    def stacked_bias(ok):
        return jnp.concatenate([jnp.where(ok, 0.0, NEG)] * C_GROUP, axis=0)

    def t_rows(n):
        return t0 + lax.broadcasted_iota(I32, (qb, n), 0)

    qscale = HEAD_DIM ** -0.5 * LOG2_E
    qs = [q_ref[:, g * HEAD_DIM:(g + 1) * HEAD_DIM] for g in range(C_GROUP)]
    qn4 = jnp.concatenate([(q * qscale).astype(BF16) for q in qs], axis=0)
    qr4 = jnp.concatenate([(_rope(q, cos_ref[...], sin_ref[...]) * qscale).astype(BF16) for q in qs], axis=0)

    wk = C_WINDOW + qb
    start = pl.multiple_of(jnp.maximum(t0 - C_WINDOW, 0), qb)
    dist = t_rows(wk) - (start + lax.broadcasted_iota(I32, (qb, wk), 1))
    s = _dot_nt(qr4, kw_ref[0, pl.ds(start, wk), :].astype(BF16)) + stacked_bias((dist >= 0) & (dist < C_WINDOW))
    p = jnp.exp2(s - jnp.max(s, axis=1, keepdims=True)).astype(BF16)
    ext = _dot(p, _with_ones(vw_ref[0, pl.ds(start, wk), :]))
    o_win4 = ext[:, :HEAD_DIM] / ext[:, HEAD_DIM:]

    cmp_bias = stacked_bias(lax.broadcasted_iota(I32, (qb, nch), 1) * C_CMP_STRIDE + (C_CMP_LEN - 1) <= t_rows(nch))
    s = _dot_nt(qn4, kc_ref[0, 0].astype(BF16)) + cmp_bias
    e = jnp.where(cmp_bias < 0.0, 0.0, jnp.exp2(s - jnp.max(s, axis=1, keepdims=True)))
    ext = _dot(e.astype(BF16), _with_ones(vc_ref[0, 0]))
    den = jnp.maximum(ext[:, HEAD_DIM:], 1e-30)
    o_cmp4 = ext[:, :HEAD_DIM] / den
    p = e / den
    p_sum = p[0:qb]
    for g in range(1, C_GROUP):
        p_sum = p_sum + p[g * qb:(g + 1) * qb]

    p_hi = p_sum.astype(BF16)
    p_lo = (p_sum - p_hi.astype(F32)).astype(BF16)
    nr = -(-nsel // 8) * 8
    imp = (_dot_nt(ov_ref[...], p_hi) + _dot_nt(ov_ref[...], p_lo))[:nr]
    blk = lax.broadcasted_iota(I32, (nr, qb), 0)
    cur = (t0 + lax.broadcasted_iota(I32, (nr, qb), 1)) // C_SEL_BLOCK
    causal = (blk <= cur) & (blk < nsel)
    forced = (blk == 0) | (blk == cur) | (blk == cur - 1)
    imp = jnp.where(causal, jnp.where(forced, C_FORCE_SCORE, imp), NEG)
    rank = jnp.zeros((nr, qb), I32)
    for jp in range(nsel):
        c = jnp.broadcast_to(imp[jp:jp + 1, :], (nr, qb))
        beats = (c > imp) | ((c == imp) & (blk > jp))
        rank = rank + beats.astype(I32)
    sel_t = jnp.where((rank < C_SEL_TOPN) & causal, 1.0, 0.0)
    sel_t = jnp.concatenate([sel_t, jnp.zeros((LANES - nr, qb), F32)], axis=0)
    sel = sel_t.T.astype(BF16)

    n_tiles = (t0 + qb + tk - 1) // tk
    mx_scr[...] = jnp.full(mx_scr.shape, NEG, F32)

    def score_tile(kt, carry):
        k0 = pl.multiple_of(kt * tk, tk)
        sel_keys = _dot(sel, ex_ref[kt])
        ok = (sel_keys > 0.5) & (k0 + lax.broadcasted_iota(I32, (qb, tk), 1) <= t_rows(tk))
        s = _dot_nt(qr4, ks_ref[0, pl.ds(k0, tk), :].astype(BF16)) + stacked_bias(ok)
        s_scr[kt] = s
        mx = mx_scr[...]
        for c in range(tk // LANES):
            mx = jnp.maximum(mx, s[:, c * LANES:(c + 1) * LANES])
        mx_scr[...] = mx
        return carry

    lax.fori_loop(0, n_tiles, score_tile, 0)
    m_slc = jnp.max(mx_scr[...], axis=1, keepdims=True)

    ext_scr[...] = jnp.zeros(ext_scr.shape, F32)

    def value_tile(kt, carry):
        k0 = pl.multiple_of(kt * tk, tk)
        p = jnp.exp2(s_scr[kt] - m_slc).astype(BF16)
        ext_scr[...] += _dot(p, _with_ones(vs_ref[0, pl.ds(k0, tk), :]))
        return carry

    lax.fori_loop(0, n_tiles, value_tile, 0)
    o_slc4 = ext_scr[:, :HEAD_DIM] / ext_scr[:, HEAD_DIM:]

    for g in range(C_GROUP):
        rows = slice(g * qb, (g + 1) * qb)
        o = (gate_ref[:, 3 * g:3 * g + 1] * o_cmp4[rows] + gate_ref[:, 3 * g + 1:3 * g + 2] * o_slc4[rows]
             + gate_ref[:, 3 * g + 2:3 * g + 3] * o_win4[rows])
        o_ref[:, g * HEAD_DIM:(g + 1) * HEAD_DIM] = o.astype(o_ref.dtype)


def nsa_prompt(z, gates, cos, sin, kcv, overlap_t, expand, batch, seq):
    m, nc = z.shape
    d = C_KV_HEADS * C_GROUP * HEAD_DIM
    qb = C_QBLOCK
    nq = seq // qb
    nsel = -(-seq // C_SEL_BLOCK)
    tk = C_KEY_TILE
    assert nsel <= LANES and seq % qb == 0 and seq >= C_WINDOW + qb and seq % tk == 0
    assert expand.shape == (seq // tk, LANES, tk)
    nch = kcv.shape[2]
    assert nch == HEAD_DIM and qb == LANES and overlap_t.shape == (LANES, nch)
    zv = z.reshape(batch, seq, nc)
    cb = d // HEAD_DIM
    kvw = C_KV_HEADS

    def kv_spec(branch, kv):
        off = cb + (branch * 2 + kv) * kvw
        return pl.BlockSpec((1, seq, HEAD_DIM), lambda b, k, q: (b, 0, off + k))

    return pl.pallas_call(
        functools.partial(_nsa_prompt_kernel, nsel=nsel),
        out_shape=jax.ShapeDtypeStruct((m, d), BF16),
        scratch_shapes=[pltpu.VMEM((seq // tk, C_GROUP * qb, tk), F32),
                        pltpu.VMEM((C_GROUP * qb, LANES), F32),
                        pltpu.VMEM((C_GROUP * qb, 2 * HEAD_DIM), F32)],
        grid=(batch, C_KV_HEADS, nq),
        in_specs=[
            pl.BlockSpec((qb, C_GROUP * HEAD_DIM), lambda b, k, q: (b * nq + q, k)),
            pl.BlockSpec((qb, LANES), lambda b, k, q: (b * nq + q, k)),
            pl.BlockSpec((qb, HEAD_DIM), lambda b, k, q: (q, 0)),
            pl.BlockSpec((qb, HEAD_DIM), lambda b, k, q: (q, 0)),
            pl.BlockSpec((1, 1, nch, HEAD_DIM), lambda b, k, q: (b, k, 0, 0)),
            pl.BlockSpec((1, 1, nch, HEAD_DIM), lambda b, k, q: (b, C_KV_HEADS + k, 0, 0)),
            kv_spec(1, 0), kv_spec(1, 1), kv_spec(2, 0), kv_spec(2, 1),
            pl.BlockSpec(overlap_t.shape, lambda b, k, q: (0, 0)),
            pl.BlockSpec(expand.shape, lambda b, k, q: (0, 0, 0)),
        ],
        out_specs=pl.BlockSpec((qb, C_GROUP * HEAD_DIM), lambda b, k, q: (b * nq + q, k)),
        compiler_params=_cparams(3),
        name="c_nsa_prompt",
    )(z, gates, cos, sin, kcv, kcv, zv, zv, zv, zv, overlap_t, expand)


def _nsa_sample_cmp_kernel(q_ref, kc_ref, vc_ref, ov_ref, ocmp_ref, idx_ref, *, t, nsel):
    nch = kc_ref.shape[2]
    scale = HEAD_DIM ** -0.5
    cmp_ok = lax.broadcasted_iota(I32, (8, nch), 1) * C_CMP_STRIDE + (C_CMP_LEN - 1) <= t
    row = lax.broadcasted_iota(I32, (8, nch), 0)
    p_sum = jnp.zeros((8, nch), F32)
    for k in range(C_KV_HEADS):
        q = (q_ref[0, k] * scale).astype(BF16)
        e, den = _masked_softmax_rows(_dot_nt(q, kc_ref[0, k].astype(BF16)), cmp_ok)
        p = e / den
        ocmp_ref[0, k] = _dot(p.astype(BF16), vc_ref[0, k].astype(BF16))
        p_k = jnp.sum(jnp.where(row < C_GROUP, p, 0.0), axis=0, keepdims=True)
        p_sum = jnp.where(row == k, jnp.broadcast_to(p_k, (8, nch)), p_sum)
    p_hi = p_sum.astype(BF16)
    p_lo = (p_sum - p_hi.astype(F32)).astype(BF16)
    imp = _dot(p_hi, ov_ref[...]) + _dot(p_lo, ov_ref[...])
    width = imp.shape[1]
    lane = lax.broadcasted_iota(I32, (8, width), 1)
    cur = t // C_SEL_BLOCK
    causal = (lane <= cur) & (lane < nsel)
    forced = (lane == 0) | (lane == cur) | (lane == cur - 1)
    work = jnp.where(causal, jnp.where(forced, C_FORCE_SCORE, imp), NEG)
    lane_f = lane.astype(F32)
    out_lane = lax.broadcasted_iota(I32, (8, LANES), 1)
    idx = jnp.full((8, LANES), -1, I32)
    for r in range(C_SEL_TOPN):
        best = jnp.max(work, axis=1, keepdims=True)
        pick = jnp.min(jnp.where(work == best, lane_f, float(width)), axis=1, keepdims=True)
        found = jnp.where(best > 0.5 * NEG, pick, -1.0).astype(I32)
        idx = jnp.where(out_lane == r, found, idx)
        work = jnp.where(lane_f == pick, NEG, work)
    idx_ref[0] = idx


def nsa_sample_cmp(q8, kcv, overlap, t, nsel):
    n_batch = q8.shape[0]
    nch = kcv.shape[2]
    assert C_KV_HEADS <= 8
    blk = pl.BlockSpec((1, C_KV_HEADS, 8, HEAD_DIM), lambda b: (b, 0, 0, 0))
    return pl.pallas_call(
        functools.partial(_nsa_sample_cmp_kernel, t=t, nsel=nsel),
        out_shape=[jax.ShapeDtypeStruct((n_batch, C_KV_HEADS, 8, HEAD_DIM), F32),
                   jax.ShapeDtypeStruct((n_batch, 8, LANES), I32)],
        grid=(n_batch,),
        in_specs=[
            blk,
            pl.BlockSpec((1, C_KV_HEADS, nch, HEAD_DIM), lambda b: (b, 0, 0, 0)),
            pl.BlockSpec((1, C_KV_HEADS, nch, HEAD_DIM), lambda b: (b, 1, 0, 0)),
            pl.BlockSpec(overlap.shape, lambda b: (0, 0)),
        ],
        out_specs=[blk, pl.BlockSpec((1, 8, LANES), lambda b: (b, 0, 0))],
        compiler_params=_cparams(1),
        name="c_nsa_sample_cmp",
    )(q8, kcv, kcv, overlap)


def _nsa_sample_kernel(idx_ref, pt_ref, q_ref, cos_ref, sin_ref, *refs, t, n_past_blocks):
    del pt_ref
    slc_refs, (new_ref, win_ref, ocmp_ref, gate_ref, o_ref) = refs[:C_SEL_TOPN], refs[C_SEL_TOPN:]
    b, k = pl.program_id(0), pl.program_id(1)

    def head_rows(ref, kv):
        return ref[:, kv, pl.ds(k, 1), :][:, 0, :].astype(BF16)

    base = (b * C_KV_HEADS + k) * C_SEL_TOPN
    scale = HEAD_DIM ** -0.5
    qr_f = _rope(q_ref[0, 0], cos_ref[...], sin_ref[...]) * scale
    qr = qr_f.astype(BF16)

    nk = C_SEL_TOPN * C_SEL_BLOCK
    s = _dot_nt(qr, jnp.concatenate([head_rows(r, 0) for r in slc_refs], axis=0))
    lane = lax.broadcasted_iota(I32, (8, nk), 1)
    blk = jnp.full((8, nk), -1, I32)
    has_new = False
    for n in range(C_SEL_TOPN):
        blk_n = idx_ref[base + n]
        blk = jnp.where((lane >= n * C_SEL_BLOCK) & (lane < (n + 1) * C_SEL_BLOCK), blk_n, blk)
        has_new = has_new | (blk_n == n_past_blocks)
    ok = (blk >= 0) & (blk < n_past_blocks) & (blk * C_SEL_BLOCK + (lane & (C_SEL_BLOCK - 1)) <= t)
    k_new, v_new = new_ref[0, 0, 0:1, :], new_ref[0, 0, 1:2, :]
    s = jnp.where(ok, s, NEG)
    s_new = jnp.where(has_new, jnp.sum(qr_f * k_new, axis=1, keepdims=True), NEG)
    m = jnp.maximum(jnp.max(s, axis=1, keepdims=True), s_new)
    p = jnp.where(ok, jnp.exp(s - m), 0.0)
    p_new = jnp.where(has_new, jnp.exp(s_new - m), 0.0)
    den = jnp.maximum(jnp.sum(p, axis=1, keepdims=True) + p_new, 1e-30)
    vs = jnp.concatenate([head_rows(r, 1) for r in slc_refs], axis=0)
    o_slc = (_dot(p.astype(BF16), vs) + p_new * v_new) / den

    lw = win_ref.shape[0]
    kw_new, vw_new = new_ref[0, 0, 2:3, :], new_ref[0, 0, 3:4, :]
    dist = lw - lax.broadcasted_iota(I32, (8, lw), 1)
    w_ok = (dist >= 0) & (dist < C_WINDOW)
    s = jnp.where(w_ok, _dot_nt(qr, head_rows(win_ref, 0)), NEG)
    s_new = jnp.sum(qr_f * kw_new, axis=1, keepdims=True)
    m = jnp.maximum(jnp.max(s, axis=1, keepdims=True), s_new)
    p = jnp.where(w_ok, jnp.exp(s - m), 0.0)
    p_new = jnp.exp(s_new - m)
    den = jnp.sum(p, axis=1, keepdims=True) + p_new
    o_win = (_dot(p.astype(BF16), head_rows(win_ref, 1)) + p_new * vw_new) / den

    gate = gate_ref[0, 0]
    o_ref[0, 0] = gate[:, 0:1] * ocmp_ref[0, 0] + gate[:, 1:2] * o_slc + gate[:, 2:3] * o_win


def nsa_sample(top_idx, page_table, q8, cos, sin, pool_slc, win_buf, li, new_kv, ocmp, gate8, t, n_past_blocks):
    n_batch = q8.shape[0]
    n_pages = page_table.shape[1]
    n_layers, n_pool, page_rows = pool_slc.shape[:3]
    blocks_per_page = page_rows // C_SEL_BLOCK
    kv_dims = (2, C_KV_HEADS, HEAD_DIM)
    pool = pool_slc.reshape((n_layers, n_pool, blocks_per_page, C_SEL_BLOCK) + kv_dims)
    lw = win_buf.shape[2]

    def slc_spec(n):
        def index(b, k, idx, pt):
            blk = jnp.clip(idx[(b * C_KV_HEADS + k) * C_SEL_TOPN + n], 0, n_past_blocks - 1)
            return (li, pt[b * n_pages + blk // blocks_per_page], blk % blocks_per_page, 0, 0, 0, 0)
        return pl.BlockSpec((None, None, None, C_SEL_BLOCK) + kv_dims, index)

    win_spec = pl.BlockSpec((None, None, lw) + kv_dims, lambda b, k, idx, pt: (li, b, 0, 0, 0, 0))
    blk8 = pl.BlockSpec((1, 1, 8, HEAD_DIM), lambda b, k, idx, pt: (b, k, 0, 0))
    tab = pl.BlockSpec((8, HEAD_DIM), lambda b, k, idx, pt: (0, 0))
    return pl.pallas_call(
        functools.partial(_nsa_sample_kernel, t=t, n_past_blocks=n_past_blocks),
        out_shape=jax.ShapeDtypeStruct((n_batch, C_KV_HEADS, 8, HEAD_DIM), F32),
        grid_spec=pltpu.PrefetchScalarGridSpec(
            num_scalar_prefetch=2,
            grid=(n_batch, C_KV_HEADS),
            in_specs=[blk8, tab, tab] + [slc_spec(n) for n in range(C_SEL_TOPN)] + [blk8, win_spec, blk8, blk8],
            out_specs=blk8,
        ),
        compiler_params=_cparams(2),
        name="c_nsa_sample",
    )(top_idx.reshape(-1), page_table.reshape(-1), q8, cos, sin, *([pool] * C_SEL_TOPN), new_kv, win_buf, ocmp, gate8)


def _shift_append_kernel(cur_ref, nxt_ref, new_ref, out_ref):
    r = cur_ref.shape[0]
    out_ref[0:r - 1] = cur_ref[1:r]
    last = pl.program_id(1) == pl.num_programs(1) - 1

    @pl.when(last)
    def _():
        out_ref[r - 1:r] = new_ref[...]

    @pl.when(jnp.logical_not(last))
    def _():
        out_ref[r - 1:r] = nxt_ref[...]


def shift_append(cache, li, new, name):
    n_batch, length = cache.shape[1:3]
    tail = cache.shape[3:]
    assert new.shape == (n_batch, 1) + tail
    r = min(length, SHIFT_ROWS)
    assert length % r == 0
    zeros = (0,) * len(tail)
    return pl.pallas_call(
        _shift_append_kernel,
        out_shape=jax.ShapeDtypeStruct(cache.shape[1:], cache.dtype),
        grid=(n_batch, length // r),
        in_specs=[
            pl.BlockSpec((None, None, r) + tail, lambda b, i: (li, b, i) + zeros),
            pl.BlockSpec((None, None, 1) + tail, lambda b, i: (li, b, jnp.minimum((i + 1) * r, length - 1)) + zeros),
            pl.BlockSpec((None, 1) + tail, lambda b, i: (b, 0) + zeros),
        ],
        out_specs=pl.BlockSpec((None, r) + tail, lambda b, i: (b, i) + zeros),
        compiler_params=_cparams(2),
        name=name,
    )(cache, cache, new)


def _rope_tables(pos):
    half = HEAD_DIM // 2
    inv = ROPE_THETA ** (-(jnp.arange(half, dtype=F32) * 2.0 / HEAD_DIM))
    ang = pos.astype(F32)[:, None] * inv[None, :]
    cos, sin = jnp.cos(ang), jnp.sin(ang)
    return jnp.concatenate([cos, cos], axis=1), jnp.concatenate([-sin, sin], axis=1)


def _overlap_matrix(nch, nsel, width):
    cmp_start = jnp.arange(nch) * C_CMP_STRIDE
    sel_start = jnp.arange(width) * C_SEL_BLOCK
    ov = ((cmp_start[:, None] < sel_start[None, :] + C_SEL_BLOCK)
          & (cmp_start[:, None] + C_CMP_LEN > sel_start[None, :])
          & (jnp.arange(width)[None, :] < nsel))
    return ov.astype(BF16)


def _c_weights(prm, li, n_heads):
    d = n_heads * HEAD_DIM
    nkv = 3 * 2 * C_KV_HEADS * HEAD_DIM
    k_gain = prm['c_k_norm'][li]
    ones = jnp.ones((C_KV_HEADS * HEAD_DIM,), F32)
    head_gain = jnp.concatenate([
        jnp.tile(prm['c_q_norm'][li], n_heads), ones, ones,
        jnp.tile(k_gain[1], C_KV_HEADS), ones, jnp.tile(k_gain[2], C_KV_HEADS), ones])[None, :]
    wg = prm['w_c_in'][li][:, d + nkv:].reshape(-1, C_KV_HEADS, C_GROUP * 3)
    wg = jnp.pad(wg, ((0, 0), (0, 0), (0, LANES - C_GROUP * 3))).reshape(1, -1, C_KV_HEADS * LANES)
    w1 = prm['c_cmp_w1'][li]
    w_cat = jnp.concatenate([w1[:, :C_CMP_STRIDE], w1[:, C_CMP_STRIDE:]], axis=-1).astype(BF16)
    w_cat = w_cat.reshape(2, C_CMP_STRIDE // 2, 2 * HEAD_DIM, w_cat.shape[-1])
    pe8 = jnp.pad(prm['c_cmp_pe'][li].reshape(2, 1, -1), ((0, 0), (0, 7), (0, 0)))
    w1_flat = w1.reshape(2, C_CMP_LEN * HEAD_DIM, -1)
    return head_gain, wg, w_cat, pe8, w1_flat, prm['c_cmp_w2'][li], k_gain[0][None, :]


TN = 512
TN_WIDE_K = 256
TM_OUT_PROJ = 2048
C_MODES = (NORM,) * 4 + (RAW, RAW, NORM_ROPE, RAW, NORM_ROPE, RAW)
A_MODES = ((NORM_ROPE,) * 8 + (RAW,) * 4) * len(A_PATTERNS)


def _pad_rows(a, rows):
    return jnp.pad(a, ((0, rows - a.shape[0]),) + ((0, 0),) * (a.ndim - 1))


def _a_head_gain(prm, li, n_heads):
    ones = jnp.ones((n_heads * HEAD_DIM,), F32)
    parts = []
    for g in range(len(A_PATTERNS)):
        parts += [jnp.tile(prm['a_q_norm'][li, g], n_heads), jnp.tile(prm['a_k_norm'][li, g], n_heads), ones]
    return jnp.concatenate(parts)[None, :]


def _channel_mixer(x, p, prm, i, tag):
    act = yield ('swiglu_in', x, dict(gain=prm['ffn_norm'], w=prm['w_ffn_in'], li=i, tn=TN, name="ffn_in"))
    x = linear_residual(act, prm['w_ffn_out'], i, x, tn=TN_WIDE_K, name=f"ffn_out_{tag}")
    x = yield ('ple', x, dict(gain=prm['ple_norm'], wp=prm['w_ple_proj'], wg=prm['w_ple_gate'], li=i, tn=TN,
                              name="ple"), p)
    return x


def _serve(req_p, req_s):
    kind, x, kw = req_p[:3]
    xs = req_s[1]
    assert kind == req_s[0]
    if kind == 'norm_linear':
        return norm_linear(x, xs, rope=req_p[3], rope_s=req_s[3], **kw)
    if kind == 'swiglu_in':
        return swiglu_in(x, xs, **kw)
    if kind == 'ple':
        return ple(x, xs, p=req_p[3], ps=req_s[3], **kw)
    bg, u, bg_s, u_s = b_in(x, xs, **kw)
    return (bg, u), (bg_s, u_s)


def _run_both(walk_p, walk_s):
    req_p, req_s = next(walk_p), next(walk_s)
    while True:
        out_p, out_s = _serve(req_p, req_s)
        try:
            req_p = walk_p.send(out_p)
        except StopIteration as done_p:
            try:
                walk_s.send(out_s)
            except StopIteration as done_s:
                return done_p.value, done_s.value
            raise AssertionError("decode walk outlived the prompt walk")
        req_s = walk_s.send(out_s)


def _run_prompt(x3, p, prm):
    batch, seq, d = x3.shape
    n_heads = d // HEAD_DIM
    x = x3.reshape(batch * seq, d)
    cos, sin = _rope_tables(jnp.arange(seq))
    new = {}
    depth = prm['attn_norm'].shape[0]
    for i in range(depth):
        kind, li = i % 4, i // 4
        if kind == 0:
            z = yield ('norm_linear', x, dict(gain=prm['attn_norm'], gi=i, w=prm['w_a_qkv'], wi=li, col0=0,
                                              modes=A_MODES, head_gain=_a_head_gain(prm, li, n_heads), tn=TN,
                                              name="a_qkv"), (cos, sin))
            x = linear_residual(a_attention_prompt(z, batch, seq), prm['w_a_out'], li, x, tn=TN, tm=TM_OUT_PROJ,
                                name="a_out_prompt")
            z3 = z.reshape(batch, seq, -1)
            for g, (win, dil) in enumerate(A_PATTERNS):
                kv = z3[:, seq - min(win, seq):, (3 * g + 1) * d:(3 * g + 3) * d]
                new.setdefault(f'a_w{g + 1}', []).append(kv.reshape(batch, -1, 2, n_heads, HEAD_DIM))
        elif kind == 1:
            bg, u = yield ('b_in', x, dict(gain=prm['attn_norm'], gi=i, w=prm['w_b_in'], li=li, tn=TN_WIDE_K,
                                           name="b_in"))
            x = b_out_prompt(u, bg, prm['b_conv'], prm['w_b_out'], li, x, seq, tn=TN, name="b_out_prompt")
            new.setdefault('b_conv', []).append(u.reshape(batch, seq, d)[:, seq - (CONV_W - 1):])
        elif kind == 2:
            head_gain, wg, w_cat, pe8, w1_flat, w2, k_gain0 = _c_weights(prm, li, n_heads)
            z = yield ('norm_linear', x, dict(gain=prm['attn_norm'], gi=i, w=prm['w_c_in'], wi=li, col0=0,
                                              modes=C_MODES, head_gain=head_gain, tn=TN, name="c_in"), (cos, sin))
            gates = yield ('norm_linear', x, dict(gain=prm['attn_norm'], gi=i, w=wg, wi=0, col0=0, modes=(SIGMOID,),
                                                  head_gain=head_gain[:, :TN], tn=TN, name="c_gate"), (cos, sin))
            n_pages = seq // LANES
            pool = z.reshape(batch * n_pages, LANES, z.shape[1])
            kv_width = 2 * C_KV_HEADS * HEAD_DIM
            fs = cmp_first_second(pool, jnp.arange(batch * n_pages, dtype=I32), d // kv_width, w_cat, batch, n_pages)
            kcv = cmp_mlp(fs, cmp_pe_term(pe8, w1_flat), w2, k_gain0)
            nsel = -(-seq // C_SEL_BLOCK)
            overlap = _overlap_matrix(kcv.shape[2], nsel, LANES)
            key_blk = (jnp.arange(seq) // C_SEL_BLOCK).reshape(seq // C_KEY_TILE, 1, C_KEY_TILE)
            expand = (key_blk == jnp.arange(LANES)[None, :, None]).astype(BF16)
            o = nsa_prompt(z, gates, cos, sin, kcv, overlap.T, expand, batch, seq)
            x = linear_residual(o, prm['w_c_out'], li, x, tn=TN, tm=TM_OUT_PROJ, name="c_out_prompt")
            z3 = z.reshape(batch, seq, -1)
            for br, nm in enumerate(('c_cmp', 'c_slc', 'c_win')):
                rows = min(C_WINDOW, seq) if nm == 'c_win' else seq
                kv = z3[:, seq - rows:, d + br * kv_width:d + (br + 1) * kv_width]
                new.setdefault(nm, []).append(kv.reshape(batch, rows, 2, C_KV_HEADS, HEAD_DIM))
        else:
            x, tail = d_mixer_prompt(x, prm['attn_norm'], i, prm['w_d_group'], prm['d_scale'], li, seq,
                                     name="d_mixer_prompt")
            new.setdefault('d_pool', []).append(tail[:, 1:])
        x = yield from _channel_mixer(x, p, prm, i, "prompt")
    return x.reshape(batch, seq, d), {nm: jnp.stack(v, axis=0) for nm, v in new.items()}


def _run_sample(x3, p, prm, past, page_table):
    n_batch, n_new, d = x3.shape
    assert n_new == 1
    n_heads = d // HEAD_DIM
    rows = SUBLANES_BF16
    x = _pad_rows(x3.reshape(n_batch, d), rows)
    past_len = page_table.shape[1] * past['c_cmp'].shape[2]
    cos, sin = _rope_tables(jnp.full((rows,), past_len))
    new = {}
    depth = prm['attn_norm'].shape[0]
    for i in range(depth):
        kind, li = i % 4, i // 4
        if kind == 0:
            z = yield ('norm_linear', x, None, (cos, sin))
            caches =[past[f'a_w{g + 1}'] for g in range(len(A_PATTERNS))]
            o = a_attention_sample(z, caches, li, n_batch)
            x = linear_residual(_pad_rows(o, rows), prm['w_a_out'], li, x, tn=TN, name="a_out_sample")
            for g, cache in enumerate(caches):
                kv = z[:n_batch, (3 * g + 1) * d:(3 * g + 3) * d].reshape(n_batch, 1, 2, n_heads, HEAD_DIM)
                new.setdefault(f'a_w{g + 1}', []).append(shift_append(cache, li, kv, f"a_w{g + 1}_shift"))
        elif kind == 1:
            bg, u = yield ('b_in', x, None)
            hist = past['b_conv'][li]
            x = b_out_sample(u, _pad_rows(hist[:, 1], rows), _pad_rows(hist[:, 0], rows), bg, prm['b_conv'],
                             prm['w_b_out'], li, x, tn=TN, name="b_out_sample")
            new.setdefault('b_conv', []).append(jnp.concatenate([hist[:, 1:], u[:n_batch, None]], axis=1))
        elif kind == 2:
            head_gain, wg, w_cat, pe8, w1_flat, w2, k_gain0 = _c_weights(prm, li, n_heads)
            z = yield ('norm_linear', x, None, (cos, sin))
            gates = yield ('norm_linear', x, None, (cos, sin))
            kv_width = 2 * C_KV_HEADS * HEAD_DIM
            kcv = cmp_mlp(cmp_first_second_paged(past['c_cmp'], li, page_table, w_cat),
                          cmp_pe_term(pe8, w1_flat), w2, k_gain0)
            nsel = -(-(past_len + 1) // C_SEL_BLOCK)
            width = -(-nsel // LANES) * LANES
            overlap = _overlap_matrix(kcv.shape[2], nsel, width)
            zb = z[:n_batch]
            q8 = jnp.pad(zb[:, :d].reshape(n_batch, C_KV_HEADS, C_GROUP, HEAD_DIM),
                         ((0, 0), (0, 0), (0, 8 - C_GROUP), (0, 0)))
            ocmp, idx = nsa_sample_cmp(q8, kcv, overlap, past_len, nsel)
            top_idx = idx[:, :C_KV_HEADS, :C_SEL_TOPN]
            kvh_w = C_KV_HEADS * HEAD_DIM
            new_rows = [zb[:, d + kv_width + j * kvh_w:d + kv_width + (j + 1) * kvh_w]
                        .reshape(n_batch, C_KV_HEADS, 1, HEAD_DIM) for j in range(4)]
            new_kv = jnp.pad(jnp.concatenate(new_rows, axis=2), ((0, 0), (0, 0), (0, 4), (0, 0)))
            gate8 = gates[:n_batch].reshape(n_batch, C_KV_HEADS, LANES)[:, :, :C_GROUP * 3]
            gate8 = jnp.pad(gate8.reshape(n_batch, C_KV_HEADS, C_GROUP, 3),
                            ((0, 0), (0, 0), (0, 8 - C_GROUP), (0, HEAD_DIM - 3)))
            o = nsa_sample(top_idx, page_table, q8, cos[:8], sin[:8], past['c_slc'], past['c_win'], li, new_kv,
                           ocmp, gate8, past_len, past_len // C_SEL_BLOCK)
            o = o[:, :, :C_GROUP].reshape(n_batch, d)
            x = linear_residual(_pad_rows(o, rows), prm['w_c_out'], li, x, tn=TN, name="c_out_sample")
            for br, nm in enumerate(('c_cmp', 'c_slc', 'c_win')):
                kv = zb[:, d + br * kv_width:d + (br + 1) * kv_width].reshape(n_batch, 1, 2, C_KV_HEADS, HEAD_DIM)
                if nm == 'c_win':
                    kv = shift_append(past['c_win'], li, kv, "c_win_shift")
                new.setdefault(nm, []).append(kv)
        else:
            hist = past['d_pool'][li]
            hist_t = jnp.pad(jnp.swapaxes(hist, 0, 1), ((0, 0), (0, rows - n_batch), (0, 0)))
            x, h = d_mixer_sample(x, hist_t, prm['attn_norm'], i, prm['w_d_group'], prm['d_scale'], li,
                                  name="d_mixer_sample")
            new.setdefault('d_pool', []).append(jnp.concatenate([hist[:, 1:], h[:n_batch, None]], axis=1))
        x = yield from _channel_mixer(x, p, prm, i, "sample")
    return x[:n_batch].reshape(n_batch, 1, d), {nm: jnp.stack(v, axis=0) for nm, v in new.items()}


def kernel(x_prompt, x_sample, cache_a_w1, cache_a_w2, cache_a_w3, state_b_conv, cache_c_cmp, cache_c_slc, cache_c_win, state_d_pool, page_table, p_prompt, p_sample, attn_norm, ffn_norm, ple_norm, w_a_qkv, a_q_norm, a_k_norm, w_a_out, w_b_in, b_conv, w_b_out, w_c_in, c_q_norm, c_k_norm, c_cmp_pe, c_cmp_w1, c_cmp_w2, w_c_out, w_d_group, d_scale, w_ffn_in, w_ffn_out, w_ple_proj, w_ple_gate):
    prm = dict(attn_norm=attn_norm, ffn_norm=ffn_norm, ple_norm=ple_norm, w_a_qkv=w_a_qkv, a_q_norm=a_q_norm,
               a_k_norm=a_k_norm, w_a_out=w_a_out, w_b_in=w_b_in, b_conv=b_conv, w_b_out=w_b_out, w_c_in=w_c_in,
               c_q_norm=c_q_norm, c_k_norm=c_k_norm, c_cmp_pe=c_cmp_pe, c_cmp_w1=c_cmp_w1, c_cmp_w2=c_cmp_w2,
               w_c_out=w_c_out, w_d_group=w_d_group, d_scale=d_scale, w_ffn_in=w_ffn_in, w_ffn_out=w_ffn_out,
               w_ple_proj=w_ple_proj, w_ple_gate=w_ple_gate)
    past = dict(a_w1=cache_a_w1, a_w2=cache_a_w2, a_w3=cache_a_w3, b_conv=state_b_conv, c_cmp=cache_c_cmp,
                c_slc=cache_c_slc, c_win=cache_c_win, d_pool=state_d_pool)
    depth = attn_norm.shape[0]
    batch, seq, _ = x_prompt.shape
    n_dec = x_sample.shape[0]
    p_p = p_prompt.reshape(depth, batch * seq, -1)
    p_s = jnp.pad(p_sample.reshape(depth, n_dec, -1), ((0, 0), (0, SUBLANES_BF16 - n_dec), (0, 0)))
    (y_prompt, sp), (y_sample, ss) = _run_both(_run_prompt(x_prompt, p_p, prm),
                                               _run_sample(x_sample, p_s, prm, past, page_table))
    return (y_prompt, y_sample,
            sp['a_w1'], ss['a_w1'], sp['a_w2'], ss['a_w2'], sp['a_w3'], ss['a_w3'],
            sp['b_conv'], ss['b_conv'],
            sp['c_cmp'], ss['c_cmp'], sp['c_slc'], ss['c_slc'], sp['c_win'], ss['c_win'],
            sp['d_pool'], ss['d_pool'])
```

```python
import functools

import jax
import jax.numpy as jnp
from jax import lax
from jax.experimental import pallas as pl
from jax.experimental.pallas import tpu as pltpu

F32 = jnp.float32
BF16 = jnp.bfloat16
I32 = jnp.int32

HEAD_DIM = 128
LANES = 128
SUBLANES_BF16 = 16
RMS_EPS = 1e-6
ROPE_THETA = 10000.0
NEG = -1e30
V7X_VMEM_BYTES = 64 * 1024 * 1024
VMEM_LIMIT = V7X_VMEM_BYTES - 8 * 1024 * 1024

A_PATTERNS = ((128, 1), (512, 4), (2048, 16))
A_BLOCK = 128
A_UNROLL = 8
CONV_W = 3
C_KV_HEADS = 4
C_GROUP = 4
C_CMP_LEN = 32
C_CMP_STRIDE = 16
C_SEL_BLOCK = 64
C_SEL_TOPN = 16
C_WINDOW = 512
C_FORCE_SCORE = 1e4
C_QBLOCK = 128
C_KEY_TILE = 512
LOG2_E = 1.4426950408889634
D_POOLS = (2, 4, 8, 16)
D_HIST = max(D_POOLS) - 1
PAGES_PER_GROUP = 16
SHIFT_ROWS = 256

RAW, NORM, NORM_ROPE, SIGMOID = range(4)


def _cparams(n_axes):
    return pltpu.CompilerParams(dimension_semantics=("arbitrary",) * n_axes,
                                vmem_limit_bytes=VMEM_LIMIT)


def _row_tile(m):
    return 1024 if m % 1024 == 0 else m


def _dot(a, b):
    return jnp.dot(a, b, preferred_element_type=F32)


def _dot_nt(a, b):
    return lax.dot_general(a, b, (((1,), (1,)), ((), ())), preferred_element_type=F32)


def _with_ones(v):
    return jnp.concatenate([v.astype(BF16), jnp.ones(v.shape, BF16)], axis=1)


def _rms(x, g):
    return x * lax.rsqrt(jnp.mean(x * x, axis=-1, keepdims=True) + RMS_EPS) * g


def _rope(y, cos, sin):
    return y * cos + pltpu.roll(y, HEAD_DIM // 2, 1) * sin


def _tile_pred(j, tiles):
    pred = None
    lo = prev = tiles[0]
    runs = []
    for t in tiles[1:]:
        if t != prev + 1:
            runs.append((lo, prev))
            lo = t
        prev = t
    runs.append((lo, prev))
    for lo, hi in runs:
        p = (j == lo) if lo == hi else ((j >= lo) & (j <= hi))
        pred = p if pred is None else (pred | p)
    return pred


def _norm_linear_kernel(x_ref, xs_ref, g_ref, w_ref, hg_ref, cos_ref, sin_ref, coss_ref, sins_ref, o_ref, os_ref,
                        h_scr, *, modes):
    j = pl.program_id(1)
    tm = x_ref.shape[0]

    @pl.when(j == 0)
    def _():
        h_scr[0:tm, :] = _rms(x_ref[...], g_ref[...]).astype(BF16)
        h_scr[tm:, :] = _rms(xs_ref[...], g_ref[...]).astype(BF16)

    acc = _dot(h_scr[...], w_ref[...].astype(BF16))
    tn = acc.shape[1]

    def emit(mode):
        for rows, out_ref, c_ref, s_ref in ((slice(0, tm), o_ref, cos_ref, sin_ref),
                                            (slice(tm, None), os_ref, coss_ref, sins_ref)):
            if mode == RAW:
                out_ref[...] = acc[rows].astype(out_ref.dtype)
            elif mode == SIGMOID:
                out_ref[...] = jax.nn.sigmoid(acc[rows]).astype(out_ref.dtype)
            else:
                for c in range(tn // HEAD_DIM):
                    sl = slice(c * HEAD_DIM, (c + 1) * HEAD_DIM)
                    y = _rms(acc[rows, sl], hg_ref[:, sl])
                    if mode == NORM_ROPE:
                        y = _rope(y, c_ref[...], s_ref[...])
                    out_ref[:, sl] = y.astype(out_ref.dtype)

    kinds = sorted(set(modes))
    if len(kinds) == 1:
        emit(kinds[0])
    else:
        for m in kinds:
            tiles = [t for t, mm in enumerate(modes) if mm == m]
            pl.when(_tile_pred(j, tiles))(functools.partial(emit, m))


def _dec_spec(rows, cols):
    return pl.BlockSpec((None, rows, cols), lambda i, j: (i, 0, j))


def _dec_shape(n_row_tiles, rows, cols, dtype):
    return jax.ShapeDtypeStruct((n_row_tiles, rows, cols), dtype)


def norm_linear(x, xs, gain, gi, w, wi, *, col0, modes, head_gain, rope, rope_s, tn, name):
    m, k = x.shape
    ms = xs.shape[0]
    tm = _row_tile(m)
    nt = len(modes)
    (cos, sin), (cos_s, sin_s) = rope, rope_s
    assert col0 % tn == 0 and cos.shape[0] % tm == 0 and cos_s.shape[0] == ms and ms % SUBLANES_BF16 == 0
    coff = col0 // tn
    n_pos_tiles = cos.shape[0] // tm
    full_s = pl.BlockSpec((ms, HEAD_DIM), lambda i, j: (0, 0))
    out, out_s = pl.pallas_call(
        functools.partial(_norm_linear_kernel, modes=tuple(modes)),
        out_shape=[jax.ShapeDtypeStruct((m, nt * tn), F32), _dec_shape(m // tm, ms, nt * tn, F32)],
        grid=(m // tm, nt),
        in_specs=[
            pl.BlockSpec((tm, k), lambda i, j: (i, 0)),
            pl.BlockSpec((ms, k), lambda i, j: (0, 0)),
            pl.BlockSpec((None, 1, k), lambda i, j: (gi, 0, 0)),
            pl.BlockSpec((None, k, tn), lambda i, j: (wi, 0, j + coff)),
            pl.BlockSpec((1, tn), lambda i, j: (0, j)),
            pl.BlockSpec((tm, HEAD_DIM), lambda i, j: (i % n_pos_tiles, 0)),
            pl.BlockSpec((tm, HEAD_DIM), lambda i, j: (i % n_pos_tiles, 0)),
            full_s, full_s,
        ],
        out_specs=[pl.BlockSpec((tm, tn), lambda i, j: (i, j)), _dec_spec(ms, tn)],
        scratch_shapes=[pltpu.VMEM((tm + ms, k), BF16)],
        compiler_params=_cparams(2),
        name=name,
    )(x, xs, gain.reshape(gain.shape[0], 1, k), w, head_gain, cos, sin, cos_s, sin_s)
    return out, out_s[0]


def _linear_res_kernel(a_ref, w_ref, r_ref, o_ref):
    o_ref[...] = r_ref[...] + _dot(a_ref[...].astype(BF16), w_ref[...].astype(BF16))


def linear_residual(a, w, li, res, *, tn, name, tm=None):
    m, k = a.shape
    n = w.shape[-1]
    tm = _row_tile(m) if tm is None or m % tm else tm
    return pl.pallas_call(
        _linear_res_kernel,
        out_shape=jax.ShapeDtypeStruct((m, n), F32),
        grid=(m // tm, n // tn),
        in_specs=[
            pl.BlockSpec((tm, k), lambda i, j: (i, 0)),
            pl.BlockSpec((None, k, tn), lambda i, j: (li, 0, j)),
            pl.BlockSpec((tm, tn), lambda i, j: (i, j)),
        ],
        out_specs=pl.BlockSpec((tm, tn), lambda i, j: (i, j)),
        compiler_params=_cparams(2),
        name=name,
    )(a, w, res)


def _swiglu_in_kernel(x_ref, xs_ref, g_ref, wg_ref, wu_ref, o_ref, os_ref, h_scr):
    tm = x_ref.shape[0]

    @pl.when(pl.program_id(1) == 0)
    def _():
        h_scr[0:tm, :] = _rms(x_ref[...], g_ref[...]).astype(BF16)
        h_scr[tm:, :] = _rms(xs_ref[...], g_ref[...]).astype(BF16)

    h = h_scr[...]
    a = _dot(h, wg_ref[...].astype(BF16))
    b = _dot(h, wu_ref[...].astype(BF16))
    act = (a * jax.nn.sigmoid(a) * b).astype(o_ref.dtype)
    o_ref[...] = act[0:tm]
    os_ref[...] = act[tm:]


def swiglu_in(x, xs, gain, w, li, *, tn, name):
    m, k = x.shape
    ms = xs.shape[0]
    f = w.shape[-1] // 2
    tm = _row_tile(m)
    nt = f // tn
    act, act_s = pl.pallas_call(
        _swiglu_in_kernel,
        out_shape=[jax.ShapeDtypeStruct((m, f), BF16), _dec_shape(m // tm, ms, f, BF16)],
        grid=(m // tm, nt),
        in_specs=[
            pl.BlockSpec((tm, k), lambda i, j: (i, 0)),
            pl.BlockSpec((ms, k), lambda i, j: (0, 0)),
            pl.BlockSpec((None, 1, k), lambda i, j: (li, 0, 0)),
            pl.BlockSpec((None, k, tn), lambda i, j: (li, 0, j)),
            pl.BlockSpec((None, k, tn), lambda i, j: (li, 0, j + nt)),
        ],
        out_specs=[pl.BlockSpec((tm, tn), lambda i, j: (i, j)), _dec_spec(ms, tn)],
        scratch_shapes=[pltpu.VMEM((tm + ms, k), BF16)],
        compiler_params=_cparams(2),
        name=name,
    )(x, xs, gain.reshape(gain.shape[0], 1, k), w, w)
    return act, act_s[0]


def _ple_kernel(x_ref, xs_ref, g_ref, p_ref, ps_ref, wp_ref, wg_ref, o_ref, os_ref, h_scr):
    j = pl.program_id(1)
    tm, tn = o_ref.shape

    @pl.when(j == 0)
    def _():
        h_scr[0:tm, :] = _rms(x_ref[...], g_ref[...]).astype(BF16)
        h_scr[tm:, :] = _rms(xs_ref[...], g_ref[...]).astype(BF16)

    gate = jax.nn.sigmoid(_dot(h_scr[...], wg_ref[...].astype(BF16)))
    p_rows = jnp.concatenate([p_ref[...].astype(BF16), ps_ref[...].astype(BF16)], axis=0)
    upd = _dot(p_rows, wp_ref[...].astype(BF16)) * gate
    cols = pl.ds(pl.multiple_of(j * tn, tn), tn)
    o_ref[...] = x_ref[:, cols] + upd[0:tm]
    os_ref[...] = xs_ref[:, cols] + upd[tm:]


def ple(x, xs, gain, p, ps, wp, wg, li, *, tn, name):
    m, k = x.shape
    ms = xs.shape[0]
    pd = p.shape[-1]
    tm = _row_tile(m)
    assert p.shape[1] == m and ps.shape[1] == ms
    out, out_s = pl.pallas_call(
        _ple_kernel,
        out_shape=[jax.ShapeDtypeStruct((m, k), F32), _dec_shape(m // tm, ms, k, F32)],
        grid=(m // tm, k // tn),
        in_specs=[
            pl.BlockSpec((tm, k), lambda i, j: (i, 0)),
            pl.BlockSpec((ms, k), lambda i, j: (0, 0)),
            pl.BlockSpec((None, 1, k), lambda i, j: (li, 0, 0)),
            pl.BlockSpec((None, tm, pd), lambda i, j: (li, i, 0)),
            pl.BlockSpec((None, ms, pd), lambda i, j: (li, 0, 0)),
            pl.BlockSpec((None, pd, tn), lambda i, j: (li, 0, j)),
            pl.BlockSpec((None, k, tn), lambda i, j: (li, 0, j)),
        ],
        out_specs=[pl.BlockSpec((tm, tn), lambda i, j: (i, j)), _dec_spec(ms, tn)],
        scratch_shapes=[pltpu.VMEM((tm + ms, k), BF16)],
        compiler_params=_cparams(2),
        name=name,
    )(x, xs, gain.reshape(gain.shape[0], 1, k), p, ps, wp, wg)
    return out, out_s[0]


def _a_prompt_kernel(*refs, seq):
    qkv_refs, o_ref = refs[:9], refs[9]
    acc_scr, m_scr, l_scr = refs[10:]
    scale = HEAD_DIM ** -0.5
    row = lax.broadcasted_iota(I32, (A_BLOCK, A_BLOCK), 0)
    col = lax.broadcasted_iota(I32, (A_BLOCK, A_BLOCK), 1)
    cur_ok = col <= row
    prev_ok = col >= row
    wide = (A_BLOCK, HEAD_DIM)
    first = len(A_PATTERNS) - 1
    for g, (win, dil) in reversed(list(enumerate(A_PATTERNS))):
        q_ref, k_ref, v_ref = qkv_refs[3 * g:3 * g + 3]
        nb = seq // dil // A_BLOCK

        def rows_of(r, n, dil=dil):
            start = r + dil * A_BLOCK * n
            if dil == 1:
                return pl.ds(pl.multiple_of(start, A_BLOCK), A_BLOCK)
            return pl.ds(start, A_BLOCK, stride=dil)

        def attend(idx, nb=nb, q_ref=q_ref, k_ref=k_ref, v_ref=v_ref, rows_of=rows_of):
            r, n = idx // nb, idx % nb
            rows = rows_of(r, n)
            q = (q_ref[0, rows, :] * scale).astype(BF16)
            s_c = jnp.where(cur_ok, _dot_nt(q, k_ref[0, rows, :].astype(BF16)), NEG)
            if nb > 1:
                prows = rows_of(r, jnp.maximum(n - 1, 0))
                s_p = jnp.where(prev_ok & (n > 0), _dot_nt(q, k_ref[0, prows, :].astype(BF16)), NEG)
                m = jnp.max(jnp.maximum(s_c, s_p), axis=1, keepdims=True)
            else:
                m = jnp.max(s_c, axis=1, keepdims=True)
            ext = _dot(jnp.exp(s_c - m).astype(BF16), _with_ones(v_ref[0, rows, :]))
            if nb > 1:
                ext = ext + _dot(jnp.exp(s_p - m).astype(BF16), _with_ones(v_ref[0, prows, :]))
            return rows, m, ext[:, :HEAD_DIM], ext[:, HEAD_DIM:]

        def block(idx, carry, g=g, attend=attend):
            rows, m, acc, den = attend(idx)
            if g == first:
                acc_scr[rows, :] = acc
                m_scr[rows, :] = jnp.broadcast_to(m, wide)
                l_scr[rows, :] = den
            else:
                m_old = m_scr[rows, :]
                m_new = jnp.maximum(m_old, m)
                a_old, a_blk = jnp.exp(m_old - m_new), jnp.exp(m - m_new)
                acc_scr[rows, :] = a_old * acc_scr[rows, :] + a_blk * acc
                l_scr[rows, :] = a_old * l_scr[rows, :] + a_blk * den
                m_scr[rows, :] = m_new
            return carry

        lax.fori_loop(0, dil * nb, block, 0, unroll=A_UNROLL)
    o_ref[0] = (acc_scr[...] / l_scr[...]).astype(o_ref.dtype)


def a_attention_prompt(z, batch, seq):
    nc = z.shape[1]
    d = nc // (3 * len(A_PATTERNS))
    n_heads = d // HEAD_DIM
    assert all(win // dil == A_BLOCK and seq % (dil * A_BLOCK) == 0 for win, dil in A_PATTERNS)
    zv = z.reshape(batch, seq, nc)
    in_specs = [pl.BlockSpec((1, seq, HEAD_DIM), functools.partial(lambda b, h, c: (b, 0, c * n_heads + h), c=c))
                for c in range(3 * len(A_PATTERNS))]
    out = pl.pallas_call(
        functools.partial(_a_prompt_kernel, seq=seq),
        out_shape=jax.ShapeDtypeStruct((batch, seq, d), BF16),
        grid=(batch, n_heads),
        in_specs=in_specs,
        out_specs=pl.BlockSpec((1, seq, HEAD_DIM), lambda b, h: (b, 0, h)),
        scratch_shapes=[pltpu.VMEM((seq, HEAD_DIM), F32)] * 3,
        compiler_params=_cparams(2),
        name="a_attn_prompt",
    )(*([zv] * (3 * len(A_PATTERNS))))
    return out.reshape(batch * seq, d)


def _a_sample_kernel(z_ref, *refs, n_heads, d):
    cache_refs, o_ref = refs[:-1], refs[-1]
    scale = HEAD_DIM ** -0.5
    for h in range(n_heads):
        ms, dens, accs = [], [], []
        for g in range(len(A_PATTERNS)):
            kv_ref = cache_refs[g]
            c0 = g * 3 * d + h * HEAD_DIM
            q = z_ref[0, :, c0:c0 + HEAD_DIM] * scale
            k_new = z_ref[0, :, c0 + d:c0 + d + HEAD_DIM]
            v_new = z_ref[0, :, c0 + 2 * d:c0 + 2 * d + HEAD_DIM]
            q8 = jnp.broadcast_to(q, (8, HEAD_DIM))
            s = _dot_nt(q8.astype(BF16), kv_ref[:, 0, h, :].astype(BF16))
            s_new = jnp.sum(q8 * k_new, axis=1, keepdims=True)
            m = jnp.maximum(jnp.max(s, axis=1, keepdims=True), s_new)
            p = jnp.exp(s - m)
            p_new = jnp.exp(s_new - m)
            dens.append(jnp.sum(p, axis=1, keepdims=True) + p_new)
            accs.append(_dot(p.astype(BF16), kv_ref[:, 1, h, :].astype(BF16)) + p_new * v_new)
            ms.append(m)
        mt = jnp.maximum(jnp.maximum(ms[0], ms[1]), ms[2])
        es = [jnp.exp(mm - mt) for mm in ms]
        tot = es[0] * dens[0] + es[1] * dens[1] + es[2] * dens[2]
        num = es[0] * accs[0] + es[1] * accs[1] + es[2] * accs[2]
        o_ref[0, :, h * HEAD_DIM:(h + 1) * HEAD_DIM] = num / tot


def a_attention_sample(z, caches, li, n_batch):
    d = z.shape[1] // 9
    n_heads = d // HEAD_DIM
    in_specs = [pl.BlockSpec((1, 1, z.shape[1]), lambda b: (b, 0, 0))]
    args = [z[:, None, :]]
    for (win, dil), cache in zip(A_PATTERNS, caches):
        lb = cache.shape[2]
        assert lb == win and (win // dil) == A_BLOCK
        in_specs.append(pl.BlockSpec((None, None, A_BLOCK, None, 2, n_heads, HEAD_DIM),
                                     lambda b: (li, b, 0, 0, 0, 0, 0)))
        args.append(cache.reshape(cache.shape[0], n_batch, lb // dil, dil, 2, n_heads, HEAD_DIM))
    out = pl.pallas_call(
        functools.partial(_a_sample_kernel, n_heads=d // HEAD_DIM, d=d),
        out_shape=jax.ShapeDtypeStruct((n_batch, 8, d), F32),
        grid=(n_batch,),
        in_specs=in_specs,
        out_specs=pl.BlockSpec((1, 8, d), lambda b: (b, 0, 0)),
        compiler_params=_cparams(1),
        name="a_attn_sample",
    )(*args)
    return out[:, 0]


def _b_in_kernel(x_ref, xs_ref, g_ref, wb_ref, wc_ref, wx_ref, bg_ref, u_ref, bgs_ref, us_ref, h_scr):
    tm = x_ref.shape[0]

    @pl.when(pl.program_id(1) == 0)
    def _():
        h_scr[0:tm, :] = _rms(x_ref[...], g_ref[...]).astype(BF16)
        h_scr[tm:, :] = _rms(xs_ref[...], g_ref[...]).astype(BF16)

    h = h_scr[...]
    bg = _dot(h, wb_ref[...].astype(BF16))
    u = _dot(h, wc_ref[...].astype(BF16)) * _dot(h, wx_ref[...].astype(BF16))
    bg_ref[...] = bg[0:tm].astype(bg_ref.dtype)
    u_ref[...] = u[0:tm]
    bgs_ref[...] = bg[tm:]
    us_ref[...] = u[tm:]


def b_in(x, xs, gain, gi, w, li, *, tn, name):
    m, k = x.shape
    ms = xs.shape[0]
    d = w.shape[-1] // 3
    tm = _row_tile(m)
    nt = d // tn
    out_spec = pl.BlockSpec((tm, tn), lambda i, j: (i, j))
    bg, u, bg_s, u_s = pl.pallas_call(
        _b_in_kernel,
        out_shape=[jax.ShapeDtypeStruct((m, d), BF16), jax.ShapeDtypeStruct((m, d), F32)]
        + [_dec_shape(m // tm, ms, d, F32)] * 2,
        grid=(m // tm, nt),
        in_specs=[
            pl.BlockSpec((tm, k), lambda i, j: (i, 0)),
            pl.BlockSpec((ms, k), lambda i, j: (0, 0)),
            pl.BlockSpec((None, 1, k), lambda i, j: (gi, 0, 0)),
            pl.BlockSpec((None, k, tn), lambda i, j: (li, 0, j)),
            pl.BlockSpec((None, k, tn), lambda i, j: (li, 0, j + nt)),
            pl.BlockSpec((None, k, tn), lambda i, j: (li, 0, j + 2 * nt)),
        ],
        out_specs=[out_spec, out_spec, _dec_spec(ms, tn), _dec_spec(ms, tn)],
        scratch_shapes=[pltpu.VMEM((tm + ms, k), BF16)],
        compiler_params=_cparams(2),
        name=name,
    )(x, xs, gain.reshape(gain.shape[0], 1, k), w, w, w)
    return bg, u, bg_s[0], u_s[0]


def _b_out_prompt_kernel(u_ref, up_ref, bg_ref, cw_ref, w_ref, r_ref, o_ref, ext_scr, a_scr, *, tiles_per_seq):
    i = pl.program_id(0)
    tm = u_ref.shape[0]

    @pl.when(pl.program_id(1) == 0)
    def _():
        ext_scr[0:8, :] = jnp.where(i % tiles_per_seq == 0, 0.0, up_ref[...])
        ext_scr[8:, :] = u_ref[...]
        y = (cw_ref[0:1, :] * ext_scr[pl.ds(6, tm), :] + cw_ref[1:2, :] * ext_scr[pl.ds(7, tm), :]
             + cw_ref[2:3, :] * ext_scr[pl.ds(8, tm), :])
        a_scr[...] = (bg_ref[...].astype(F32) * y).astype(BF16)

    o_ref[...] = r_ref[...] + _dot(a_scr[...], w_ref[...].astype(BF16))


def b_out_prompt(u, bg, conv_w, w, li, res, seq, *, tn, name):
    m, d = u.shape
    tm = _row_tile(m)
    assert seq % tm == 0
    return pl.pallas_call(
        functools.partial(_b_out_prompt_kernel, tiles_per_seq=seq // tm),
        out_shape=jax.ShapeDtypeStruct((m, d), F32),
        grid=(m // tm, d // tn),
        in_specs=[
            pl.BlockSpec((tm, d), lambda i, j: (i, 0)),
            pl.BlockSpec((8, d), lambda i, j: (jnp.maximum(i * (tm // 8) - 1, 0), 0)),
            pl.BlockSpec((tm, d), lambda i, j: (i, 0)),
            pl.BlockSpec((None, CONV_W, d), lambda i, j: (li, 0, 0)),
            pl.BlockSpec((None, d, tn), lambda i, j: (li, 0, j)),
            pl.BlockSpec((tm, tn), lambda i, j: (i, j)),
        ],
        out_specs=pl.BlockSpec((tm, tn), lambda i, j: (i, j)),
        scratch_shapes=[pltpu.VMEM((tm + 8, d), F32), pltpu.VMEM((tm, d), BF16)],
        compiler_params=_cparams(2),
        name=name,
    )(u, u, bg, conv_w, w, res)


def _b_out_sample_kernel(u_ref, um1_ref, um2_ref, bg_ref, cw_ref, w_ref, r_ref, o_ref):
    y = cw_ref[0:1, :] * um2_ref[...] + cw_ref[1:2, :] * um1_ref[...] + cw_ref[2:3, :] * u_ref[...]
    o_ref[...] = r_ref[...] + _dot((bg_ref[...] * y).astype(BF16), w_ref[...].astype(BF16))


def b_out_sample(u, um1, um2, bg, conv_w, w, li, res, *, tn, name):
    m, d = u.shape
    full = pl.BlockSpec((m, d), lambda j: (0, 0))
    return pl.pallas_call(
        _b_out_sample_kernel,
        out_shape=jax.ShapeDtypeStruct((m, d), F32),
        grid=(d // tn,),
        in_specs=[full, full, full, full,
                  pl.BlockSpec((None, CONV_W, d), lambda j: (li, 0, 0)),
                  pl.BlockSpec((None, d, tn), lambda j: (li, 0, j)),
                  pl.BlockSpec((m, tn), lambda j: (0, j))],
        out_specs=pl.BlockSpec((m, tn), lambda j: (0, j)),
        compiler_params=_cparams(1),
        name=name,
    )(u, um1, um2, bg, conv_w, w, res)


def _d_prompt_kernel(x_ref, xp_ref, g_ref, w_ref, sc_ref, o_ref, ht_ref, ext_scr, *, tiles_per_seq):
    i = pl.program_id(0)
    j = pl.program_id(1)
    tm = x_ref.shape[0]
    halo = D_HIST + 1
    gw = w_ref.shape[0]

    @pl.when(j == 0)
    def _():
        ext_scr[0:halo, :] = jnp.where(i % tiles_per_seq == 0, 0.0, _rms(xp_ref[...], g_ref[...]))
        ext_scr[halo:, :] = _rms(x_ref[...], g_ref[...])
        ht_ref[0] = ext_scr[pl.ds(tm, halo), :]

    pos = (i % tiles_per_seq) * tm + lax.broadcasted_iota(I32, (tm, 1), 0)
    for g, win in enumerate(D_POOLS):
        @pl.when(j == g)
        def _(g=g, win=win):
            cols = slice(g * gw, (g + 1) * gw)
            tot = ext_scr[pl.ds(halo, tm), cols]
            h = tot
            for back in range(1, win):
                tot = tot + ext_scr[pl.ds(halo - back, tm), cols]
            count = jnp.minimum(pos + 1, win).astype(F32)
            pooled = tot / count - h
            o_ref[...] = x_ref[:, cols] + _dot(pooled.astype(BF16), w_ref[...].astype(BF16)) * sc_ref[...]


def d_mixer_prompt(x, gain, li_norm, w_group, scale, li, seq, *, name):
    m, d = x.shape
    n_groups, gw = w_group.shape[1], w_group.shape[2]
    tm = 512
    halo = D_HIST + 1
    assert seq % tm == 0 and n_groups == len(D_POOLS)
    tps = seq // tm
    return pl.pallas_call(
        functools.partial(_d_prompt_kernel, tiles_per_seq=tps),
        out_shape=[jax.ShapeDtypeStruct((m, d), F32), jax.ShapeDtypeStruct((m // seq, halo, d), F32)],
        grid=(m // tm, n_groups),
        in_specs=[
            pl.BlockSpec((tm, d), lambda i, j: (i, 0)),
            pl.BlockSpec((halo, d), lambda i, j: (jnp.maximum(i * (tm // halo) - 1, 0), 0)),
            pl.BlockSpec((None, 1, d), lambda i, j: (li_norm, 0, 0)),
            pl.BlockSpec((None, None, gw, gw), lambda i, j: (li, j, 0, 0)),
            pl.BlockSpec((None, 1, gw), lambda i, j: (li, 0, j)),
        ],
        out_specs=[pl.BlockSpec((tm, gw), lambda i, j: (i, j)),
                   pl.BlockSpec((1, halo, d), lambda i, j: (i // tps, 0, 0))],
        scratch_shapes=[pltpu.VMEM((tm + halo, d), F32)],
        compiler_params=_cparams(2),
        name=name,
    )(x, x, gain.reshape(gain.shape[0], 1, d), w_group, scale.reshape(scale.shape[0], 1, d))


def _d_sample_kernel(x_ref, hist_ref, g_ref, w_ref, sc_ref, o_ref, h_ref):
    h = _rms(x_ref[...], g_ref[...])
    h_ref[...] = h
    gw = w_ref.shape[1]
    for g, win in enumerate(D_POOLS):
        cols = slice(g * gw, (g + 1) * gw)
        tot = h[:, cols]
        for back in range(1, win):
            tot = tot + hist_ref[D_HIST - back, :, cols]
        pooled = tot / float(win) - h[:, cols]
        o_ref[:, cols] = x_ref[:, cols] + _dot(pooled.astype(BF16), w_ref[g].astype(BF16)) * sc_ref[:, cols]


def d_mixer_sample(x, hist_t, gain, li_norm, w_group, scale, li, *, name):
    m, d = x.shape
    n_groups, gw = w_group.shape[1], w_group.shape[2]
    assert hist_t.shape[0] == D_HIST
    return pl.pallas_call(
        _d_sample_kernel,
        out_shape=[jax.ShapeDtypeStruct((m, d), F32)] * 2,
        grid=(1,),
        in_specs=[
            pl.BlockSpec((m, d), lambda i: (0, 0)),
            pl.BlockSpec(hist_t.shape, lambda i: (0, 0, 0)),
            pl.BlockSpec((None, 1, d), lambda i: (li_norm, 0, 0)),
            pl.BlockSpec((None, n_groups, gw, gw), lambda i: (li, 0, 0, 0)),
            pl.BlockSpec((None, 1, d), lambda i: (li, 0, 0)),
        ],
        out_specs=[pl.BlockSpec((m, d), lambda i: (0, 0))] * 2,
        compiler_params=_cparams(1),
        name=name,
    )(x, hist_t, gain.reshape(gain.shape[0], 1, d), w_group, scale.reshape(scale.shape[0], 1, d))


def _cmp_fs_kernel(pt_ref, page_ref, w_ref, o_ref, ring_scr):
    del pt_ref
    slot = pl.program_id(1) % PAGES_PER_GROUP
    rows = page_ref.shape[1]
    for ck in range(2 * C_KV_HEADS):
        ring_scr[ck, pl.ds(pl.multiple_of(slot * rows, rows), rows), :] = page_ref[0, :, ck * HEAD_DIM:(ck + 1) * HEAD_DIM]

    @pl.when(slot == PAGES_PER_GROUP - 1)
    def _():
        n_chunks = PAGES_PER_GROUP * rows // C_CMP_STRIDE
        for ck in range(2 * C_KV_HEADS):
            c = ck // C_KV_HEADS
            acc = jnp.zeros((n_chunks, w_ref.shape[-1]), F32)
            for pp in range(C_CMP_STRIDE // 2):
                lhs = jnp.concatenate([ring_scr[ck, pl.ds(2 * pp + i, n_chunks, stride=C_CMP_STRIDE), :]
                                       for i in range(2)], axis=1)
                acc = acc + _dot(lhs.astype(BF16), w_ref[c, pp])
            o_ref[0, ck] = acc


def cmp_first_second(pool, page_ids, col_block, w_cat, n_batch, n_pages):
    rows = pool.shape[1]
    width = 2 * C_KV_HEADS * HEAD_DIM
    assert n_pages % PAGES_PER_GROUP == 0 and rows % C_CMP_STRIDE == 0
    chunks_per_group = PAGES_PER_GROUP * rows // C_CMP_STRIDE
    n_out = w_cat.shape[-1]
    return pl.pallas_call(
        _cmp_fs_kernel,
        out_shape=jax.ShapeDtypeStruct((n_batch, 2 * C_KV_HEADS, n_pages * rows // C_CMP_STRIDE, n_out), F32),
        grid_spec=pltpu.PrefetchScalarGridSpec(
            num_scalar_prefetch=1,
            grid=(n_batch, n_pages),
            in_specs=[
                pl.BlockSpec((1, rows, width), lambda b, pg, pt: (pt[b * n_pages + pg], 0, col_block)),
                pl.BlockSpec(w_cat.shape, lambda b, pg, pt: (0, 0, 0, 0)),
            ],
            out_specs=pl.BlockSpec((1, 2 * C_KV_HEADS, chunks_per_group, n_out),
                                   lambda b, pg, pt: (b, 0, pg // PAGES_PER_GROUP, 0)),
            scratch_shapes=[pltpu.VMEM((width // HEAD_DIM, PAGES_PER_GROUP * rows, HEAD_DIM), F32)],
        ),
        compiler_params=_cparams(2),
        name="c_cmp_first_second",
    )(page_ids, pool, w_cat)


def _cmp_fs_paged_kernel(pt_ref, *refs):
    del pt_ref
    page_refs, (w_ref, o_ref, acc_scr) = refs[:PAGES_PER_GROUP], refs[PAGES_PER_GROUP:]
    chunks_per_page = page_refs[0].shape[0] // C_CMP_STRIDE
    n_chunks = PAGES_PER_GROUP * chunks_per_page
    n_out = w_ref.shape[-1]

    def chunk_rows(p, c):
        return jnp.concatenate([pr[pl.ds(p, chunks_per_page, stride=C_CMP_STRIDE), c, :, :]
                                .reshape(chunks_per_page * C_KV_HEADS, HEAD_DIM) for pr in page_refs], axis=0)

    for c in range(2):
        acc = jnp.zeros((n_chunks * C_KV_HEADS, n_out), F32)
        for pp in range(C_CMP_STRIDE // 2):
            lhs = jnp.concatenate([chunk_rows(2 * pp, c), chunk_rows(2 * pp + 1, c)], axis=1)
            acc = acc + _dot(lhs.astype(BF16), w_ref[c, pp])
        for j in range(n_out // LANES):
            acc_scr[j] = acc[:, j * LANES:(j + 1) * LANES]
        for k in range(C_KV_HEADS):
            for j in range(n_out // LANES):
                o_ref[0, c * C_KV_HEADS + k, :, j * LANES:(j + 1) * LANES] = (
                    acc_scr[j, pl.ds(k, n_chunks, stride=C_KV_HEADS), :])


def cmp_first_second_paged(pool, li, page_table, w_cat):
    n_batch, n_pages = page_table.shape
    rows = pool.shape[2]
    assert n_pages % PAGES_PER_GROUP == 0 and rows % C_CMP_STRIDE == 0
    chunks_per_group = PAGES_PER_GROUP * rows // C_CMP_STRIDE
    n_out = w_cat.shape[-1]

    def page_spec(s):
        return pl.BlockSpec((None, None, rows, 2, C_KV_HEADS, HEAD_DIM),
                            lambda b, grp, pt: (li, pt[b * n_pages + grp * PAGES_PER_GROUP + s], 0, 0, 0, 0))

    return pl.pallas_call(
        _cmp_fs_paged_kernel,
        out_shape=jax.ShapeDtypeStruct((n_batch, 2 * C_KV_HEADS, n_pages * rows // C_CMP_STRIDE, n_out), F32),
        grid_spec=pltpu.PrefetchScalarGridSpec(
            num_scalar_prefetch=1,
            grid=(n_batch, n_pages // PAGES_PER_GROUP),
            in_specs=[page_spec(s) for s in range(PAGES_PER_GROUP)]
            + [pl.BlockSpec(w_cat.shape, lambda b, grp, pt: (0, 0, 0, 0))],
            out_specs=pl.BlockSpec((1, 2 * C_KV_HEADS, chunks_per_group, n_out), lambda b, grp, pt: (b, 0, grp, 0)),
            scratch_shapes=[pltpu.VMEM((n_out // LANES, chunks_per_group * C_KV_HEADS, LANES), F32)],
        ),
        compiler_params=_cparams(2),
        name="c_cmp_first_second_paged",
    )(page_table.reshape(-1), *([pool] * PAGES_PER_GROUP), w_cat)


def _cmp_pe_kernel(pe_ref, w1_ref, o_ref):
    o_ref[0] = _dot(pe_ref[0].astype(BF16), w1_ref[0].astype(BF16))


def cmp_pe_term(pe8, w1_flat):
    return pl.pallas_call(
        _cmp_pe_kernel,
        out_shape=jax.ShapeDtypeStruct((pe8.shape[0], pe8.shape[1], w1_flat.shape[-1]), F32),
        grid=(pe8.shape[0],),
        in_specs=[pl.BlockSpec((1,) + pe8.shape[1:], lambda c: (c, 0, 0)),
                  pl.BlockSpec((1,) + w1_flat.shape[1:], lambda c: (c, 0, 0))],
        out_specs=pl.BlockSpec((1, pe8.shape[1], w1_flat.shape[-1]), lambda c: (c, 0, 0)),
        compiler_params=_cparams(1),
        name="c_cmp_pe_term",
    )(pe8, w1_flat)


def _cmp_mlp_kernel(fs_ref, pe_ref, w2_ref, kg_ref, o_ref):
    ck = pl.program_id(1)
    nch = fs_ref.shape[2]
    hid_w = w2_ref.shape[1]
    pe_term = pe_ref[0, 0:1, :]
    first = fs_ref[0, 0, :, 0:hid_w]
    second = fs_ref[0, 0, :, hid_w:2 * hid_w]
    nxt = pltpu.roll(second, nch - 1, 0)
    last = lax.broadcasted_iota(I32, (nch, 1), 0) == nch - 1
    pre = first + jnp.where(last, 0.0, nxt) + pe_term
    hid = pre * jax.nn.sigmoid(pre)
    out = _dot(hid.astype(BF16), w2_ref[0].astype(BF16))

    @pl.when(ck < C_KV_HEADS)
    def _():
        o_ref[0, 0] = _rms(out, kg_ref[...])

    @pl.when(ck >= C_KV_HEADS)
    def _():
        o_ref[0, 0] = out


def cmp_mlp(fs, pe_term, w2, k_gain0):
    n_batch, n_ck, nch, _ = fs.shape
    return pl.pallas_call(
        _cmp_mlp_kernel,
        out_shape=jax.ShapeDtypeStruct((n_batch, n_ck, nch, HEAD_DIM), F32),
        grid=(n_batch, n_ck),
        in_specs=[
            pl.BlockSpec((1, 1, nch, fs.shape[-1]), lambda b, ck: (b, ck, 0, 0)),
            pl.BlockSpec((1,) + pe_term.shape[1:], lambda b, ck: (ck // C_KV_HEADS, 0, 0)),
            pl.BlockSpec((1,) + w2.shape[1:], lambda b, ck: (ck // C_KV_HEADS, 0, 0)),
            pl.BlockSpec((1, HEAD_DIM), lambda b, ck: (0, 0)),
        ],
        out_specs=pl.BlockSpec((1, 1, nch, HEAD_DIM), lambda b, ck: (b, ck, 0, 0)),
        compiler_params=_cparams(2),
        name="c_cmp_mlp",
    )(fs, pe_term, w2, k_gain0)


def _masked_softmax_rows(s, ok):
    s = jnp.where(ok, s, NEG)
    m = jnp.max(s, axis=1, keepdims=True)
    e = jnp.where(ok, jnp.exp(s - m), 0.0)
    return e, jnp.maximum(jnp.sum(e, axis=1, keepdims=True), 1e-30)


def _nsa_prompt_kernel(q_ref, gate_ref, cos_ref, sin_ref, kc_ref, vc_ref, ks_ref, vs_ref, kw_ref, vw_ref,
                       ov_ref, ex_ref, o_ref, s_scr, mx_scr, ext_scr, *, nsel):
    qb = q_ref.shape[0]
    nch = kc_ref.shape[2]
    tk = s_scr.shape[2]
    t0 = pl.program_id(2) * qb

    def stacked_bias(ok):
        return jnp.concatenate([jnp.where(ok, 0.0, NEG)] * C_GROUP, axis=0)

    def t_rows(n):
        return t0 + lax.broadcasted_iota(I32, (qb, n), 0)

    qscale = HEAD_DIM ** -0.5 * LOG2_E
    qs = [q_ref[:, g * HEAD_DIM:(g + 1) * HEAD_DIM] for g in range(C_GROUP)]
    qn4 = jnp.concatenate([(q * qscale).astype(BF16) for q in qs], axis=0)
    qr4 = jnp.concatenate([(_rope(q, cos_ref[...], sin_ref[...]) * qscale).astype(BF16) for q in qs], axis=0)

    wk = C_WINDOW + qb
    start = pl.multiple_of(jnp.maximum(t0 - C_WINDOW, 0), qb)
    dist = t_rows(wk) - (start + lax.broadcasted_iota(I32, (qb, wk), 1))
    s = _dot_nt(qr4, kw_ref[0, pl.ds(start, wk), :].astype(BF16)) + stacked_bias((dist >= 0) & (dist < C_WINDOW))
    p = jnp.exp2(s - jnp.max(s, axis=1, keepdims=True)).astype(BF16)
    ext = _dot(p, _with_ones(vw_ref[0, pl.ds(start, wk), :]))
    o_win4 = ext[:, :HEAD_DIM] / ext[:, HEAD_DIM:]

    cmp_bias = stacked_bias(lax.broadcasted_iota(I32, (qb, nch), 1) * C_CMP_STRIDE + (C_CMP_LEN - 1) <= t_rows(nch))
    s = _dot_nt(qn4, kc_ref[0, 0].astype(BF16)) + cmp_bias
    e = jnp.where(cmp_bias < 0.0, 0.0, jnp.exp2(s - jnp.max(s, axis=1, keepdims=True)))
    ext = _dot(e.astype(BF16), _with_ones(vc_ref[0, 0]))
    den = jnp.maximum(ext[:, HEAD_DIM:], 1e-30)
    o_cmp4 = ext[:, :HEAD_DIM] / den
    p = e / den
    p_sum = p[0:qb]
    for g in range(1, C_GROUP):
        p_sum = p_sum + p[g * qb:(g + 1) * qb]

    p_hi = p_sum.astype(BF16)
    p_lo = (p_sum - p_hi.astype(F32)).astype(BF16)
    nr = -(-nsel // 8) * 8
    imp = (_dot_nt(ov_ref[...], p_hi) + _dot_nt(ov_ref[...], p_lo))[:nr]
    blk = lax.broadcasted_iota(I32, (nr, qb), 0)
    cur = (t0 + lax.broadcasted_iota(I32, (nr, qb), 1)) // C_SEL_BLOCK
    causal = (blk <= cur) & (blk < nsel)
    forced = (blk == 0) | (blk == cur) | (blk == cur - 1)
    imp = jnp.where(causal, jnp.where(forced, C_FORCE_SCORE, imp), NEG)
    rank = jnp.zeros((nr, qb), I32)
    for jp in range(nsel):
        c = jnp.broadcast_to(imp[jp:jp + 1, :], (nr, qb))
        beats = (c > imp) | ((c == imp) & (blk > jp))
        rank = rank + beats.astype(I32)
    sel_t = jnp.where((rank < C_SEL_TOPN) & causal, 1.0, 0.0)
    sel_t = jnp.concatenate([sel_t, jnp.zeros((LANES - nr, qb), F32)], axis=0)
    sel = sel_t.T.astype(BF16)

    n_tiles = (t0 + qb + tk - 1) // tk
    mx_scr[...] = jnp.full(mx_scr.shape, NEG, F32)

    def score_tile(kt, carry):
        k0 = pl.multiple_of(kt * tk, tk)
        sel_keys = _dot(sel, ex_ref[kt])
        ok = (sel_keys > 0.5) & (k0 + lax.broadcasted_iota(I32, (qb, tk), 1) <= t_rows(tk))
        s = _dot_nt(qr4, ks_ref[0, pl.ds(k0, tk), :].astype(BF16)) + stacked_bias(ok)
        s_scr[kt] = s
        mx = mx_scr[...]
        for c in range(tk // LANES):
            mx = jnp.maximum(mx, s[:, c * LANES:(c + 1) * LANES])
        mx_scr[...] = mx
        return carry

    lax.fori_loop(0, n_tiles, score_tile, 0)
    m_slc = jnp.max(mx_scr[...], axis=1, keepdims=True)

    ext_scr[...] = jnp.zeros(ext_scr.shape, F32)

    def value_tile(kt, carry):
        k0 = pl.multiple_of(kt * tk, tk)
        p = jnp.exp2(s_scr[kt] - m_slc).astype(BF16)
        ext_scr[...] += _dot(p, _with_ones(vs_ref[0, pl.ds(k0, tk), :]))
        return carry

    lax.fori_loop(0, n_tiles, value_tile, 0)
    o_slc4 = ext_scr[:, :HEAD_DIM] / ext_scr[:, HEAD_DIM:]

    for g in range(C_GROUP):
        rows = slice(g * qb, (g + 1) * qb)
        o = (gate_ref[:, 3 * g:3 * g + 1] * o_cmp4[rows] + gate_ref[:, 3 * g + 1:3 * g + 2] * o_slc4[rows]
             + gate_ref[:, 3 * g + 2:3 * g + 3] * o_win4[rows])
        o_ref[:, g * HEAD_DIM:(g + 1) * HEAD_DIM] = o.astype(o_ref.dtype)


def nsa_prompt(z, gates, cos, sin, kcv, overlap_t, expand, batch, seq):
    m, nc = z.shape
    d = C_KV_HEADS * C_GROUP * HEAD_DIM
    qb = C_QBLOCK
    nq = seq // qb
    nsel = -(-seq // C_SEL_BLOCK)
    tk = C_KEY_TILE
    assert nsel <= LANES and seq % qb == 0 and seq >= C_WINDOW + qb and seq % tk == 0
    assert expand.shape == (seq // tk, LANES, tk)
    nch = kcv.shape[2]
    assert nch == HEAD_DIM and qb == LANES and overlap_t.shape == (LANES, nch)
    zv = z.reshape(batch, seq, nc)
    cb = d // HEAD_DIM
    kvw = C_KV_HEADS

    def kv_spec(branch, kv):
        off = cb + (branch * 2 + kv) * kvw
        return pl.BlockSpec((1, seq, HEAD_DIM), lambda b, k, q: (b, 0, off + k))

    return pl.pallas_call(
        functools.partial(_nsa_prompt_kernel, nsel=nsel),
        out_shape=jax.ShapeDtypeStruct((m, d), BF16),
        scratch_shapes=[pltpu.VMEM((seq // tk, C_GROUP * qb, tk), F32),
                        pltpu.VMEM((C_GROUP * qb, LANES), F32),
                        pltpu.VMEM((C_GROUP * qb, 2 * HEAD_DIM), F32)],
        grid=(batch, C_KV_HEADS, nq),
        in_specs=[
            pl.BlockSpec((qb, C_GROUP * HEAD_DIM), lambda b, k, q: (b * nq + q, k)),
            pl.BlockSpec((qb, LANES), lambda b, k, q: (b * nq + q, k)),
            pl.BlockSpec((qb, HEAD_DIM), lambda b, k, q: (q, 0)),
            pl.BlockSpec((qb, HEAD_DIM), lambda b, k, q: (q, 0)),
            pl.BlockSpec((1, 1, nch, HEAD_DIM), lambda b, k, q: (b, k, 0, 0)),
            pl.BlockSpec((1, 1, nch, HEAD_DIM), lambda b, k, q: (b, C_KV_HEADS + k, 0, 0)),
            kv_spec(1, 0), kv_spec(1, 1), kv_spec(2, 0), kv_spec(2, 1),
            pl.BlockSpec(overlap_t.shape, lambda b, k, q: (0, 0)),
            pl.BlockSpec(expand.shape, lambda b, k, q: (0, 0, 0)),
        ],
        out_specs=pl.BlockSpec((qb, C_GROUP * HEAD_DIM), lambda b, k, q: (b * nq + q, k)),
        compiler_params=_cparams(3),
        name="c_nsa_prompt",
    )(z, gates, cos, sin, kcv, kcv, zv, zv, zv, zv, overlap_t, expand)


def _nsa_sample_cmp_kernel(q_ref, kc_ref, vc_ref, ov_ref, ocmp_ref, idx_ref, *, t, nsel):
    nch = kc_ref.shape[2]
    scale = HEAD_DIM ** -0.5
    cmp_ok = lax.broadcasted_iota(I32, (8, nch), 1) * C_CMP_STRIDE + (C_CMP_LEN - 1) <= t
    row = lax.broadcasted_iota(I32, (8, nch), 0)
    p_sum = jnp.zeros((8, nch), F32)
    for k in range(C_KV_HEADS):
        q = (q_ref[0, k] * scale).astype(BF16)
        e, den = _masked_softmax_rows(_dot_nt(q, kc_ref[0, k].astype(BF16)), cmp_ok)
        p = e / den
        ocmp_ref[0, k] = _dot(p.astype(BF16), vc_ref[0, k].astype(BF16))
        p_k = jnp.sum(jnp.where(row < C_GROUP, p, 0.0), axis=0, keepdims=True)
        p_sum = jnp.where(row == k, jnp.broadcast_to(p_k, (8, nch)), p_sum)
    p_hi = p_sum.astype(BF16)
    p_lo = (p_sum - p_hi.astype(F32)).astype(BF16)
    imp = _dot(p_hi, ov_ref[...]) + _dot(p_lo, ov_ref[...])
    width = imp.shape[1]
    lane = lax.broadcasted_iota(I32, (8, width), 1)
    cur = t // C_SEL_BLOCK
    causal = (lane <= cur) & (lane < nsel)
    forced = (lane == 0) | (lane == cur) | (lane == cur - 1)
    work = jnp.where(causal, jnp.where(forced, C_FORCE_SCORE, imp), NEG)
    lane_f = lane.astype(F32)
    out_lane = lax.broadcasted_iota(I32, (8, LANES), 1)
    idx = jnp.full((8, LANES), -1, I32)
    for r in range(C_SEL_TOPN):
        best = jnp.max(work, axis=1, keepdims=True)
        pick = jnp.min(jnp.where(work == best, lane_f, float(width)), axis=1, keepdims=True)
        found = jnp.where(best > 0.5 * NEG, pick, -1.0).astype(I32)
        idx = jnp.where(out_lane == r, found, idx)
        work = jnp.where(lane_f == pick, NEG, work)
    idx_ref[0] = idx


def nsa_sample_cmp(q8, kcv, overlap, t, nsel):
    n_batch = q8.shape[0]
    nch = kcv.shape[2]
    assert C_KV_HEADS <= 8
    blk = pl.BlockSpec((1, C_KV_HEADS, 8, HEAD_DIM), lambda b: (b, 0, 0, 0))
    return pl.pallas_call(
        functools.partial(_nsa_sample_cmp_kernel, t=t, nsel=nsel),
        out_shape=[jax.ShapeDtypeStruct((n_batch, C_KV_HEADS, 8, HEAD_DIM), F32),
                   jax.ShapeDtypeStruct((n_batch, 8, LANES), I32)],
        grid=(n_batch,),
        in_specs=[
            blk,
            pl.BlockSpec((1, C_KV_HEADS, nch, HEAD_DIM), lambda b: (b, 0, 0, 0)),
            pl.BlockSpec((1, C_KV_HEADS, nch, HEAD_DIM), lambda b: (b, 1, 0, 0)),
            pl.BlockSpec(overlap.shape, lambda b: (0, 0)),
        ],
        out_specs=[blk, pl.BlockSpec((1, 8, LANES), lambda b: (b, 0, 0))],
        compiler_params=_cparams(1),
        name="c_nsa_sample_cmp",
    )(q8, kcv, kcv, overlap)


def _nsa_sample_kernel(idx_ref, pt_ref, q_ref, cos_ref, sin_ref, *refs, t, n_past_blocks):
    del pt_ref
    slc_refs, (new_ref, win_ref, ocmp_ref, gate_ref, o_ref) = refs[:C_SEL_TOPN], refs[C_SEL_TOPN:]
    b, k = pl.program_id(0), pl.program_id(1)

    def head_rows(ref, kv):
        return ref[:, kv, pl.ds(k, 1), :][:, 0, :].astype(BF16)

    base = (b * C_KV_HEADS + k) * C_SEL_TOPN
    scale = HEAD_DIM ** -0.5
    qr_f = _rope(q_ref[0, 0], cos_ref[...], sin_ref[...]) * scale
    qr = qr_f.astype(BF16)

    nk = C_SEL_TOPN * C_SEL_BLOCK
    s = _dot_nt(qr, jnp.concatenate([head_rows(r, 0) for r in slc_refs], axis=0))
    lane = lax.broadcasted_iota(I32, (8, nk), 1)
    blk = jnp.full((8, nk), -1, I32)
    has_new = False
    for n in range(C_SEL_TOPN):
        blk_n = idx_ref[base + n]
        blk = jnp.where((lane >= n * C_SEL_BLOCK) & (lane < (n + 1) * C_SEL_BLOCK), blk_n, blk)
        has_new = has_new | (blk_n == n_past_blocks)
    ok = (blk >= 0) & (blk < n_past_blocks) & (blk * C_SEL_BLOCK + (lane & (C_SEL_BLOCK - 1)) <= t)
    k_new, v_new = new_ref[0, 0, 0:1, :], new_ref[0, 0, 1:2, :]
    s = jnp.where(ok, s, NEG)
    s_new = jnp.where(has_new, jnp.sum(qr_f * k_new, axis=1, keepdims=True), NEG)
    m = jnp.maximum(jnp.max(s, axis=1, keepdims=True), s_new)
    p = jnp.where(ok, jnp.exp(s - m), 0.0)
    p_new = jnp.where(has_new, jnp.exp(s_new - m), 0.0)
    den = jnp.maximum(jnp.sum(p, axis=1, keepdims=True) + p_new, 1e-30)
    vs = jnp.concatenate([head_rows(r, 1) for r in slc_refs], axis=0)
    o_slc = (_dot(p.astype(BF16), vs) + p_new * v_new) / den

    lw = win_ref.shape[0]
    kw_new, vw_new = new_ref[0, 0, 2:3, :], new_ref[0, 0, 3:4, :]
    dist = lw - lax.broadcasted_iota(I32, (8, lw), 1)
    w_ok = (dist >= 0) & (dist < C_WINDOW)
    s = jnp.where(w_ok, _dot_nt(qr, head_rows(win_ref, 0)), NEG)
    s_new = jnp.sum(qr_f * kw_new, axis=1, keepdims=True)
    m = jnp.maximum(jnp.max(s, axis=1, keepdims=True), s_new)
    p = jnp.where(w_ok, jnp.exp(s - m), 0.0)
    p_new = jnp.exp(s_new - m)
    den = jnp.sum(p, axis=1, keepdims=True) + p_new
    o_win = (_dot(p.astype(BF16), head_rows(win_ref, 1)) + p_new * vw_new) / den

    gate = gate_ref[0, 0]
    o_ref[0, 0] = gate[:, 0:1] * ocmp_ref[0, 0] + gate[:, 1:2] * o_slc + gate[:, 2:3] * o_win


def nsa_sample(top_idx, page_table, q8, cos, sin, pool_slc, win_buf, li, new_kv, ocmp, gate8, t, n_past_blocks):
    n_batch = q8.shape[0]
    n_pages = page_table.shape[1]
    n_layers, n_pool, page_rows = pool_slc.shape[:3]
    blocks_per_page = page_rows // C_SEL_BLOCK
    kv_dims = (2, C_KV_HEADS, HEAD_DIM)
    pool = pool_slc.reshape((n_layers, n_pool, blocks_per_page, C_SEL_BLOCK) + kv_dims)
    lw = win_buf.shape[2]

    def slc_spec(n):
        def index(b, k, idx, pt):
            blk = jnp.clip(idx[(b * C_KV_HEADS + k) * C_SEL_TOPN + n], 0, n_past_blocks - 1)
            return (li, pt[b * n_pages + blk // blocks_per_page], blk % blocks_per_page, 0, 0, 0, 0)
        return pl.BlockSpec((None, None, None, C_SEL_BLOCK) + kv_dims, index)

    win_spec = pl.BlockSpec((None, None, lw) + kv_dims, lambda b, k, idx, pt: (li, b, 0, 0, 0, 0))
    blk8 = pl.BlockSpec((1, 1, 8, HEAD_DIM), lambda b, k, idx, pt: (b, k, 0, 0))
    tab = pl.BlockSpec((8, HEAD_DIM), lambda b, k, idx, pt: (0, 0))
    return pl.pallas_call(
        functools.partial(_nsa_sample_kernel, t=t, n_past_blocks=n_past_blocks),
        out_shape=jax.ShapeDtypeStruct((n_batch, C_KV_HEADS, 8, HEAD_DIM), F32),
        grid_spec=pltpu.PrefetchScalarGridSpec(
            num_scalar_prefetch=2,
            grid=(n_batch, C_KV_HEADS),
            in_specs=[blk8, tab, tab] + [slc_spec(n) for n in range(C_SEL_TOPN)] + [blk8, win_spec, blk8, blk8],
            out_specs=blk8,
        ),
        compiler_params=_cparams(2),
        name="c_nsa_sample",
    )(top_idx.reshape(-1), page_table.reshape(-1), q8, cos, sin, *([pool] * C_SEL_TOPN), new_kv, win_buf, ocmp, gate8)


def _shift_append_kernel(cur_ref, nxt_ref, new_ref, out_ref):
    r = cur_ref.shape[0]
    out_ref[0:r - 1] = cur_ref[1:r]
    last = pl.program_id(1) == pl.num_programs(1) - 1

    @pl.when(last)
    def _():
        out_ref[r - 1:r] = new_ref[...]

    @pl.when(jnp.logical_not(last))
    def _():
        out_ref[r - 1:r] = nxt_ref[...]


def shift_append(cache, li, new, name):
    n_batch, length = cache.shape[1:3]
    tail = cache.shape[3:]
    assert new.shape == (n_batch, 1) + tail
    r = min(length, SHIFT_ROWS)
    assert length % r == 0
    zeros = (0,) * len(tail)
    return pl.pallas_call(
        _shift_append_kernel,
        out_shape=jax.ShapeDtypeStruct(cache.shape[1:], cache.dtype),
        grid=(n_batch, length // r),
        in_specs=[
            pl.BlockSpec((None, None, r) + tail, lambda b, i: (li, b, i) + zeros),
            pl.BlockSpec((None, None, 1) + tail, lambda b, i: (li, b, jnp.minimum((i + 1) * r, length - 1)) + zeros),
            pl.BlockSpec((None, 1) + tail, lambda b, i: (b, 0) + zeros),
        ],
        out_specs=pl.BlockSpec((None, r) + tail, lambda b, i: (b, i) + zeros),
        compiler_params=_cparams(2),
        name=name,
    )(cache, cache, new)


def _rope_tables(pos):
    half = HEAD_DIM // 2
    inv = ROPE_THETA ** (-(jnp.arange(half, dtype=F32) * 2.0 / HEAD_DIM))
    ang = pos.astype(F32)[:, None] * inv[None, :]
    cos, sin = jnp.cos(ang), jnp.sin(ang)
    return jnp.concatenate([cos, cos], axis=1), jnp.concatenate([-sin, sin], axis=1)


def _overlap_matrix(nch, nsel, width):
    cmp_start = jnp.arange(nch) * C_CMP_STRIDE
    sel_start = jnp.arange(width) * C_SEL_BLOCK
    ov = ((cmp_start[:, None] < sel_start[None, :] + C_SEL_BLOCK)
          & (cmp_start[:, None] + C_CMP_LEN > sel_start[None, :])
          & (jnp.arange(width)[None, :] < nsel))
    return ov.astype(BF16)


def _c_weights(prm, li, n_heads):
    d = n_heads * HEAD_DIM
    nkv = 3 * 2 * C_KV_HEADS * HEAD_DIM
    k_gain = prm['c_k_norm'][li]
    ones = jnp.ones((C_KV_HEADS * HEAD_DIM,), F32)
    head_gain = jnp.concatenate([
        jnp.tile(prm['c_q_norm'][li], n_heads), ones, ones,
        jnp.tile(k_gain[1], C_KV_HEADS), ones, jnp.tile(k_gain[2], C_KV_HEADS), ones])[None, :]
    wg = prm['w_c_in'][li][:, d + nkv:].reshape(-1, C_KV_HEADS, C_GROUP * 3)
    wg = jnp.pad(wg, ((0, 0), (0, 0), (0, LANES - C_GROUP * 3))).reshape(1, -1, C_KV_HEADS * LANES)
    w1 = prm['c_cmp_w1'][li]
    w_cat = jnp.concatenate([w1[:, :C_CMP_STRIDE], w1[:, C_CMP_STRIDE:]], axis=-1).astype(BF16)
    w_cat = w_cat.reshape(2, C_CMP_STRIDE // 2, 2 * HEAD_DIM, w_cat.shape[-1])
    pe8 = jnp.pad(prm['c_cmp_pe'][li].reshape(2, 1, -1), ((0, 0), (0, 7), (0, 0)))
    w1_flat = w1.reshape(2, C_CMP_LEN * HEAD_DIM, -1)
    return head_gain, wg, w_cat, pe8, w1_flat, prm['c_cmp_w2'][li], k_gain[0][None, :]


TN = 512
TN_WIDE_K = 256
TM_OUT_PROJ = 2048
C_MODES = (NORM,) * 4 + (RAW, RAW, NORM_ROPE, RAW, NORM_ROPE, RAW)
A_MODES = ((NORM_ROPE,) * 8 + (RAW,) * 4) * len(A_PATTERNS)


def _pad_rows(a, rows):
    return jnp.pad(a, ((0, rows - a.shape[0]),) + ((0, 0),) * (a.ndim - 1))


def _a_head_gain(prm, li, n_heads):
    ones = jnp.ones((n_heads * HEAD_DIM,), F32)
    parts = []
    for g in range(len(A_PATTERNS)):
        parts += [jnp.tile(prm['a_q_norm'][li, g], n_heads), jnp.tile(prm['a_k_norm'][li, g], n_heads), ones]
    return jnp.concatenate(parts)[None, :]


def _channel_mixer(x, p, prm, i, tag):
    act = yield ('swiglu_in', x, dict(gain=prm['ffn_norm'], w=prm['w_ffn_in'], li=i, tn=TN, name="ffn_in"))
    x = linear_residual(act, prm['w_ffn_out'], i, x, tn=TN_WIDE_K, name=f"ffn_out_{tag}")
    x = yield ('ple', x, dict(gain=prm['ple_norm'], wp=prm['w_ple_proj'], wg=prm['w_ple_gate'], li=i, tn=TN,
                              name="ple"), p)
    return x


def _serve(req_p, req_s):
    kind, x, kw = req_p[:3]
    xs = req_s[1]
    assert kind == req_s[0]
    if kind == 'norm_linear':
        return norm_linear(x, xs, rope=req_p[3], rope_s=req_s[3], **kw)
    if kind == 'swiglu_in':
        return swiglu_in(x, xs, **kw)
    if kind == 'ple':
        return ple(x, xs, p=req_p[3], ps=req_s[3], **kw)
    bg, u, bg_s, u_s = b_in(x, xs, **kw)
    return (bg, u), (bg_s, u_s)


def _run_both(walk_p, walk_s):
    req_p, req_s = next(walk_p), next(walk_s)
    while True:
        out_p, out_s = _serve(req_p, req_s)
        try:
            req_p = walk_p.send(out_p)
        except StopIteration as done_p:
            try:
                walk_s.send(out_s)
            except StopIteration as done_s:
                return done_p.value, done_s.value
            raise AssertionError("decode walk outlived the prompt walk")
        req_s = walk_s.send(out_s)


def _run_prompt(x3, p, prm):
    batch, seq, d = x3.shape
    n_heads = d // HEAD_DIM
    x = x3.reshape(batch * seq, d)
    cos, sin = _rope_tables(jnp.arange(seq))
    new = {}
    depth = prm['attn_norm'].shape[0]
    for i in range(depth):
        kind, li = i % 4, i // 4
        if kind == 0:
            z = yield ('norm_linear', x, dict(gain=prm['attn_norm'], gi=i, w=prm['w_a_qkv'], wi=li, col0=0,
                                              modes=A_MODES, head_gain=_a_head_gain(prm, li, n_heads), tn=TN,
                                              name="a_qkv"), (cos, sin))
            x = linear_residual(a_attention_prompt(z, batch, seq), prm['w_a_out'], li, x, tn=TN, tm=TM_OUT_PROJ,
                                name="a_out_prompt")
            z3 = z.reshape(batch, seq, -1)
            for g, (win, dil) in enumerate(A_PATTERNS):
                kv = z3[:, seq - min(win, seq):, (3 * g + 1) * d:(3 * g + 3) * d]
                new.setdefault(f'a_w{g + 1}', []).append(kv.reshape(batch, -1, 2, n_heads, HEAD_DIM))
        elif kind == 1:
            bg, u = yield ('b_in', x, dict(gain=prm['attn_norm'], gi=i, w=prm['w_b_in'], li=li, tn=TN_WIDE_K,
                                           name="b_in"))
            x = b_out_prompt(u, bg, prm['b_conv'], prm['w_b_out'], li, x, seq, tn=TN_WIDE_K, name="b_out_prompt")
            new.setdefault('b_conv', []).append(u.reshape(batch, seq, d)[:, seq - (CONV_W - 1):])
        elif kind == 2:
            head_gain, wg, w_cat, pe8, w1_flat, w2, k_gain0 = _c_weights(prm, li, n_heads)
            z = yield ('norm_linear', x, dict(gain=prm['attn_norm'], gi=i, w=prm['w_c_in'], wi=li, col0=0,
                                              modes=C_MODES, head_gain=head_gain, tn=TN, name="c_in"), (cos, sin))
            gates = yield ('norm_linear', x, dict(gain=prm['attn_norm'], gi=i, w=wg, wi=0, col0=0, modes=(SIGMOID,),
                                                  head_gain=head_gain[:, :TN], tn=TN, name="c_gate"), (cos, sin))
            n_pages = seq // LANES
            pool = z.reshape(batch * n_pages, LANES, z.shape[1])
            kv_width = 2 * C_KV_HEADS * HEAD_DIM
            fs = cmp_first_second(pool, jnp.arange(batch * n_pages, dtype=I32), d // kv_width, w_cat, batch, n_pages)
            kcv = cmp_mlp(fs, cmp_pe_term(pe8, w1_flat), w2, k_gain0)
            nsel = -(-seq // C_SEL_BLOCK)
            overlap = _overlap_matrix(kcv.shape[2], nsel, LANES)
            key_blk = (jnp.arange(seq) // C_SEL_BLOCK).reshape(seq // C_KEY_TILE, 1, C_KEY_TILE)
            expand = (key_blk == jnp.arange(LANES)[None, :, None]).astype(BF16)
            o = nsa_prompt(z, gates, cos, sin, kcv, overlap.T, expand, batch, seq)
            x = linear_residual(o, prm['w_c_out'], li, x, tn=TN, tm=TM_OUT_PROJ, name="c_out_prompt")
            z3 = z.reshape(batch, seq, -1)
            for br, nm in enumerate(('c_cmp', 'c_slc', 'c_win')):
                rows = min(C_WINDOW, seq) if nm == 'c_win' else seq
                kv = z3[:, seq - rows:, d + br * kv_width:d + (br + 1) * kv_width]
                new.setdefault(nm, []).append(kv.reshape(batch, rows, 2, C_KV_HEADS, HEAD_DIM))
        else:
            x, tail = d_mixer_prompt(x, prm['attn_norm'], i, prm['w_d_group'], prm['d_scale'], li, seq,
                                     name="d_mixer_prompt")
            new.setdefault('d_pool', []).append(tail[:, 1:])
        x = yield from _channel_mixer(x, p, prm, i, "prompt")
    return x.reshape(batch, seq, d), {nm: jnp.stack(v, axis=0) for nm, v in new.items()}


def _run_sample(x3, p, prm, past, page_table):
    n_batch, n_new, d = x3.shape
    assert n_new == 1
    n_heads = d // HEAD_DIM
    rows = SUBLANES_BF16
    x = _pad_rows(x3.reshape(n_batch, d), rows)
    past_len = page_table.shape[1] * past['c_cmp'].shape[2]
    cos, sin = _rope_tables(jnp.full((rows,), past_len))
    new = {}
    depth = prm['attn_norm'].shape[0]
    for i in range(depth):
        kind, li = i % 4, i // 4
        if kind == 0:
            z = yield ('norm_linear', x, None, (cos, sin))
            caches =[past[f'a_w{g + 1}'] for g in range(len(A_PATTERNS))]
            o = a_attention_sample(z, caches, li, n_batch)
            x = linear_residual(_pad_rows(o, rows), prm['w_a_out'], li, x, tn=TN, name="a_out_sample")
            for g, cache in enumerate(caches):
                kv = z[:n_batch, (3 * g + 1) * d:(3 * g + 3) * d].reshape(n_batch, 1, 2, n_heads, HEAD_DIM)
                new.setdefault(f'a_w{g + 1}', []).append(shift_append(cache, li, kv, f"a_w{g + 1}_shift"))
        elif kind == 1:
            bg, u = yield ('b_in', x, None)
            hist = past['b_conv'][li]
            x = b_out_sample(u, _pad_rows(hist[:, 1], rows), _pad_rows(hist[:, 0], rows), bg, prm['b_conv'],
                             prm['w_b_out'], li, x, tn=TN, name="b_out_sample")
            new.setdefault('b_conv', []).append(jnp.concatenate([hist[:, 1:], u[:n_batch, None]], axis=1))
        elif kind == 2:
            head_gain, wg, w_cat, pe8, w1_flat, w2, k_gain0 = _c_weights(prm, li, n_heads)
            z = yield ('norm_linear', x, None, (cos, sin))
            gates = yield ('norm_linear', x, None, (cos, sin))
            kv_width = 2 * C_KV_HEADS * HEAD_DIM
            kcv = cmp_mlp(cmp_first_second_paged(past['c_cmp'], li, page_table, w_cat),
                          cmp_pe_term(pe8, w1_flat), w2, k_gain0)
            nsel = -(-(past_len + 1) // C_SEL_BLOCK)
            width = -(-nsel // LANES) * LANES
            overlap = _overlap_matrix(kcv.shape[2], nsel, width)
            zb = z[:n_batch]
            q8 = jnp.pad(zb[:, :d].reshape(n_batch, C_KV_HEADS, C_GROUP, HEAD_DIM),
                         ((0, 0), (0, 0), (0, 8 - C_GROUP), (0, 0)))
            ocmp, idx = nsa_sample_cmp(q8, kcv, overlap, past_len, nsel)
            top_idx = idx[:, :C_KV_HEADS, :C_SEL_TOPN]
            kvh_w = C_KV_HEADS * HEAD_DIM
            new_rows = [zb[:, d + kv_width + j * kvh_w:d + kv_width + (j + 1) * kvh_w]
                        .reshape(n_batch, C_KV_HEADS, 1, HEAD_DIM) for j in range(4)]
            new_kv = jnp.pad(jnp.concatenate(new_rows, axis=2), ((0, 0), (0, 0), (0, 4), (0, 0)))
            gate8 = gates[:n_batch].reshape(n_batch, C_KV_HEADS, LANES)[:, :, :C_GROUP * 3]
            gate8 = jnp.pad(gate8.reshape(n_batch, C_KV_HEADS, C_GROUP, 3),
                            ((0, 0), (0, 0), (0, 8 - C_GROUP), (0, HEAD_DIM - 3)))
            o = nsa_sample(top_idx, page_table, q8, cos[:8], sin[:8], past['c_slc'], past['c_win'], li, new_kv,
                           ocmp, gate8, past_len, past_len // C_SEL_BLOCK)
            o = o[:, :, :C_GROUP].reshape(n_batch, d)
            x = linear_residual(_pad_rows(o, rows), prm['w_c_out'], li, x, tn=TN, name="c_out_sample")
            for br, nm in enumerate(('c_cmp', 'c_slc', 'c_win')):
                kv = zb[:, d + br * kv_width:d + (br + 1) * kv_width].reshape(n_batch, 1, 2, C_KV_HEADS, HEAD_DIM)
                if nm == 'c_win':
                    kv = shift_append(past['c_win'], li, kv, "c_win_shift")
                new.setdefault(nm, []).append(kv)
        else:
            hist = past['d_pool'][li]
            hist_t = jnp.pad(jnp.swapaxes(hist, 0, 1), ((0, 0), (0, rows - n_batch), (0, 0)))
            x, h = d_mixer_sample(x, hist_t, prm['attn_norm'], i, prm['w_d_group'], prm['d_scale'], li,
                                  name="d_mixer_sample")
            new.setdefault('d_pool', []).append(jnp.concatenate([hist[:, 1:], h[:n_batch, None]], axis=1))
        x = yield from _channel_mixer(x, p, prm, i, "sample")
    return x[:n_batch].reshape(n_batch, 1, d), {nm: jnp.stack(v, axis=0) for nm, v in new.items()}


def kernel(x_prompt, x_sample, cache_a_w1, cache_a_w2, cache_a_w3, state_b_conv, cache_c_cmp, cache_c_slc, cache_c_win, state_d_pool, page_table, p_prompt, p_sample, attn_norm, ffn_norm, ple_norm, w_a_qkv, a_q_norm, a_k_norm, w_a_out, w_b_in, b_conv, w_b_out, w_c_in, c_q_norm, c_k_norm, c_cmp_pe, c_cmp_w1, c_cmp_w2, w_c_out, w_d_group, d_scale, w_ffn_in, w_ffn_out, w_ple_proj, w_ple_gate):
    prm = dict(attn_norm=attn_norm, ffn_norm=ffn_norm, ple_norm=ple_norm, w_a_qkv=w_a_qkv, a_q_norm=a_q_norm,
               a_k_norm=a_k_norm, w_a_out=w_a_out, w_b_in=w_b_in, b_conv=b_conv, w_b_out=w_b_out, w_c_in=w_c_in,
               c_q_norm=c_q_norm, c_k_norm=c_k_norm, c_cmp_pe=c_cmp_pe, c_cmp_w1=c_cmp_w1, c_cmp_w2=c_cmp_w2,
               w_c_out=w_c_out, w_d_group=w_d_group, d_scale=d_scale, w_ffn_in=w_ffn_in, w_ffn_out=w_ffn_out,
               w_ple_proj=w_ple_proj, w_ple_gate=w_ple_gate)
    past = dict(a_w1=cache_a_w1, a_w2=cache_a_w2, a_w3=cache_a_w3, b_conv=state_b_conv, c_cmp=cache_c_cmp,
                c_slc=cache_c_slc, c_win=cache_c_win, d_pool=state_d_pool)
    depth = attn_norm.shape[0]
    batch, seq, _ = x_prompt.shape
    n_dec = x_sample.shape[0]
    p_p = p_prompt.reshape(depth, batch * seq, -1)
    p_s = jnp.pad(p_sample.reshape(depth, n_dec, -1), ((0, 0), (0, SUBLANES_BF16 - n_dec), (0, 0)))
    (y_prompt, sp), (y_sample, ss) = _run_both(_run_prompt(x_prompt, p_p, prm),
                                               _run_sample(x_sample, p_s, prm, past, page_table))
    return (y_prompt, y_sample,
            sp['a_w1'], ss['a_w1'], sp['a_w2'], ss['a_w2'], sp['a_w3'], ss['a_w3'],
            sp['b_conv'], ss['b_conv'],
            sp['c_cmp'], ss['c_cmp'], sp['c_slc'], ss['c_slc'], sp['c_win'], ss['c_win'],
            sp['d_pool'], ss['d_pool'])
```
